```python
import jax, jax.numpy as jnp
from jax import lax
import numpy as np


D_MODEL = 1024
BATCH = 4
SEQ = 8192
DEPTH = 2

MIX_WIDTH = D_MODEL
POOL_WIDTH = D_MODEL // 4
SG_WIDTH = D_MODEL // 4
SB_WIDTH = D_MODEL // 2
POOL_WINDOWS = (2, 4, 8, 16)
N_POOL_GROUPS = len(POOL_WINDOWS)
POOL_GW = POOL_WIDTH // N_POOL_GROUPS
CHUNK = 128
SG_HEADS = 4
SG_HD = SG_WIDTH // SG_HEADS
SB_HD = 64
SB_HEADS = SB_WIDTH // SB_HD
Q_BLOCK = 128
IN_COLS = POOL_WIDTH + 2 * SG_WIDTH + 3 * SB_WIDTH
D_FF = 4 * D_MODEL
EPS = 1e-6

kernel_name = "hybrid_pool_sgmlp_stickbreak_block"


def rms_norm(x, g):
    xf = x.astype(jnp.float32)
    y = xf * lax.rsqrt(jnp.mean(xf * xf, axis=-1, keepdims=True) + EPS)
    return (y * g.astype(jnp.float32)).astype(x.dtype)


def pool_mixer(h, w_grp, scale):
    B, S, _ = h.shape
    hf = h.astype(jnp.float32)
    cs = jnp.cumsum(hf, axis=1)
    t = jnp.arange(S)
    pooled = []
    for gi, w in enumerate(POOL_WINDOWS):
        c = cs[..., gi * POOL_GW:(gi + 1) * POOL_GW]
        lag = jnp.pad(c, ((0, 0), (w, 0), (0, 0)))[:, :S]
        cnt = jnp.minimum(t + 1, w).astype(jnp.float32)[None, :, None]
        pooled.append((c - lag) / cnt)
    d = (jnp.concatenate(pooled, axis=-1) - hf).astype(h.dtype)
    d = d.reshape(B, S, N_POOL_GROUPS, POOL_GW)
    y = jnp.einsum('bsgc,gcd->bsgd', d, w_grp).reshape(B, S, POOL_WIDTH)
    return y * scale


def spatial_gate(z, g_norm, w_s, b_s):
    B, S, _ = z.shape
    u, v = z[..., :SG_WIDTH], z[..., SG_WIDTH:]
    v = rms_norm(v, g_norm)
    v = v.reshape(B, S // CHUNK, CHUNK, SG_HEADS, SG_HD)
    mask = jnp.tril(jnp.ones((CHUNK, CHUNK), dtype=w_s.dtype))
    sv = jnp.einsum('hts,bnshc->bnthc', w_s * mask, v)
    sv = sv + b_s.T[None, None, :, :, None]
    return u * sv.reshape(B, S, SG_WIDTH)


def stick_breaking_attention(q, k, v):
    B, S, _ = q.shape
    nb = S // Q_BLOCK
    q = q.reshape(B, S, SB_HEADS, SB_HD).transpose(0, 2, 1, 3)
    k = k.reshape(B, S, SB_HEADS, SB_HD).transpose(0, 2, 1, 3)
    v = v.reshape(B, S, SB_HEADS, SB_HD).transpose(0, 2, 1, 3)
    qb = q.reshape(B, SB_HEADS, nb, Q_BLOCK, SB_HD).transpose(2, 0, 1, 3, 4)
    inv_sqrt_d = 1.0 / np.sqrt(SB_HD).astype(np.float32)
    tk = jnp.arange(S)

    def block(args):
        i, qi = args
        z = jnp.einsum('bhqd,bhkd->bhqk', qi, k).astype(jnp.float32) * inv_sqrt_d
        tq = i * Q_BLOCK + jnp.arange(Q_BLOCK)
        causal = (tk[None, :] < tq[:, None])[None, None]
        log_beta = jax.nn.log_sigmoid(z)
        log_1m = jnp.where(causal, jax.nn.log_sigmoid(-z), 0.0)
        after = lax.cumsum(log_1m, axis=3, reverse=True) - log_1m
        a = jnp.where(causal, jnp.exp(log_beta + after), 0.0)
        return jnp.einsum('bhqk,bhkd->bhqd', a.astype(v.dtype), v)

    out = lax.map(block, (jnp.arange(nb), qb))
    return out.transpose(1, 0, 3, 2, 4).reshape(B, S, SB_WIDTH)


def setup_inputs(seed: int = 0) -> dict:
    key = jax.random.key(seed)
    ks = jax.random.split(key, 16)
    f32 = jnp.float32
    nrm = lambda k, shape, s: jax.random.normal(k, shape, f32) * s
    return {
        "x": jax.random.normal(ks[0], (BATCH, SEQ, D_MODEL), f32),
        "norm1": 1.0 + nrm(ks[1], (DEPTH, D_MODEL), 0.02),
        "w_in": nrm(ks[2], (DEPTH, D_MODEL, IN_COLS), D_MODEL ** -0.5),
        "pool_w": nrm(ks[3], (DEPTH, N_POOL_GROUPS, POOL_GW, POOL_GW), POOL_GW ** -0.5),
        "pool_scale": 1.0 + nrm(ks[4], (DEPTH, POOL_WIDTH), 0.02),
        "sg_norm": 1.0 + nrm(ks[5], (DEPTH, SG_WIDTH), 0.02),
        "sg_w": nrm(ks[6], (DEPTH, SG_HEADS, CHUNK, CHUNK), 0.5 * CHUNK ** -0.5),
        "sg_b": 1.0 + nrm(ks[7], (DEPTH, SG_HEADS, CHUNK), 0.1),
        "w_out": nrm(ks[8], (DEPTH, MIX_WIDTH, D_MODEL), MIX_WIDTH ** -0.5),
        "norm2": 1.0 + nrm(ks[9], (DEPTH, D_MODEL), 0.02),
        "w_up": nrm(ks[10], (DEPTH, D_MODEL, D_FF), D_MODEL ** -0.5),
        "w_down": nrm(ks[11], (DEPTH, D_FF, D_MODEL), 0.5 * D_FF ** -0.5),
        "final_norm": 1.0 + nrm(ks[12], (D_MODEL,), 0.02),
    }


def reference(x, norm1, w_in, pool_w, pool_scale, sg_norm, sg_w, sg_b, w_out, norm2, w_up, w_down, final_norm):
    c1 = POOL_WIDTH
    c2 = c1 + 2 * SG_WIDTH
    c3 = c2 + SB_WIDTH
    c4 = c3 + SB_WIDTH
    for l in range(DEPTH):
        h = rms_norm(x, norm1[l])
        proj = h @ w_in[l]
        a_in, b_in = proj[..., :c1], proj[..., c1:c2]
        q, k, v = proj[..., c2:c3], proj[..., c3:c4], proj[..., c4:]
        ya = pool_mixer(a_in, pool_w[l], pool_scale[l])
        yb = spatial_gate(jax.nn.gelu(b_in), sg_norm[l], sg_w[l], sg_b[l])
        yc = stick_breaking_attention(q, k, v)
        x = x + jnp.concatenate([ya, yb, yc], axis=-1) @ w_out[l]
        h = rms_norm(x, norm2[l])
        x = x + jnp.square(jax.nn.relu(h @ w_up[l])) @ w_down[l]
    return rms_norm(x, final_norm)
```

```python
import functools

import jax
import jax.numpy as jnp
import numpy as np
from jax import lax
from jax.experimental import pallas as pl
from jax.experimental.pallas import tpu as pltpu

EPS = 1e-6
POOL_WINDOWS = (2, 4, 8, 16)
POOL_HALO = 16
CHUNK = 128
SG_HEADS = 4
SB_HD = 64
LANES = 128

ROW_TILE = 512
ATT_BLOCK = 256
STICK_CUTOFF = 30.0
VMEM_LIMIT = 56 * 1024 * 1024

F32 = jnp.float32
BF16 = jnp.bfloat16


def _rms(x, g):
    ms = jnp.mean(x * x, axis=-1, keepdims=True)
    return x * lax.rsqrt(ms + EPS) * g


def _gelu_tanh(x):
    c = np.float32(np.sqrt(2.0 / np.pi))
    return 0.5 * x * (1.0 + jnp.tanh(c * (x + 0.044715 * (x * x * x))))


def _inproj_kernel(x_ref, g1_ref, win_ref, poolw_ref, pscale_ref, sgn_ref, sgw_ref, sgb_ref,
                   yab_ref, q_ref, k_ref, v_ref, lvl_ref, *, tiles_per_seq):
    T = x_ref.shape[0]
    pw = pscale_ref.shape[1]
    sw = sgn_ref.shape[1]
    i = pl.program_id(0)
    tile_in_seq = i % tiles_per_seq

    h = _rms(x_ref[...], g1_ref[...]).astype(BF16)
    a = jnp.dot(h, win_ref[:, 0:pw], preferred_element_type=F32)
    bpre = jnp.dot(h, win_ref[:, pw:pw + 2 * sw], preferred_element_type=F32)
    c2 = pw + 2 * sw
    sbw = q_ref.shape[1]
    q_ref[...] = jnp.dot(h, win_ref[:, c2:c2 + sbw], preferred_element_type=F32).astype(BF16)
    k_ref[...] = jnp.dot(h, win_ref[:, c2 + sbw:c2 + 2 * sbw], preferred_element_type=F32).astype(BF16)
    v_ref[...] = jnp.dot(h, win_ref[:, c2 + 2 * sbw:c2 + 3 * sbw], preferred_element_type=F32).astype(BF16)

    H = POOL_HALO
    lo, n = H, T + H

    @pl.when(i == 0)
    def _():
        lvl_ref[:, 0:H, :] = jnp.zeros((lvl_ref.shape[0], H, pw), F32)

    @pl.when(tile_in_seq == 0)
    def _():
        lvl_ref[0, H:2 * H, :] = jnp.zeros((H, pw), F32)

    @pl.when(tile_in_seq != 0)
    def _():
        lvl_ref[0, H:2 * H, :] = lvl_ref[0, T + H:T + 2 * H, :]

    lvl_ref[0, 2 * H:2 * H + T, :] = a
    for s, shift in enumerate((1, 2, 4, 8)):
        lvl_ref[s + 1, lo:lo + n, :] = (lvl_ref[s, lo:lo + n, :]
                                        + lvl_ref[s, lo - shift:lo - shift + n, :])
    lane = lax.broadcasted_iota(jnp.int32, (1, pw), 1)
    grp = lane // (pw // len(POOL_WINDOWS))
    win = jnp.where(grp == 0, 2, jnp.where(grp == 1, 4, jnp.where(grp == 2, 8, 16)))
    psum = jnp.where(grp == 0, lvl_ref[1, 2 * H:2 * H + T, :],
                     jnp.where(grp == 1, lvl_ref[2, 2 * H:2 * H + T, :],
                               jnp.where(grp == 2, lvl_ref[3, 2 * H:2 * H + T, :],
                                         lvl_ref[4, 2 * H:2 * H + T, :])))
    pos = tile_in_seq * T + lax.broadcasted_iota(jnp.int32, (T, 1), 0)
    cnt = jnp.minimum(pos + 1, win).astype(F32)
    d = psum / cnt - a
    ya = jnp.dot(d.astype(BF16), poolw_ref[...], preferred_element_type=F32) * pscale_ref[...]
    yab_ref[:, 0:pw] = ya.astype(BF16)

    bz = _gelu_tanh(bpre)
    u = bz[:, 0:sw]
    vn = _rms(bz[:, sw:2 * sw], sgn_ref[...]).astype(BF16)
    r_i = lax.broadcasted_iota(jnp.int32, (CHUNK, CHUNK), 0)
    c_i = lax.broadcasted_iota(jnp.int32, (CHUNK, CHUNK), 1)
    tril = c_i <= r_i
    wm = [jnp.where(tril, sgw_ref[hh], 0.0).astype(BF16) for hh in range(SG_HEADS)]
    head = lax.broadcasted_iota(jnp.int32, (1, sw), 1) // (sw // SG_HEADS)
    for c in range(T // CHUNK):
        vc = vn[c * CHUNK:(c + 1) * CHUNK, :]
        sv = sgb_ref[...]
        for hh in range(SG_HEADS):
            r = jnp.dot(wm[hh], vc, preferred_element_type=F32)
            sv = sv + jnp.where(head == hh, r, 0.0)
        yab_ref[c * CHUNK:(c + 1) * CHUNK, pw:pw + sw] = (u[c * CHUNK:(c + 1) * CHUNK, :] * sv).astype(BF16)


def _inproj(x2, g1, win, poolw, pscale, sgn, sgw, sgb, *, seq):
    N, D = x2.shape
    T = ROW_TILE
    pw, sw = pscale.shape[1], sgn.shape[1]
    sbw = (win.shape[1] - pw - 2 * sw) // 3
    const = lambda i: (0, 0)
    row = lambda i: (i, 0)
    return pl.pallas_call(
        functools.partial(_inproj_kernel, tiles_per_seq=seq // T),
        grid=(N // T,),
        in_specs=[
            pl.BlockSpec((T, D), row),
            pl.BlockSpec((1, D), const),
            pl.BlockSpec(win.shape, const),
            pl.BlockSpec(poolw.shape, const),
            pl.BlockSpec((1, pw), const),
            pl.BlockSpec((1, sw), const),
            pl.BlockSpec(sgw.shape, lambda i: (0, 0, 0)),
            pl.BlockSpec(sgb.shape, const),
        ],
        out_specs=[
            pl.BlockSpec((T, pw + sw), row),
            pl.BlockSpec((T, sbw), row),
            pl.BlockSpec((T, sbw), row),
            pl.BlockSpec((T, sbw), row),
        ],
        out_shape=[
            jax.ShapeDtypeStruct((N, pw + sw), BF16),
            jax.ShapeDtypeStruct((N, sbw), BF16),
            jax.ShapeDtypeStruct((N, sbw), BF16),
            jax.ShapeDtypeStruct((N, sbw), BF16),
        ],
        scratch_shapes=[pltpu.VMEM((len(POOL_WINDOWS) + 1, T + 2 * POOL_HALO, pw), F32)],
        compiler_params=pltpu.CompilerParams(
            dimension_semantics=("arbitrary",), vmem_limit_bytes=VMEM_LIMIT),
        name="inproj_mixers",
    )(x2, g1, win, poolw, pscale, sgn, sgw, sgb)


def _attn_kernel(q_ref, k_ref, v_ref, tri_ref, o_ref, acc_ref, carry_ref):
    BQ = q_ref.shape[0]
    BK = BQ
    qi = pl.program_id(2)
    lane = lax.broadcasted_iota(jnp.int32, (1, LANES), 1)
    row = lax.broadcasted_iota(jnp.int32, (BQ, BK), 0)
    col = lax.broadcasted_iota(jnp.int32, (BQ, BK), 1)
    causal = col < row
    tri = tri_ref[...]

    def tile(qh, j, masked):
        ks = pl.multiple_of(j * BK, BK)
        kj = k_ref[pl.ds(ks, BK), :]
        vj = v_ref[pl.ds(ks, BK), :]
        z = lax.dot_general(qh, kj, (((1,), (1,)), ((), ())), preferred_element_type=F32)
        sp = jnp.log(1.0 + jnp.exp(-jnp.abs(z)))
        log_beta = jnp.minimum(z, 0.0) - sp
        log_1m = log_beta - z
        if masked:
            log_1m = jnp.where(causal, log_1m, 0.0)
        hi = log_1m.astype(BF16)
        lo = (log_1m - hi.astype(F32)).astype(BF16)
        after = (jnp.dot(hi, tri, preferred_element_type=F32)
                 + jnp.dot(lo, tri, preferred_element_type=F32))
        a = jnp.exp(log_beta + after)
        if masked:
            a = jnp.where(causal, a, 0.0)
        pv = jnp.dot(a.astype(BF16), vj, preferred_element_type=F32)
        carry = carry_ref[...]
        acc_ref[...] += jnp.exp(carry) * pv
        carry = carry + jnp.sum(log_1m, axis=-1, keepdims=True)
        carry_ref[...] = carry
        return jnp.max(carry)

    outs = []
    for hs in range(LANES // SB_HD):
        hmask = (lane // SB_HD) == hs
        qh = (jnp.where(hmask, q_ref[...], 0.0) * float(SB_HD ** -0.5)).astype(BF16)
        acc_ref[...] = jnp.zeros_like(acc_ref)
        carry_ref[...] = jnp.zeros_like(carry_ref)
        cmax = tile(qh, qi, True)

        def cond(st):
            j, cm = st
            return jnp.logical_and(j >= 0, cm > -STICK_CUTOFF)

        def body(st, qh=qh):
            j, _ = st
            return j - 1, tile(qh, j, False)

        lax.while_loop(cond, body, (qi - 1, cmax))
        outs.append((hmask, acc_ref[...]))
    o_ref[...] = jnp.where(outs[0][0], outs[0][1], outs[1][1]).astype(o_ref.dtype)


def _attention(q, k, v, tri, *, batch, seq):
    N, W = q.shape
    BQ = ATT_BLOCK
    nq = seq // BQ
    q3, k3, v3 = (t.reshape(batch, seq, W) for t in (q, k, v))
    out = pl.pallas_call(
        _attn_kernel,
        grid=(batch, W // LANES, nq),
        in_specs=[
            pl.BlockSpec((None, BQ, LANES), lambda b, hp, i: (b, i, hp)),
            pl.BlockSpec((None, seq, LANES), lambda b, hp, i: (b, 0, hp)),
            pl.BlockSpec((None, seq, LANES), lambda b, hp, i: (b, 0, hp)),
            pl.BlockSpec((BQ, BQ), lambda b, hp, i: (0, 0)),
        ],
        out_specs=pl.BlockSpec((None, BQ, LANES), lambda b, hp, i: (b, i, hp)),
        out_shape=jax.ShapeDtypeStruct((batch, seq, W), BF16),
        scratch_shapes=[pltpu.VMEM((BQ, LANES), F32), pltpu.VMEM((BQ, 1), F32)],
        compiler_params=pltpu.CompilerParams(
            dimension_semantics=("arbitrary", "arbitrary", "arbitrary"),
            vmem_limit_bytes=VMEM_LIMIT),
        name="stickbreak_attn",
    )(q3, k3, v3, tri)
    return out.reshape(N, W)


def _mlp_kernel(x_ref, yab_ref, yc_ref, g2_ref, gf_ref, wo_hbm, wup_hbm, wdn_hbm, o_ref,
                wo_ref, wup_ref, wdn_ref, sem, *, final_norm, ff_chunk):
    @pl.when(pl.program_id(0) == 0)
    def _():
        copies = [pltpu.make_async_copy(src, dst, sem.at[n])
                  for n, (src, dst) in enumerate(((wo_hbm, wo_ref), (wup_hbm, wup_ref), (wdn_hbm, wdn_ref)))]
        for c in copies:
            c.start()
        for c in copies:
            c.wait()

    nab = yab_ref.shape[1]
    x1 = (x_ref[...]
          + jnp.dot(yab_ref[...], wo_ref[0:nab, :], preferred_element_type=F32)
          + jnp.dot(yc_ref[...], wo_ref[nab:, :], preferred_element_type=F32))
    h = _rms(x1, g2_ref[...]).astype(BF16)
    acc = None
    for c in range(wup_ref.shape[1] // ff_chunk):
        up = jnp.dot(h, wup_ref[:, c * ff_chunk:(c + 1) * ff_chunk], preferred_element_type=F32)
        act = jnp.square(jnp.maximum(up, 0.0)).astype(BF16)
        dn = jnp.dot(act, wdn_ref[c * ff_chunk:(c + 1) * ff_chunk, :], preferred_element_type=F32)
        acc = dn if acc is None else acc + dn
    acc = acc + x1
    if final_norm:
        acc = _rms(acc, gf_ref[...])
    o_ref[...] = acc


def _mlp(x2, yab, yc, wo, g2, wup, wdn, gf, *, final_norm):
    N, D = x2.shape
    T = ROW_TILE
    const = lambda i: (0, 0)
    row = lambda i: (i, 0)
    hbm = pl.BlockSpec(memory_space=pl.ANY)
    return pl.pallas_call(
        functools.partial(_mlp_kernel, final_norm=final_norm, ff_chunk=1024),
        grid=(N // T,),
        in_specs=[
            pl.BlockSpec((T, D), row),
            pl.BlockSpec((T, yab.shape[1]), row),
            pl.BlockSpec((T, yc.shape[1]), row),
            pl.BlockSpec((1, D), const),
            pl.BlockSpec((1, D), const),
            hbm, hbm, hbm,
        ],
        out_specs=pl.BlockSpec((T, D), row),
        out_shape=jax.ShapeDtypeStruct((N, D), F32),
        scratch_shapes=[pltpu.VMEM(wo.shape, BF16), pltpu.VMEM(wup.shape, BF16),
                        pltpu.VMEM(wdn.shape, BF16), pltpu.SemaphoreType.DMA((3,))],
        compiler_params=pltpu.CompilerParams(
            dimension_semantics=("arbitrary",), vmem_limit_bytes=VMEM_LIMIT),
        name="outproj_mlp",
    )(x2, yab, yc, g2, gf, wo, wup, wdn)


def kernel(x, norm1, w_in, pool_w, pool_scale, sg_norm, sg_w, sg_b, w_out, norm2, w_up, w_down, final_norm):
    B, S, D = x.shape
    depth = norm1.shape[0]
    n_grp, gw = pool_w.shape[1], pool_w.shape[2]
    sw = sg_norm.shape[1]
    assert S % ROW_TILE == 0 and S % ATT_BLOCK == 0 and ROW_TILE % CHUNK == 0
    assert sg_w.shape[1] == SG_HEADS and sg_w.shape[2] == CHUNK and n_grp == len(POOL_WINDOWS)

    tri = (np.arange(ATT_BLOCK)[:, None] > np.arange(ATT_BLOCK)[None, :])
    tri = jnp.asarray(tri, BF16)

    x2 = x.reshape(B * S, D)
    for l in range(depth):
        win = w_in[l].astype(BF16)
        poolw = jax.scipy.linalg.block_diag(*[pool_w[l, g] for g in range(n_grp)]).astype(BF16)
        sgb = jnp.repeat(sg_b[l].T, sw // SG_HEADS, axis=1)
        yab, q, k, v = _inproj(x2, norm1[l][None], win, poolw, pool_scale[l][None],
                               sg_norm[l][None], sg_w[l], sgb, seq=S)
        yc = _attention(q, k, v, tri, batch=B, seq=S)
        x2 = _mlp(x2, yab, yc, w_out[l].astype(BF16), norm2[l][None],
                  w_up[l].astype(BF16), w_down[l].astype(BF16), final_norm[None],
                  final_norm=(l == depth - 1))
    return x2.reshape(B, S, D)
```

```python
import functools

import jax
import jax.numpy as jnp
import numpy as np
from jax import lax
from jax.experimental import pallas as pl
from jax.experimental.pallas import tpu as pltpu

EPS = 1e-6
POOL_WINDOWS = (2, 4, 8, 16)
POOL_HALO = 16
CHUNK = 128
SG_HEADS = 4
SB_HD = 64
LANES = 128

ROW_TILE = 512
ATT_BLOCK = 256
STICK_CUTOFF = 30.0
VMEM_LIMIT = 56 * 1024 * 1024

F32 = jnp.float32
BF16 = jnp.bfloat16


def _rms(x, g):
    ms = jnp.mean(x * x, axis=-1, keepdims=True)
    return x * lax.rsqrt(ms + EPS) * g


def _gelu_tanh(x):
    c = np.float32(np.sqrt(2.0 / np.pi))
    return 0.5 * x * (1.0 + jnp.tanh(c * (x + 0.044715 * (x * x * x))))


def _inproj_kernel(x_ref, g1_ref, win_ref, poolw_ref, pscale_ref, sgn_ref, sgw_ref, sgb_ref,
                   yab_ref, q_ref, k_ref, v_ref, lvl_ref, *, tiles_per_seq):
    T = x_ref.shape[0]
    pw = pscale_ref.shape[1]
    sw = sgn_ref.shape[1]
    i = pl.program_id(0)
    tile_in_seq = i % tiles_per_seq

    h = _rms(x_ref[...], g1_ref[...]).astype(BF16)
    a = jnp.dot(h, win_ref[:, 0:pw], preferred_element_type=F32)
    bpre = jnp.dot(h, win_ref[:, pw:pw + 2 * sw], preferred_element_type=F32)
    c2 = pw + 2 * sw
    sbw = q_ref.shape[1]
    q_ref[...] = jnp.dot(h, win_ref[:, c2:c2 + sbw], preferred_element_type=F32).astype(BF16)
    k_ref[...] = jnp.dot(h, win_ref[:, c2 + sbw:c2 + 2 * sbw], preferred_element_type=F32).astype(BF16)
    v_ref[...] = jnp.dot(h, win_ref[:, c2 + 2 * sbw:c2 + 3 * sbw], preferred_element_type=F32).astype(BF16)

    H = POOL_HALO
    lo, n = H, T + H

    @pl.when(i == 0)
    def _():
        lvl_ref[:, 0:H, :] = jnp.zeros((lvl_ref.shape[0], H, pw), F32)

    @pl.when(tile_in_seq == 0)
    def _():
        lvl_ref[0, H:2 * H, :] = jnp.zeros((H, pw), F32)

    @pl.when(tile_in_seq != 0)
    def _():
        lvl_ref[0, H:2 * H, :] = lvl_ref[0, T + H:T + 2 * H, :]

    lvl_ref[0, 2 * H:2 * H + T, :] = a
    for s, shift in enumerate((1, 2, 4, 8)):
        lvl_ref[s + 1, lo:lo + n, :] = (lvl_ref[s, lo:lo + n, :]
                                        + lvl_ref[s, lo - shift:lo - shift + n, :])
    lane = lax.broadcasted_iota(jnp.int32, (1, pw), 1)
    grp = lane // (pw // len(POOL_WINDOWS))
    win = jnp.where(grp == 0, 2, jnp.where(grp == 1, 4, jnp.where(grp == 2, 8, 16)))
    psum = jnp.where(grp == 0, lvl_ref[1, 2 * H:2 * H + T, :],
                     jnp.where(grp == 1, lvl_ref[2, 2 * H:2 * H + T, :],
                               jnp.where(grp == 2, lvl_ref[3, 2 * H:2 * H + T, :],
                                         lvl_ref[4, 2 * H:2 * H + T, :])))
    pos = tile_in_seq * T + lax.broadcasted_iota(jnp.int32, (T, 1), 0)
    cnt = jnp.minimum(pos + 1, win).astype(F32)
    d = psum / cnt - a
    ya = jnp.dot(d.astype(BF16), poolw_ref[...], preferred_element_type=F32) * pscale_ref[...]
    yab_ref[:, 0:pw] = ya.astype(BF16)

    bz = _gelu_tanh(bpre)
    u = bz[:, 0:sw]
    vn = _rms(bz[:, sw:2 * sw], sgn_ref[...]).astype(BF16)
    r_i = lax.broadcasted_iota(jnp.int32, (CHUNK, CHUNK), 0)
    c_i = lax.broadcasted_iota(jnp.int32, (CHUNK, CHUNK), 1)
    tril = c_i <= r_i
    wm = [jnp.where(tril, sgw_ref[hh], 0.0).astype(BF16) for hh in range(SG_HEADS)]
    head = lax.broadcasted_iota(jnp.int32, (1, sw), 1) // (sw // SG_HEADS)
    for c in range(T // CHUNK):
        vc = vn[c * CHUNK:(c + 1) * CHUNK, :]
        sv = sgb_ref[...]
        for hh in range(SG_HEADS):
            r = jnp.dot(wm[hh], vc, preferred_element_type=F32)
            sv = sv + jnp.where(head == hh, r, 0.0)
        yab_ref[c * CHUNK:(c + 1) * CHUNK, pw:pw + sw] = (u[c * CHUNK:(c + 1) * CHUNK, :] * sv).astype(BF16)


def _inproj(x2, g1, win, poolw, pscale, sgn, sgw, sgb, *, seq):
    N, D = x2.shape
    T = ROW_TILE
    pw, sw = pscale.shape[1], sgn.shape[1]
    sbw = (win.shape[1] - pw - 2 * sw) // 3
    const = lambda i: (0, 0)
    row = lambda i: (i, 0)
    return pl.pallas_call(
        functools.partial(_inproj_kernel, tiles_per_seq=seq // T),
        grid=(N // T,),
        in_specs=[
            pl.BlockSpec((T, D), row),
            pl.BlockSpec((1, D), const),
            pl.BlockSpec(win.shape, const),
            pl.BlockSpec(poolw.shape, const),
            pl.BlockSpec((1, pw), const),
            pl.BlockSpec((1, sw), const),
            pl.BlockSpec(sgw.shape, lambda i: (0, 0, 0)),
            pl.BlockSpec(sgb.shape, const),
        ],
        out_specs=[
            pl.BlockSpec((T, pw + sw), row),
            pl.BlockSpec((T, sbw), row),
            pl.BlockSpec((T, sbw), row),
            pl.BlockSpec((T, sbw), row),
        ],
        out_shape=[
            jax.ShapeDtypeStruct((N, pw + sw), BF16),
            jax.ShapeDtypeStruct((N, sbw), BF16),
            jax.ShapeDtypeStruct((N, sbw), BF16),
            jax.ShapeDtypeStruct((N, sbw), BF16),
        ],
        scratch_shapes=[pltpu.VMEM((len(POOL_WINDOWS) + 1, T + 2 * POOL_HALO, pw), F32)],
        compiler_params=pltpu.CompilerParams(
            dimension_semantics=("arbitrary",), vmem_limit_bytes=VMEM_LIMIT),
        name="inproj_mixers",
    )(x2, g1, win, poolw, pscale, sgn, sgw, sgb)


def _attn_kernel(q_ref, k_ref, v_ref, tri2_ref, o_ref, acc_ref, carry_ref):
    BQ = q_ref.shape[0]
    BK = BQ
    NH = LANES // SB_HD
    qi = pl.program_id(2)
    lane = lax.broadcasted_iota(jnp.int32, (1, LANES), 1)
    row = lax.broadcasted_iota(jnp.int32, (BQ, BK), 0)
    col = lax.broadcasted_iota(jnp.int32, (BQ, BK), 1)
    causal = col < row
    tri2 = tri2_ref[...]
    hmasks = [(lane // SB_HD) == hs for hs in range(NH)]
    qhs = [(jnp.where(hm, q_ref[...], 0.0) * float(SB_HD ** -0.5)).astype(BF16) for hm in hmasks]

    def tile(qh, j, masked):
        ks = pl.multiple_of(j * BK, BK)
        kj = k_ref[pl.ds(ks, BK), :]
        vj = v_ref[pl.ds(ks, BK), :]
        z = lax.dot_general(qh, kj, (((1,), (1,)), ((), ())), preferred_element_type=F32)
        sp = jnp.log(1.0 + jnp.exp(-jnp.abs(z)))
        log_beta = jnp.minimum(z, 0.0) - sp
        log_1m = log_beta - z
        if masked:
            log_1m = jnp.where(causal, log_1m, 0.0)
        hi = log_1m.astype(BF16)
        lo = (log_1m - hi.astype(F32)).astype(BF16)
        after = jnp.dot(jnp.concatenate([hi, lo], axis=1), tri2, preferred_element_type=F32)
        a = jnp.exp(log_beta + after)
        if masked:
            a = jnp.where(causal, a, 0.0)
        pv = jnp.dot(a.astype(BF16), vj, preferred_element_type=F32)
        return pv, jnp.sum(log_1m, axis=-1, keepdims=True)

    @pl.when(qi == 0)
    def _():
        for hs in range(NH):
            acc_ref[hs], carry_ref[hs] = tile(qhs[hs], qi, True)

    @pl.when(qi > 0)
    def _():
        parts = [(tile(qhs[hs], qi, True), tile(qhs[hs], qi - 1, False)) for hs in range(NH)]
        for hs, ((pv_d, rs_d), (pv_p, rs_p)) in enumerate(parts):
            acc_ref[hs] = pv_d + jnp.exp(rs_d) * pv_p
            carry_ref[hs] = rs_d + rs_p

    def cond(st):
        j, cmax = st
        return jnp.logical_and(j >= 0, cmax > -STICK_CUTOFF)

    def body(st):
        j, _ = st
        cmax = None
        for hs in range(NH):
            pv, rs = tile(qhs[hs], j, False)
            carry = carry_ref[hs]
            acc_ref[hs] += jnp.exp(carry) * pv
            carry_ref[hs] = carry + rs
            cm = jnp.max(carry + rs)
            cmax = cm if cmax is None else jnp.maximum(cmax, cm)
        return j - 1, cmax

    cmax0 = jnp.maximum(jnp.max(carry_ref[0]), jnp.max(carry_ref[1]))
    lax.while_loop(cond, body, (qi - 2, cmax0))
    o_ref[...] = jnp.where(hmasks[0], acc_ref[0], acc_ref[1]).astype(o_ref.dtype)


def _attention(q, k, v, tri, *, batch, seq):
    N, W = q.shape
    BQ = ATT_BLOCK
    nq = seq // BQ
    q3, k3, v3 = (t.reshape(batch, seq, W) for t in (q, k, v))
    out = pl.pallas_call(
        _attn_kernel,
        grid=(batch, W // LANES, nq),
        in_specs=[
            pl.BlockSpec((None, BQ, LANES), lambda b, hp, i: (b, i, hp)),
            pl.BlockSpec((None, seq, LANES), lambda b, hp, i: (b, 0, hp)),
            pl.BlockSpec((None, seq, LANES), lambda b, hp, i: (b, 0, hp)),
            pl.BlockSpec((2 * BQ, BQ), lambda b, hp, i: (0, 0)),
        ],
        out_specs=pl.BlockSpec((None, BQ, LANES), lambda b, hp, i: (b, i, hp)),
        out_shape=jax.ShapeDtypeStruct((batch, seq, W), BF16),
        scratch_shapes=[pltpu.VMEM((LANES // SB_HD, BQ, LANES), F32),
                        pltpu.VMEM((LANES // SB_HD, BQ, 1), F32)],
        compiler_params=pltpu.CompilerParams(
            dimension_semantics=("arbitrary", "arbitrary", "arbitrary"),
            vmem_limit_bytes=VMEM_LIMIT),
        name="stickbreak_attn",
    )(q3, k3, v3, tri)
    return out.reshape(N, W)


def _mlp_kernel(x_ref, yab_ref, yc_ref, g2_ref, gf_ref, wo_hbm, wup_hbm, wdn_hbm, o_ref,
                wo_ref, wup_ref, wdn_ref, sem, *, final_norm, ff_chunk):
    @pl.when(pl.program_id(0) == 0)
    def _():
        copies = [pltpu.make_async_copy(src, dst, sem.at[n])
                  for n, (src, dst) in enumerate(((wo_hbm, wo_ref), (wup_hbm, wup_ref), (wdn_hbm, wdn_ref)))]
        for c in copies:
            c.start()
        for c in copies:
            c.wait()

    nab = yab_ref.shape[1]
    x1 = (x_ref[...]
          + jnp.dot(yab_ref[...], wo_ref[0:nab, :], preferred_element_type=F32)
          + jnp.dot(yc_ref[...], wo_ref[nab:, :], preferred_element_type=F32))
    h = _rms(x1, g2_ref[...]).astype(BF16)
    acc = None
    for c in range(wup_ref.shape[1] // ff_chunk):
        up = jnp.dot(h, wup_ref[:, c * ff_chunk:(c + 1) * ff_chunk], preferred_element_type=F32)
        act = jnp.square(jnp.maximum(up, 0.0)).astype(BF16)
        dn = jnp.dot(act, wdn_ref[c * ff_chunk:(c + 1) * ff_chunk, :], preferred_element_type=F32)
        acc = dn if acc is None else acc + dn
    acc = acc + x1
    if final_norm:
        acc = _rms(acc, gf_ref[...])
    o_ref[...] = acc


def _mlp(x2, yab, yc, wo, g2, wup, wdn, gf, *, final_norm):
    N, D = x2.shape
    T = ROW_TILE
    const = lambda i: (0, 0)
    row = lambda i: (i, 0)
    hbm = pl.BlockSpec(memory_space=pl.ANY)
    return pl.pallas_call(
        functools.partial(_mlp_kernel, final_norm=final_norm, ff_chunk=1024),
        grid=(N // T,),
        in_specs=[
            pl.BlockSpec((T, D), row),
            pl.BlockSpec((T, yab.shape[1]), row),
            pl.BlockSpec((T, yc.shape[1]), row),
            pl.BlockSpec((1, D), const),
            pl.BlockSpec((1, D), const),
            hbm, hbm, hbm,
        ],
        out_specs=pl.BlockSpec((T, D), row),
        out_shape=jax.ShapeDtypeStruct((N, D), F32),
        scratch_shapes=[pltpu.VMEM(wo.shape, BF16), pltpu.VMEM(wup.shape, BF16),
                        pltpu.VMEM(wdn.shape, BF16), pltpu.SemaphoreType.DMA((3,))],
        compiler_params=pltpu.CompilerParams(
            dimension_semantics=("arbitrary",), vmem_limit_bytes=VMEM_LIMIT),
        name="outproj_mlp",
    )(x2, yab, yc, g2, gf, wo, wup, wdn)


def kernel(x, norm1, w_in, pool_w, pool_scale, sg_norm, sg_w, sg_b, w_out, norm2, w_up, w_down, final_norm):
    B, S, D = x.shape
    depth = norm1.shape[0]
    n_grp, gw = pool_w.shape[1], pool_w.shape[2]
    sw = sg_norm.shape[1]
    assert S % ROW_TILE == 0 and S % ATT_BLOCK == 0 and ROW_TILE % CHUNK == 0
    assert sg_w.shape[1] == SG_HEADS and sg_w.shape[2] == CHUNK and n_grp == len(POOL_WINDOWS)

    tri = (np.arange(ATT_BLOCK)[:, None] > np.arange(ATT_BLOCK)[None, :])
    tri = jnp.asarray(np.concatenate([tri, tri], axis=0), BF16)

    x2 = x.reshape(B * S, D)
    for l in range(depth):
        win = w_in[l].astype(BF16)
        poolw = jax.scipy.linalg.block_diag(*[pool_w[l, g] for g in range(n_grp)]).astype(BF16)
        sgb = jnp.repeat(sg_b[l].T, sw // SG_HEADS, axis=1)
        yab, q, k, v = _inproj(x2, norm1[l][None], win, poolw, pool_scale[l][None],
                               sg_norm[l][None], sg_w[l], sgb, seq=S)
        yc = _attention(q, k, v, tri, batch=B, seq=S)
        x2 = _mlp(x2, yab, yc, w_out[l].astype(BF16), norm2[l][None],
                  w_up[l].astype(BF16), w_down[l].astype(BF16), final_norm[None],
                  final_norm=(l == depth - 1))
    return x2.reshape(B, S, D)
```

```python
import functools

import jax
import jax.numpy as jnp
import numpy as np
from jax import lax
from jax.experimental import pallas as pl
from jax.experimental.pallas import tpu as pltpu

EPS = 1e-6
POOL_WINDOWS = (2, 4, 8, 16)
POOL_HALO = 16
CHUNK = 128
SG_HEADS = 4
SB_HD = 64
LANES = 128

ROW_TILE = 512
ATT_BLOCK = 128
ATT_ROWS = 512
STICK_CUTOFF = 30.0
VMEM_LIMIT = 56 * 1024 * 1024

F32 = jnp.float32
BF16 = jnp.bfloat16


def _rms(x, g):
    ms = jnp.mean(x * x, axis=-1, keepdims=True)
    return x * lax.rsqrt(ms + EPS) * g


def _gelu_tanh(x):
    c = np.float32(np.sqrt(2.0 / np.pi))
    return 0.5 * x * (1.0 + jnp.tanh(c * (x + 0.044715 * (x * x * x))))


def _inproj_kernel(x_ref, g1_ref, win_ref, wkt_ref, poolw_ref, pscale_ref, sgn_ref, sgw_ref, sgb_ref,
                   yab_ref, q_ref, kt_ref, v_ref, lvl_ref, *, tiles_per_seq):
    T = x_ref.shape[0]
    pw = pscale_ref.shape[1]
    sw = sgn_ref.shape[1]
    i = pl.program_id(0)
    tile_in_seq = i % tiles_per_seq

    h = _rms(x_ref[...], g1_ref[...]).astype(BF16)
    a = jnp.dot(h, win_ref[:, 0:pw], preferred_element_type=F32)
    bpre = jnp.dot(h, win_ref[:, pw:pw + 2 * sw], preferred_element_type=F32)
    c2 = pw + 2 * sw
    sbw = q_ref.shape[1]
    q_ref[...] = (jnp.dot(h, win_ref[:, c2:c2 + sbw], preferred_element_type=F32)
                  * float(SB_HD ** -0.5)).astype(BF16)
    v_ref[...] = jnp.dot(h, win_ref[:, c2 + sbw:c2 + 2 * sbw], preferred_element_type=F32).astype(BF16)
    kt = lax.dot_general(wkt_ref[...], h, (((1,), (1,)), ((), ())),
                         preferred_element_type=F32).astype(BF16)
    for hp in range(kt_ref.shape[0]):
        for c in range(kt_ref.shape[1]):
            kt_ref[hp, c] = kt[hp * LANES:(hp + 1) * LANES, c * ATT_BLOCK:(c + 1) * ATT_BLOCK]

    H = POOL_HALO
    lo, n = H, T + H

    @pl.when(i == 0)
    def _():
        lvl_ref[:, 0:H, :] = jnp.zeros((lvl_ref.shape[0], H, pw), F32)

    @pl.when(tile_in_seq == 0)
    def _():
        lvl_ref[0, H:2 * H, :] = jnp.zeros((H, pw), F32)

    @pl.when(tile_in_seq != 0)
    def _():
        lvl_ref[0, H:2 * H, :] = lvl_ref[0, T + H:T + 2 * H, :]

    lvl_ref[0, 2 * H:2 * H + T, :] = a
    for s, shift in enumerate((1, 2, 4, 8)):
        lvl_ref[s + 1, lo:lo + n, :] = (lvl_ref[s, lo:lo + n, :]
                                        + lvl_ref[s, lo - shift:lo - shift + n, :])
    lane = lax.broadcasted_iota(jnp.int32, (1, pw), 1)
    grp = lane // (pw // len(POOL_WINDOWS))
    win = jnp.where(grp == 0, 2, jnp.where(grp == 1, 4, jnp.where(grp == 2, 8, 16)))
    psum = jnp.where(grp == 0, lvl_ref[1, 2 * H:2 * H + T, :],
                     jnp.where(grp == 1, lvl_ref[2, 2 * H:2 * H + T, :],
                               jnp.where(grp == 2, lvl_ref[3, 2 * H:2 * H + T, :],
                                         lvl_ref[4, 2 * H:2 * H + T, :])))
    pos = tile_in_seq * T + lax.broadcasted_iota(jnp.int32, (T, 1), 0)
    cnt = jnp.minimum(pos + 1, win).astype(F32)
    d = psum / cnt - a
    ya = jnp.dot(d.astype(BF16), poolw_ref[...], preferred_element_type=F32) * pscale_ref[...]
    yab_ref[:, 0:pw] = ya.astype(BF16)

    bz = _gelu_tanh(bpre)
    u = bz[:, 0:sw]
    vn = _rms(bz[:, sw:2 * sw], sgn_ref[...]).astype(BF16)
    r_i = lax.broadcasted_iota(jnp.int32, (CHUNK, CHUNK), 0)
    c_i = lax.broadcasted_iota(jnp.int32, (CHUNK, CHUNK), 1)
    tril = c_i <= r_i
    wm = [jnp.where(tril, sgw_ref[hh], 0.0).astype(BF16) for hh in range(SG_HEADS)]
    head = lax.broadcasted_iota(jnp.int32, (1, sw), 1) // (sw // SG_HEADS)
    for c in range(T // CHUNK):
        vc = vn[c * CHUNK:(c + 1) * CHUNK, :]
        sv = sgb_ref[...]
        for hh in range(SG_HEADS):
            r = jnp.dot(wm[hh], vc, preferred_element_type=F32)
            sv = sv + jnp.where(head == hh, r, 0.0)
        yab_ref[c * CHUNK:(c + 1) * CHUNK, pw:pw + sw] = (u[c * CHUNK:(c + 1) * CHUNK, :] * sv).astype(BF16)


def _inproj(x2, g1, win, wkt, poolw, pscale, sgn, sgw, sgb, *, batch, seq):
    N, D = x2.shape
    T = ROW_TILE
    pw, sw = pscale.shape[1], sgn.shape[1]
    sbw = wkt.shape[0]
    tps = seq // T
    const = lambda i: (0, 0)
    row = lambda i: (i, 0)
    return pl.pallas_call(
        functools.partial(_inproj_kernel, tiles_per_seq=tps),
        grid=(N // T,),
        in_specs=[
            pl.BlockSpec((T, D), row),
            pl.BlockSpec((1, D), const),
            pl.BlockSpec(win.shape, const),
            pl.BlockSpec(wkt.shape, const),
            pl.BlockSpec(poolw.shape, const),
            pl.BlockSpec((1, pw), const),
            pl.BlockSpec((1, sw), const),
            pl.BlockSpec(sgw.shape, lambda i: (0, 0, 0)),
            pl.BlockSpec(sgb.shape, const),
        ],
        out_specs=[
            pl.BlockSpec((T, pw + sw), row),
            pl.BlockSpec((T, sbw), row),
            pl.BlockSpec((None, sbw // LANES, T // ATT_BLOCK, LANES, ATT_BLOCK),
                         lambda i: (i // tps, 0, i % tps, 0, 0)),
            pl.BlockSpec((T, sbw), row),
        ],
        out_shape=[
            jax.ShapeDtypeStruct((N, pw + sw), BF16),
            jax.ShapeDtypeStruct((N, sbw), BF16),
            jax.ShapeDtypeStruct((batch, sbw // LANES, seq // ATT_BLOCK, LANES, ATT_BLOCK), BF16),
            jax.ShapeDtypeStruct((N, sbw), BF16),
        ],
        scratch_shapes=[pltpu.VMEM((len(POOL_WINDOWS) + 1, T + 2 * POOL_HALO, pw), F32)],
        compiler_params=pltpu.CompilerParams(
            dimension_semantics=("arbitrary",), vmem_limit_bytes=VMEM_LIMIT),
        name="inproj_mixers",
    )(x2, g1, win, wkt, poolw, pscale, sgn, sgw, sgb)


def _attn_kernel(q_ref, kt_ref, v_ref, tri_ref, o_ref, acc_ref, carry_ref):
    BQ = BK = ATT_BLOCK
    NS = q_ref.shape[0] // BQ
    i = pl.program_id(2)
    head0_lane = lax.broadcasted_iota(jnp.int32, (1, LANES), 1) < SB_HD
    head0_dim = lax.broadcasted_iota(jnp.int32, (LANES, 1), 0) < SB_HD
    row = lax.broadcasted_iota(jnp.int32, (BQ, 2 * BK), 0)
    col = lax.broadcasted_iota(jnp.int32, (BQ, 2 * BK), 1) & (BK - 1)
    causal = col < row
    tri = tri_ref[...]

    def tile(qr, kb, masked):
        ktile = kt_ref[kb]
        zero = jnp.zeros_like(ktile)
        krhs = jnp.concatenate([jnp.where(head0_dim, ktile, zero),
                                jnp.where(head0_dim, zero, ktile)], axis=1)
        vtile = v_ref[pl.ds(pl.multiple_of(kb * BK, BK), BK), :]
        vrhs = jnp.concatenate([jnp.where(head0_lane, vtile, zero),
                                jnp.where(head0_lane, zero, vtile)], axis=0)
        z = jnp.dot(qr, krhs, preferred_element_type=F32)
        sp = jnp.log(1.0 + jnp.exp(-jnp.abs(z)))
        log_beta = jnp.minimum(z, 0.0) - sp
        log_1m = log_beta - z
        if masked:
            log_1m = jnp.where(causal, log_1m, 0.0)
        hi = log_1m.astype(BF16)
        lo = (log_1m - hi.astype(F32)).astype(BF16)
        after = jnp.dot(jnp.concatenate([hi, lo], axis=1), tri, preferred_element_type=F32)
        a = jnp.exp(log_beta + after)
        if masked:
            a = jnp.where(causal, a, 0.0)
        pv = jnp.dot(a.astype(BF16), vrhs, preferred_element_type=F32)
        rs = jnp.where(head0_lane,
                       jnp.sum(log_1m[:, :BK], axis=-1, keepdims=True),
                       jnp.sum(log_1m[:, BK:], axis=-1, keepdims=True))
        return pv, rs

    carry_max = None
    for r in range(NS):
        kb = i * NS + r
        qr = q_ref[r * BQ:(r + 1) * BQ, :]
        pv_d, rs_d = tile(qr, kb, True)
        pv_p, rs_p = tile(qr, jnp.maximum(kb - 1, 0), False)
        has_prev = kb > 0
        acc_ref[r] = pv_d + jnp.where(has_prev, jnp.exp(rs_d), 0.0) * pv_p
        carry_ref[r] = rs_d + rs_p
        carry_max = rs_d + rs_p if carry_max is None else jnp.maximum(carry_max, rs_d + rs_p)

    def sub_block(r, _):
        qr = q_ref[pl.ds(pl.multiple_of(r * BQ, BQ), BQ), :]

        def cond(st):
            j, cmax = st
            return jnp.logical_and(j >= 0, cmax > -STICK_CUTOFF)

        def body(st):
            j, _ = st
            pv, rs = tile(qr, j, False)
            carry = carry_ref[r]
            acc_ref[r] += jnp.exp(carry) * pv
            carry_ref[r] = carry + rs
            return j - 1, jnp.max(carry + rs)

        lax.while_loop(cond, body, (i * NS + r - 2, jnp.max(carry_ref[r])))
        return 0

    @pl.when(jnp.max(carry_max) > -STICK_CUTOFF)
    def _():
        lax.fori_loop(0, NS, sub_block, 0)

    for r in range(NS):
        o_ref[r * BQ:(r + 1) * BQ, :] = acc_ref[r].astype(o_ref.dtype)


def _attention(q, kt, v, tri, *, batch, seq):
    N, W = q.shape
    R = ATT_ROWS
    q3, v3 = q.reshape(batch, seq, W), v.reshape(batch, seq, W)
    out = pl.pallas_call(
        _attn_kernel,
        grid=(batch, W // LANES, seq // R),
        in_specs=[
            pl.BlockSpec((None, R, LANES), lambda b, hp, i: (b, i, hp)),
            pl.BlockSpec((None, None, seq // ATT_BLOCK, LANES, ATT_BLOCK), lambda b, hp, i: (b, hp, 0, 0, 0)),
            pl.BlockSpec((None, seq, LANES), lambda b, hp, i: (b, 0, hp)),
            pl.BlockSpec(tri.shape, lambda b, hp, i: (0, 0)),
        ],
        out_specs=pl.BlockSpec((None, R, LANES), lambda b, hp, i: (b, i, hp)),
        out_shape=jax.ShapeDtypeStruct((batch, seq, W), BF16),
        scratch_shapes=[pltpu.VMEM((R // ATT_BLOCK, ATT_BLOCK, LANES), F32),
                        pltpu.VMEM((R // ATT_BLOCK, ATT_BLOCK, LANES), F32)],
        compiler_params=pltpu.CompilerParams(
            dimension_semantics=("arbitrary", "arbitrary", "arbitrary"),
            vmem_limit_bytes=VMEM_LIMIT),
        name="stickbreak_attn",
    )(q3, kt, v3, tri)
    return out.reshape(N, W)


def _mlp_kernel(x_ref, yab_ref, yc_ref, g2_ref, gf_ref, wo_hbm, wup_hbm, wdn_hbm, o_ref,
                wo_ref, wup_ref, wdn_ref, sem, *, final_norm, ff_chunk):
    @pl.when(pl.program_id(0) == 0)
    def _():
        copies = [pltpu.make_async_copy(src, dst, sem.at[n])
                  for n, (src, dst) in enumerate(((wo_hbm, wo_ref), (wup_hbm, wup_ref), (wdn_hbm, wdn_ref)))]
        for c in copies:
            c.start()
        for c in copies:
            c.wait()

    nab = yab_ref.shape[1]
    x1 = (x_ref[...]
          + jnp.dot(yab_ref[...], wo_ref[0:nab, :], preferred_element_type=F32)
          + jnp.dot(yc_ref[...], wo_ref[nab:, :], preferred_element_type=F32))
    h = _rms(x1, g2_ref[...]).astype(BF16)
    acc = None
    for c in range(wup_ref.shape[1] // ff_chunk):
        up = jnp.dot(h, wup_ref[:, c * ff_chunk:(c + 1) * ff_chunk], preferred_element_type=F32)
        act = jnp.square(jnp.maximum(up, 0.0)).astype(BF16)
        dn = jnp.dot(act, wdn_ref[c * ff_chunk:(c + 1) * ff_chunk, :], preferred_element_type=F32)
        acc = dn if acc is None else acc + dn
    acc = acc + x1
    if final_norm:
        acc = _rms(acc, gf_ref[...])
    o_ref[...] = acc


def _mlp(x2, yab, yc, wo, g2, wup, wdn, gf, *, final_norm):
    N, D = x2.shape
    T = ROW_TILE
    const = lambda i: (0, 0)
    row = lambda i: (i, 0)
    hbm = pl.BlockSpec(memory_space=pl.ANY)
    return pl.pallas_call(
        functools.partial(_mlp_kernel, final_norm=final_norm, ff_chunk=1024),
        grid=(N // T,),
        in_specs=[
            pl.BlockSpec((T, D), row),
            pl.BlockSpec((T, yab.shape[1]), row),
            pl.BlockSpec((T, yc.shape[1]), row),
            pl.BlockSpec((1, D), const),
            pl.BlockSpec((1, D), const),
            hbm, hbm, hbm,
        ],
        out_specs=pl.BlockSpec((T, D), row),
        out_shape=jax.ShapeDtypeStruct((N, D), F32),
        scratch_shapes=[pltpu.VMEM(wo.shape, BF16), pltpu.VMEM(wup.shape, BF16),
                        pltpu.VMEM(wdn.shape, BF16), pltpu.SemaphoreType.DMA((3,))],
        compiler_params=pltpu.CompilerParams(
            dimension_semantics=("arbitrary",), vmem_limit_bytes=VMEM_LIMIT),
        name="outproj_mlp",
    )(x2, yab, yc, g2, gf, wo, wup, wdn)


def kernel(x, norm1, w_in, pool_w, pool_scale, sg_norm, sg_w, sg_b, w_out, norm2, w_up, w_down, final_norm):
    B, S, D = x.shape
    depth = norm1.shape[0]
    n_grp, gw = pool_w.shape[1], pool_w.shape[2]
    pw, sw = pool_scale.shape[1], sg_norm.shape[1]
    sbw = (w_in.shape[2] - pw - 2 * sw) // 3
    assert S % ROW_TILE == 0 and S % ATT_ROWS == 0 and ROW_TILE % CHUNK == 0 and ROW_TILE % ATT_BLOCK == 0
    assert sg_w.shape[1] == SG_HEADS and sg_w.shape[2] == CHUNK and n_grp == len(POOL_WINDOWS)
    assert sbw % LANES == 0 and ATT_BLOCK & (ATT_BLOCK - 1) == 0

    t1 = np.arange(ATT_BLOCK)[:, None] > np.arange(ATT_BLOCK)[None, :]
    t2 = np.kron(np.eye(LANES // SB_HD), t1)
    tri = jnp.asarray(np.concatenate([t2, t2], axis=0), BF16)

    x2 = x.reshape(B * S, D)
    ck = pw + 2 * sw + sbw
    for l in range(depth):
        win = jnp.concatenate([w_in[l][:, :ck], w_in[l][:, ck + sbw:]], axis=1).astype(BF16)
        wkt = w_in[l][:, ck:ck + sbw].T.astype(BF16)
        poolw = jax.scipy.linalg.block_diag(*[pool_w[l, g] for g in range(n_grp)]).astype(BF16)
        sgb = jnp.repeat(sg_b[l].T, sw // SG_HEADS, axis=1)
        yab, q, kt, v = _inproj(x2, norm1[l][None], win, wkt, poolw, pool_scale[l][None],
                                sg_norm[l][None], sg_w[l], sgb, batch=B, seq=S)
        yc = _attention(q, kt, v, tri, batch=B, seq=S)
        x2 = _mlp(x2, yab, yc, w_out[l].astype(BF16), norm2[l][None],
                  w_up[l].astype(BF16), w_down[l].astype(BF16), final_norm[None],
                  final_norm=(l == depth - 1))
    return x2.reshape(B, S, D)
```

```python
import functools

import jax
import jax.numpy as jnp
import numpy as np
from jax import lax
from jax.experimental import pallas as pl
from jax.experimental.pallas import tpu as pltpu

EPS = 1e-6
POOL_WINDOWS = (2, 4, 8, 16)
POOL_HALO = 16
CHUNK = 128
SG_HEADS = 4
SB_HD = 64
LANES = 128

ROW_TILE = 512
ATT_BLOCK = 128
ATT_ROWS = 1024
STICK_CUTOFF = 30.0
MASK_LOGIT = -1e30
NEG_LOG2E = -1.4426950408889634
VMEM_LIMIT = 56 * 1024 * 1024

F32 = jnp.float32
BF16 = jnp.bfloat16


def _rms(x, g):
    ms = jnp.mean(x * x, axis=-1, keepdims=True)
    return x * lax.rsqrt(ms + EPS) * g


def _gelu_tanh(x):
    c = np.float32(np.sqrt(2.0 / np.pi))
    return 0.5 * x * (1.0 + jnp.tanh(c * (x + 0.044715 * (x * x * x))))


def _inproj_kernel(x_ref, g1_ref, win_ref, wkt_ref, poolw_ref, pscale_ref, sgn_ref, sgw_ref, sgb_ref,
                   yab_ref, q_ref, kt_ref, v_ref, lvl_ref, *, tiles_per_seq):
    T = x_ref.shape[0]
    pw = pscale_ref.shape[1]
    sw = sgn_ref.shape[1]
    i = pl.program_id(0)
    tile_in_seq = i % tiles_per_seq

    h = _rms(x_ref[...], g1_ref[...]).astype(BF16)
    a = jnp.dot(h, win_ref[:, 0:pw], preferred_element_type=F32)
    bpre = jnp.dot(h, win_ref[:, pw:pw + 2 * sw], preferred_element_type=F32)
    c2 = pw + 2 * sw
    sbw = q_ref.shape[1]
    q_ref[...] = (jnp.dot(h, win_ref[:, c2:c2 + sbw], preferred_element_type=F32)
                  * float(SB_HD ** -0.5)).astype(BF16)
    v_ref[...] = jnp.dot(h, win_ref[:, c2 + sbw:c2 + 2 * sbw], preferred_element_type=F32).astype(BF16)
    kt = lax.dot_general(wkt_ref[...], h, (((1,), (1,)), ((), ())),
                         preferred_element_type=F32).astype(BF16)
    for hp in range(kt_ref.shape[0]):
        for c in range(kt_ref.shape[1]):
            kt_ref[hp, c] = kt[hp * LANES:(hp + 1) * LANES, c * ATT_BLOCK:(c + 1) * ATT_BLOCK]

    H = POOL_HALO
    lo, n = H, T + H

    @pl.when(i == 0)
    def _():
        lvl_ref[:, 0:H, :] = jnp.zeros((lvl_ref.shape[0], H, pw), F32)

    @pl.when(tile_in_seq == 0)
    def _():
        lvl_ref[0, H:2 * H, :] = jnp.zeros((H, pw), F32)

    @pl.when(tile_in_seq != 0)
    def _():
        lvl_ref[0, H:2 * H, :] = lvl_ref[0, T + H:T + 2 * H, :]

    lvl_ref[0, 2 * H:2 * H + T, :] = a
    for s, shift in enumerate((1, 2, 4, 8)):
        lvl_ref[s + 1, lo:lo + n, :] = (lvl_ref[s, lo:lo + n, :]
                                        + lvl_ref[s, lo - shift:lo - shift + n, :])
    lane = lax.broadcasted_iota(jnp.int32, (1, pw), 1)
    grp = lane // (pw // len(POOL_WINDOWS))
    win = jnp.where(grp == 0, 2, jnp.where(grp == 1, 4, jnp.where(grp == 2, 8, 16)))
    psum = jnp.where(grp == 0, lvl_ref[1, 2 * H:2 * H + T, :],
                     jnp.where(grp == 1, lvl_ref[2, 2 * H:2 * H + T, :],
                               jnp.where(grp == 2, lvl_ref[3, 2 * H:2 * H + T, :],
                                         lvl_ref[4, 2 * H:2 * H + T, :])))
    pos = tile_in_seq * T + lax.broadcasted_iota(jnp.int32, (T, 1), 0)
    cnt = jnp.minimum(pos + 1, win).astype(F32)
    d = psum / cnt - a
    ya = jnp.dot(d.astype(BF16), poolw_ref[...], preferred_element_type=F32) * pscale_ref[...]
    yab_ref[:, 0:pw] = ya.astype(BF16)

    bz = _gelu_tanh(bpre)
    u = bz[:, 0:sw]
    vn = _rms(bz[:, sw:2 * sw], sgn_ref[...]).astype(BF16)
    r_i = lax.broadcasted_iota(jnp.int32, (CHUNK, CHUNK), 0)
    c_i = lax.broadcasted_iota(jnp.int32, (CHUNK, CHUNK), 1)
    tril = c_i <= r_i
    wm = [jnp.where(tril, sgw_ref[hh], 0.0).astype(BF16) for hh in range(SG_HEADS)]
    head = lax.broadcasted_iota(jnp.int32, (1, sw), 1) // (sw // SG_HEADS)
    for c in range(T // CHUNK):
        vc = vn[c * CHUNK:(c + 1) * CHUNK, :]
        sv = sgb_ref[...]
        for hh in range(SG_HEADS):
            r = jnp.dot(wm[hh], vc, preferred_element_type=F32)
            sv = sv + jnp.where(head == hh, r, 0.0)
        yab_ref[c * CHUNK:(c + 1) * CHUNK, pw:pw + sw] = (u[c * CHUNK:(c + 1) * CHUNK, :] * sv).astype(BF16)


def _inproj(x2, g1, win, wkt, poolw, pscale, sgn, sgw, sgb, *, batch, seq):
    N, D = x2.shape
    T = ROW_TILE
    pw, sw = pscale.shape[1], sgn.shape[1]
    sbw = wkt.shape[0]
    tps = seq // T
    const = lambda i: (0, 0)
    row = lambda i: (i, 0)
    return pl.pallas_call(
        functools.partial(_inproj_kernel, tiles_per_seq=tps),
        grid=(N // T,),
        in_specs=[
            pl.BlockSpec((T, D), row),
            pl.BlockSpec((1, D), const),
            pl.BlockSpec(win.shape, const),
            pl.BlockSpec(wkt.shape, const),
            pl.BlockSpec(poolw.shape, const),
            pl.BlockSpec((1, pw), const),
            pl.BlockSpec((1, sw), const),
            pl.BlockSpec(sgw.shape, lambda i: (0, 0, 0)),
            pl.BlockSpec(sgb.shape, const),
        ],
        out_specs=[
            pl.BlockSpec((T, pw + sw), row),
            pl.BlockSpec((T, sbw), row),
            pl.BlockSpec((None, sbw // LANES, T // ATT_BLOCK, LANES, ATT_BLOCK),
                         lambda i: (i // tps, 0, i % tps, 0, 0)),
            pl.BlockSpec((T, sbw), row),
        ],
        out_shape=[
            jax.ShapeDtypeStruct((N, pw + sw), BF16),
            jax.ShapeDtypeStruct((N, sbw), BF16),
            jax.ShapeDtypeStruct((batch, sbw // LANES, seq // ATT_BLOCK, LANES, ATT_BLOCK), BF16),
            jax.ShapeDtypeStruct((N, sbw), BF16),
        ],
        scratch_shapes=[pltpu.VMEM((len(POOL_WINDOWS) + 1, T + 2 * POOL_HALO, pw), F32)],
        compiler_params=pltpu.CompilerParams(
            dimension_semantics=("arbitrary",), vmem_limit_bytes=VMEM_LIMIT),
        name="inproj_mixers",
    )(x2, g1, win, wkt, poolw, pscale, sgn, sgw, sgb)


def _attn_kernel(q_ref, kt_ref, v_ref, tri_ref, o_ref, acc_ref, carry_ref):
    BQ = BK = ATT_BLOCK
    NS = q_ref.shape[0] // BQ
    i = pl.program_id(2)
    head0_lane = lax.broadcasted_iota(jnp.int32, (1, LANES), 1) < SB_HD
    head0_dim = lax.broadcasted_iota(jnp.int32, (LANES, 1), 0) < SB_HD
    row = lax.broadcasted_iota(jnp.int32, (BQ, 2 * BK), 0)
    col = lax.broadcasted_iota(jnp.int32, (BQ, 2 * BK), 1) & (BK - 1)
    diag_bias = jnp.where(col < row, 0.0, MASK_LOGIT)
    tri = tri_ref[...]

    def key_block(kb):
        ktile = kt_ref[kb]
        vtile = v_ref[pl.ds(pl.multiple_of(kb * BK, BK), BK), :]
        zero = jnp.zeros_like(ktile)
        krhs = jnp.concatenate([jnp.where(head0_dim, ktile, zero),
                                jnp.where(head0_dim, zero, ktile)], axis=1)
        vrhs = jnp.concatenate([jnp.where(head0_lane, vtile, zero),
                                jnp.where(head0_lane, zero, vtile)], axis=0)
        return krhs, vrhs

    def tile(qr, krhs, vrhs, diagonal):
        z = jnp.dot(qr, krhs, preferred_element_type=F32)
        if diagonal:
            z = z + diag_bias
        sp = jnp.log(1.0 + jnp.exp2(jnp.abs(z) * NEG_LOG2E))
        log_beta = jnp.minimum(z, 0.0) - sp
        log_1m = log_beta - z
        hi = log_1m.astype(BF16)
        lo = (log_1m - hi.astype(F32)).astype(BF16)
        after = jnp.dot(jnp.concatenate([hi, lo], axis=1), tri, preferred_element_type=F32)
        a = jnp.exp(log_beta + after)
        pv = jnp.dot(a.astype(BF16), vrhs, preferred_element_type=F32)
        rs = jnp.where(head0_lane,
                       jnp.sum(log_1m[:, :BK], axis=-1, keepdims=True),
                       jnp.sum(log_1m[:, BK:], axis=-1, keepdims=True))
        return pv, rs

    kb0 = i * NS
    blocks = [key_block(jnp.maximum(kb0 - 1, 0))] + [key_block(kb0 + r) for r in range(NS)]
    carry_max = None
    for r in range(NS):
        qr = q_ref[r * BQ:(r + 1) * BQ, :]
        pv_d, rs_d = tile(qr, *blocks[r + 1], True)
        pv_p, rs_p = tile(qr, *blocks[r], False)
        has_prev = kb0 + r > 0
        acc_ref[r] = pv_d + jnp.where(has_prev, jnp.exp(rs_d), 0.0) * pv_p
        carry_ref[r] = rs_d + rs_p
        carry_max = rs_d + rs_p if carry_max is None else jnp.maximum(carry_max, rs_d + rs_p)

    def sub_block(r, _):
        qr = q_ref[pl.ds(pl.multiple_of(r * BQ, BQ), BQ), :]

        def cond(st):
            j, cmax = st
            return jnp.logical_and(j >= 0, cmax > -STICK_CUTOFF)

        def body(st):
            j, _ = st
            pv, rs = tile(qr, *key_block(j), False)
            carry = carry_ref[r]
            acc_ref[r] += jnp.exp(carry) * pv
            carry_ref[r] = carry + rs
            return j - 1, jnp.max(carry + rs)

        lax.while_loop(cond, body, (i * NS + r - 2, jnp.max(carry_ref[r])))
        return 0

    @pl.when(jnp.max(carry_max) > -STICK_CUTOFF)
    def _():
        lax.fori_loop(0, NS, sub_block, 0)

    for r in range(NS):
        o_ref[r * BQ:(r + 1) * BQ, :] = acc_ref[r].astype(o_ref.dtype)


def _attention(q, kt, v, tri, *, batch, seq):
    N, W = q.shape
    R = ATT_ROWS
    q3, v3 = q.reshape(batch, seq, W), v.reshape(batch, seq, W)
    out = pl.pallas_call(
        _attn_kernel,
        grid=(batch, W // LANES, seq // R),
        in_specs=[
            pl.BlockSpec((None, R, LANES), lambda b, hp, i: (b, i, hp)),
            pl.BlockSpec((None, None, seq // ATT_BLOCK, LANES, ATT_BLOCK), lambda b, hp, i: (b, hp, 0, 0, 0)),
            pl.BlockSpec((None, seq, LANES), lambda b, hp, i: (b, 0, hp)),
            pl.BlockSpec(tri.shape, lambda b, hp, i: (0, 0)),
        ],
        out_specs=pl.BlockSpec((None, R, LANES), lambda b, hp, i: (b, i, hp)),
        out_shape=jax.ShapeDtypeStruct((batch, seq, W), BF16),
        scratch_shapes=[pltpu.VMEM((R // ATT_BLOCK, ATT_BLOCK, LANES), F32),
                        pltpu.VMEM((R // ATT_BLOCK, ATT_BLOCK, LANES), F32)],
        compiler_params=pltpu.CompilerParams(
            dimension_semantics=("arbitrary", "arbitrary", "arbitrary"),
            vmem_limit_bytes=VMEM_LIMIT),
        name="stickbreak_attn",
    )(q3, kt, v3, tri)
    return out.reshape(N, W)


def _mlp_kernel(x_ref, yab_ref, yc_ref, g2_ref, gf_ref, wo_hbm, wup_hbm, wdn_hbm, o_ref,
                wo_ref, wup_ref, wdn_ref, sem, *, final_norm, ff_chunk):
    @pl.when(pl.program_id(0) == 0)
    def _():
        copies = [pltpu.make_async_copy(src, dst, sem.at[n])
                  for n, (src, dst) in enumerate(((wo_hbm, wo_ref), (wup_hbm, wup_ref), (wdn_hbm, wdn_ref)))]
        for c in copies:
            c.start()
        for c in copies:
            c.wait()

    nab = yab_ref.shape[1]
    x1 = (x_ref[...]
          + jnp.dot(yab_ref[...], wo_ref[0:nab, :], preferred_element_type=F32)
          + jnp.dot(yc_ref[...], wo_ref[nab:, :], preferred_element_type=F32))
    h = _rms(x1, g2_ref[...]).astype(BF16)
    acc = None
    for c in range(wup_ref.shape[1] // ff_chunk):
        up = jnp.dot(h, wup_ref[:, c * ff_chunk:(c + 1) * ff_chunk], preferred_element_type=F32)
        act = jnp.square(jnp.maximum(up, 0.0)).astype(BF16)
        dn = jnp.dot(act, wdn_ref[c * ff_chunk:(c + 1) * ff_chunk, :], preferred_element_type=F32)
        acc = dn if acc is None else acc + dn
    acc = acc + x1
    if final_norm:
        acc = _rms(acc, gf_ref[...])
    o_ref[...] = acc


def _mlp(x2, yab, yc, wo, g2, wup, wdn, gf, *, final_norm):
    N, D = x2.shape
    T = ROW_TILE
    const = lambda i: (0, 0)
    row = lambda i: (i, 0)
    hbm = pl.BlockSpec(memory_space=pl.ANY)
    return pl.pallas_call(
        functools.partial(_mlp_kernel, final_norm=final_norm, ff_chunk=1024),
        grid=(N // T,),
        in_specs=[
            pl.BlockSpec((T, D), row),
            pl.BlockSpec((T, yab.shape[1]), row),
            pl.BlockSpec((T, yc.shape[1]), row),
            pl.BlockSpec((1, D), const),
            pl.BlockSpec((1, D), const),
            hbm, hbm, hbm,
        ],
        out_specs=pl.BlockSpec((T, D), row),
        out_shape=jax.ShapeDtypeStruct((N, D), F32),
        scratch_shapes=[pltpu.VMEM(wo.shape, BF16), pltpu.VMEM(wup.shape, BF16),
                        pltpu.VMEM(wdn.shape, BF16), pltpu.SemaphoreType.DMA((3,))],
        compiler_params=pltpu.CompilerParams(
            dimension_semantics=("arbitrary",), vmem_limit_bytes=VMEM_LIMIT),
        name="outproj_mlp",
    )(x2, yab, yc, g2, gf, wo, wup, wdn)


def kernel(x, norm1, w_in, pool_w, pool_scale, sg_norm, sg_w, sg_b, w_out, norm2, w_up, w_down, final_norm):
    B, S, D = x.shape
    depth = norm1.shape[0]
    n_grp, gw = pool_w.shape[1], pool_w.shape[2]
    pw, sw = pool_scale.shape[1], sg_norm.shape[1]
    sbw = (w_in.shape[2] - pw - 2 * sw) // 3
    assert S % ROW_TILE == 0 and S % ATT_ROWS == 0 and ROW_TILE % CHUNK == 0 and ROW_TILE % ATT_BLOCK == 0
    assert sg_w.shape[1] == SG_HEADS and sg_w.shape[2] == CHUNK and n_grp == len(POOL_WINDOWS)
    assert sbw % LANES == 0 and ATT_BLOCK & (ATT_BLOCK - 1) == 0

    t1 = np.arange(ATT_BLOCK)[:, None] > np.arange(ATT_BLOCK)[None, :]
    t2 = np.kron(np.eye(LANES // SB_HD), t1)
    tri = jnp.asarray(np.concatenate([t2, t2], axis=0), BF16)

    x2 = x.reshape(B * S, D)
    ck = pw + 2 * sw + sbw
    for l in range(depth):
        win = jnp.concatenate([w_in[l][:, :ck], w_in[l][:, ck + sbw:]], axis=1).astype(BF16)
        wkt = w_in[l][:, ck:ck + sbw].T.astype(BF16)
        poolw = jax.scipy.linalg.block_diag(*[pool_w[l, g] for g in range(n_grp)]).astype(BF16)
        sgb = jnp.repeat(sg_b[l].T, sw // SG_HEADS, axis=1)
        yab, q, kt, v = _inproj(x2, norm1[l][None], win, wkt, poolw, pool_scale[l][None],
                                sg_norm[l][None], sg_w[l], sgb, batch=B, seq=S)
        yc = _attention(q, kt, v, tri, batch=B, seq=S)
        x2 = _mlp(x2, yab, yc, w_out[l].astype(BF16), norm2[l][None],
                  w_up[l].astype(BF16), w_down[l].astype(BF16), final_norm[None],
                  final_norm=(l == depth - 1))
    return x2.reshape(B, S, D)
```

```python
import functools

import jax
import jax.numpy as jnp
import numpy as np
from jax import lax
from jax.experimental import pallas as pl
from jax.experimental.pallas import tpu as pltpu

EPS = 1e-6
POOL_WINDOWS = (2, 4, 8, 16)
POOL_HALO = 16
CHUNK = 128
SG_HEADS = 4
SB_HD = 64
LANES = 128

ROW_TILE = 512
ATT_BLOCK = 128
ATT_SUB = 64
ATT_ROWS = 4096
ATT_SKEW = 3
STICK_CUTOFF = 30.0
MASK_LOGIT = -1e30
NEG_LOG2E = -1.4426950408889634
VMEM_LIMIT = 56 * 1024 * 1024

F32 = jnp.float32
BF16 = jnp.bfloat16


def _rms(x, g):
    ms = jnp.mean(x * x, axis=-1, keepdims=True)
    return x * lax.rsqrt(ms + EPS) * g


def _gelu_tanh(x):
    c = np.float32(np.sqrt(2.0 / np.pi))
    return 0.5 * x * (1.0 + jnp.tanh(c * (x + 0.044715 * (x * x * x))))


def _inproj_kernel(x_ref, g1_ref, win_ref, wkt_ref, poolw_ref, pscale_ref, sgn_ref, sgw_ref, sgb_ref,
                   yab_ref, q_ref, kt_ref, v_ref, lvl_ref, *, tiles_per_seq):
    T = x_ref.shape[0]
    pw = pscale_ref.shape[1]
    sw = sgn_ref.shape[1]
    i = pl.program_id(0)
    tile_in_seq = i % tiles_per_seq

    h = _rms(x_ref[...], g1_ref[...]).astype(BF16)
    a = jnp.dot(h, win_ref[:, 0:pw], preferred_element_type=F32)
    bpre = jnp.dot(h, win_ref[:, pw:pw + 2 * sw], preferred_element_type=F32)
    c2 = pw + 2 * sw
    sbw = q_ref.shape[1]
    q_ref[...] = (jnp.dot(h, win_ref[:, c2:c2 + sbw], preferred_element_type=F32)
                  * float(SB_HD ** -0.5)).astype(BF16)
    v_ref[...] = jnp.dot(h, win_ref[:, c2 + sbw:c2 + 2 * sbw], preferred_element_type=F32).astype(BF16)
    kt = lax.dot_general(wkt_ref[...], h, (((1,), (1,)), ((), ())),
                         preferred_element_type=F32).astype(BF16)
    for hp in range(kt_ref.shape[0]):
        for c in range(kt_ref.shape[1]):
            kt_ref[hp, c] = kt[hp * LANES:(hp + 1) * LANES, c * ATT_BLOCK:(c + 1) * ATT_BLOCK]

    H = POOL_HALO
    lo, n = H, T + H

    @pl.when(i == 0)
    def _():
        lvl_ref[:, 0:H, :] = jnp.zeros((lvl_ref.shape[0], H, pw), F32)

    @pl.when(tile_in_seq == 0)
    def _():
        lvl_ref[0, H:2 * H, :] = jnp.zeros((H, pw), F32)

    @pl.when(tile_in_seq != 0)
    def _():
        lvl_ref[0, H:2 * H, :] = lvl_ref[0, T + H:T + 2 * H, :]

    lvl_ref[0, 2 * H:2 * H + T, :] = a
    for s, shift in enumerate((1, 2, 4, 8)):
        lvl_ref[s + 1, lo:lo + n, :] = (lvl_ref[s, lo:lo + n, :]
                                        + lvl_ref[s, lo - shift:lo - shift + n, :])
    lane = lax.broadcasted_iota(jnp.int32, (1, pw), 1)
    grp = lane // (pw // len(POOL_WINDOWS))
    win = jnp.where(grp == 0, 2, jnp.where(grp == 1, 4, jnp.where(grp == 2, 8, 16)))
    psum = jnp.where(grp == 0, lvl_ref[1, 2 * H:2 * H + T, :],
                     jnp.where(grp == 1, lvl_ref[2, 2 * H:2 * H + T, :],
                               jnp.where(grp == 2, lvl_ref[3, 2 * H:2 * H + T, :],
                                         lvl_ref[4, 2 * H:2 * H + T, :])))
    pos = tile_in_seq * T + lax.broadcasted_iota(jnp.int32, (T, 1), 0)
    cnt = jnp.minimum(pos + 1, win).astype(F32)
    d = psum / cnt - a
    ya = jnp.dot(d.astype(BF16), poolw_ref[...], preferred_element_type=F32) * pscale_ref[...]
    yab_ref[:, 0:pw] = ya.astype(BF16)

    bz = _gelu_tanh(bpre)
    u = bz[:, 0:sw]
    vn = _rms(bz[:, sw:2 * sw], sgn_ref[...]).astype(BF16)
    r_i = lax.broadcasted_iota(jnp.int32, (CHUNK, CHUNK), 0)
    c_i = lax.broadcasted_iota(jnp.int32, (CHUNK, CHUNK), 1)
    tril = c_i <= r_i
    wm = [jnp.where(tril, sgw_ref[hh], 0.0).astype(BF16) for hh in range(SG_HEADS)]
    head = lax.broadcasted_iota(jnp.int32, (1, sw), 1) // (sw // SG_HEADS)
    for c in range(T // CHUNK):
        vc = vn[c * CHUNK:(c + 1) * CHUNK, :]
        sv = sgb_ref[...]
        for hh in range(SG_HEADS):
            r = jnp.dot(wm[hh], vc, preferred_element_type=F32)
            sv = sv + jnp.where(head == hh, r, 0.0)
        yab_ref[c * CHUNK:(c + 1) * CHUNK, pw:pw + sw] = (u[c * CHUNK:(c + 1) * CHUNK, :] * sv).astype(BF16)


def _inproj(x2, g1, win, wkt, poolw, pscale, sgn, sgw, sgb, *, batch, seq):
    N, D = x2.shape
    T = ROW_TILE
    pw, sw = pscale.shape[1], sgn.shape[1]
    sbw = wkt.shape[0]
    tps = seq // T
    const = lambda i: (0, 0)
    row = lambda i: (i, 0)
    return pl.pallas_call(
        functools.partial(_inproj_kernel, tiles_per_seq=tps),
        grid=(N // T,),
        in_specs=[
            pl.BlockSpec((T, D), row),
            pl.BlockSpec((1, D), const),
            pl.BlockSpec(win.shape, const),
            pl.BlockSpec(wkt.shape, const),
            pl.BlockSpec(poolw.shape, const),
            pl.BlockSpec((1, pw), const),
            pl.BlockSpec((1, sw), const),
            pl.BlockSpec(sgw.shape, lambda i: (0, 0, 0)),
            pl.BlockSpec(sgb.shape, const),
        ],
        out_specs=[
            pl.BlockSpec((T, pw + sw), row),
            pl.BlockSpec((T, sbw), row),
            pl.BlockSpec((None, sbw // LANES, T // ATT_BLOCK, LANES, ATT_BLOCK),
                         lambda i: (i // tps, 0, i % tps, 0, 0)),
            pl.BlockSpec((T, sbw), row),
        ],
        out_shape=[
            jax.ShapeDtypeStruct((N, pw + sw), BF16),
            jax.ShapeDtypeStruct((N, sbw), BF16),
            jax.ShapeDtypeStruct((batch, sbw // LANES, seq // ATT_BLOCK, LANES, ATT_BLOCK), BF16),
            jax.ShapeDtypeStruct((N, sbw), BF16),
        ],
        scratch_shapes=[pltpu.VMEM((len(POOL_WINDOWS) + 1, T + 2 * POOL_HALO, pw), F32)],
        compiler_params=pltpu.CompilerParams(
            dimension_semantics=("arbitrary",), vmem_limit_bytes=VMEM_LIMIT),
        name="inproj_mixers",
    )(x2, g1, win, wkt, poolw, pscale, sgn, sgw, sgb)


def _attn_kernel(q_ref, kt_ref, v_ref, tri_ref, o_ref, acc_ref, carry_ref):
    SB = ATT_SUB
    NS = o_ref.shape[0] // SB
    i = pl.program_id(2)
    kb0 = i * NS
    head0_lane = lax.broadcasted_iota(jnp.int32, (1, LANES), 1) < SB_HD
    row = lax.broadcasted_iota(jnp.int32, (SB, LANES), 0)
    col = lax.broadcasted_iota(jnp.int32, (SB, LANES), 1) & (SB - 1)
    bias_diag = jnp.where(col < row, 0.0, MASK_LOGIT)
    tri = tri_ref[...]

    def key_rhs(ktile, rolled, half):
        h0, h1 = (ktile, rolled) if half == 0 else (rolled, ktile)
        zero = jnp.zeros((SB_HD, LANES), ktile.dtype)
        return jnp.concatenate([jnp.where(head0_lane, h0[:SB_HD], zero),
                                jnp.where(head0_lane, zero, h1[SB_HD:])], axis=0)

    def value_rhs(kb):
        vrows = v_ref[pl.ds(pl.multiple_of(kb * SB, SB), SB), :]
        zero = jnp.zeros_like(vrows)
        return jnp.concatenate([jnp.where(head0_lane, vrows, zero),
                                jnp.where(head0_lane, zero, vrows)], axis=0)

    def logits_stage(qrows, krhs, diagonal_rows):
        z = jnp.dot(qrows, krhs, preferred_element_type=F32)
        if diagonal_rows == z.shape[0]:
            z = z + bias_diag
        elif diagonal_rows:
            z = jnp.concatenate([z[:diagonal_rows] + bias_diag, z[diagonal_rows:]], axis=0)
        neg_log_1m = jnp.maximum(z, 0.0) + jnp.log(1.0 + jnp.exp2(jnp.abs(z) * NEG_LOG2E))
        hi = neg_log_1m.astype(BF16)
        lo = (neg_log_1m - hi.astype(F32)).astype(BF16)
        return z, jnp.concatenate([hi, lo], axis=1)

    def weights_stage(z, hilo):
        ext = jnp.dot(hilo, tri, preferred_element_type=F32)
        return jnp.exp(z + ext[:, :LANES]).astype(BF16), ext[:, LANES:]

    def output_stage(a, vrhs):
        return jnp.dot(a, vrhs, preferred_element_type=F32)

    def tile(qrows, krhs, vrhs):
        a, rs = weights_stage(*logits_stage(qrows, krhs, 0))
        return output_stage(a, vrhs), rs

    def q_rows(first, count):
        return q_ref[first * SB:(first + count) * SB, :]

    ktiles = {}
    for t in range(-1, NS // 2):
        kt_tile = kt_ref[jnp.maximum(kb0 // 2 + t, 0)]
        ktiles[t] = (kt_tile, pltpu.roll(kt_tile, SB, 1))

    def first_stage(m):
        krhs = key_rhs(*ktiles[m // 2], m % 2)
        if m == -1:
            return logits_stage(q_rows(0, 1), krhs, 0)
        if m == NS - 1:
            return logits_stage(q_rows(m, 1), krhs, SB)
        return logits_stage(q_rows(m, 2), krhs, SB)

    blocks = list(range(-1, NS))
    stage1, stage2, parts = {}, {}, []
    for s in range(len(blocks) + 2 * ATT_SKEW):
        if s < len(blocks):
            stage1[s] = first_stage(blocks[s])
        if 0 <= s - ATT_SKEW < len(blocks):
            stage2[s - ATT_SKEW] = weights_stage(*stage1.pop(s - ATT_SKEW))
        if 0 <= s - 2 * ATT_SKEW < len(blocks):
            a, rs = stage2.pop(s - 2 * ATT_SKEW)
            vrhs = value_rhs(jnp.maximum(kb0 + blocks[s - 2 * ATT_SKEW], 0))
            parts.append((output_stage(a, vrhs), rs))

    carry_max = None
    for r in range(NS):
        pv_p, rs_p = parts[r] if r == 0 else (parts[r][0][SB:], parts[r][1][SB:])
        pv_d, rs_d = parts[r + 1] if r == NS - 1 else (parts[r + 1][0][:SB], parts[r + 1][1][:SB])
        scale = jnp.exp(rs_d)
        if r == 0:
            scale = jnp.where(kb0 > 0, scale, 0.0)
        acc_ref[r] = pv_d + scale * pv_p
        carry_ref[r] = rs_d + rs_p
        carry_max = rs_d + rs_p if carry_max is None else jnp.maximum(carry_max, rs_d + rs_p)

    def sub_block(r, _):
        qr = q_ref[pl.ds(pl.multiple_of(r * SB, SB), SB), :]

        def cond(st):
            j, cmax = st
            return jnp.logical_and(j >= 0, cmax > -STICK_CUTOFF)

        def body(st):
            j, _ = st
            kt_tile = kt_ref[lax.shift_right_logical(j, 1)]
            rolled = pltpu.roll(kt_tile, SB, 1)
            krhs = jnp.where((j & 1) == 0, key_rhs(kt_tile, rolled, 0), key_rhs(kt_tile, rolled, 1))
            pv, rs = tile(qr, krhs, value_rhs(j))
            carry = carry_ref[r]
            acc_ref[r] += jnp.exp(carry) * pv
            carry_ref[r] = carry + rs
            return j - 1, jnp.max(carry + rs)

        lax.while_loop(cond, body, (kb0 + r - 2, jnp.max(carry_ref[r])))
        return 0

    @pl.when(jnp.max(carry_max) > -STICK_CUTOFF)
    def _():
        lax.fori_loop(0, NS, sub_block, 0)

    for r in range(NS):
        o_ref[r * SB:(r + 1) * SB, :] = acc_ref[r].astype(o_ref.dtype)


def _stick_matrix():
    key = np.arange(LANES) % ATT_SUB
    head = np.arange(LANES) // ATT_SUB
    same = head[:, None] == head[None, :]
    suffix = np.logical_and(same, key[:, None] >= key[None, :])
    half = -np.concatenate([suffix, same], axis=1).astype(np.float32)
    return jnp.asarray(np.concatenate([half, half], axis=0), BF16)


def _attention(q, kt, v, *, batch, seq):
    N, W = q.shape
    R = ATT_ROWS
    q3, v3 = q.reshape(batch, seq, W), v.reshape(batch, seq, W)
    tri = _stick_matrix()
    resident = lambda b, hp, i: (b, 0, hp)
    out = pl.pallas_call(
        _attn_kernel,
        grid=(batch, W // LANES, seq // R),
        in_specs=[
            pl.BlockSpec((None, R, LANES), lambda b, hp, i: (b, i, hp)),
            pl.BlockSpec((None, None, seq // ATT_BLOCK, LANES, ATT_BLOCK), lambda b, hp, i: (b, hp, 0, 0, 0)),
            pl.BlockSpec((None, seq, LANES), resident),
            pl.BlockSpec(tri.shape, lambda b, hp, i: (0, 0)),
        ],
        out_specs=pl.BlockSpec((None, R, LANES), lambda b, hp, i: (b, i, hp)),
        out_shape=jax.ShapeDtypeStruct((batch, seq, W), BF16),
        scratch_shapes=[pltpu.VMEM((R // ATT_SUB, ATT_SUB, LANES), F32),
                        pltpu.VMEM((R // ATT_SUB, ATT_SUB, LANES), F32)],
        compiler_params=pltpu.CompilerParams(
            dimension_semantics=("arbitrary", "arbitrary", "arbitrary"),
            vmem_limit_bytes=VMEM_LIMIT),
        name="stickbreak_attn",
    )(q3, kt, v3, tri)
    return out.reshape(N, W)


def _mlp_kernel(x_ref, yab_ref, yc_ref, g2_ref, gf_ref, wo_hbm, wup_hbm, wdn_hbm, o_ref,
                wo_ref, wup_ref, wdn_ref, sem, *, final_norm, ff_chunk):
    @pl.when(pl.program_id(0) == 0)
    def _():
        copies = [pltpu.make_async_copy(src, dst, sem.at[n])
                  for n, (src, dst) in enumerate(((wo_hbm, wo_ref), (wup_hbm, wup_ref), (wdn_hbm, wdn_ref)))]
        for c in copies:
            c.start()
        for c in copies:
            c.wait()

    nab = yab_ref.shape[1]
    x1 = (x_ref[...]
          + jnp.dot(yab_ref[...], wo_ref[0:nab, :], preferred_element_type=F32)
          + jnp.dot(yc_ref[...], wo_ref[nab:, :], preferred_element_type=F32))
    h = _rms(x1, g2_ref[...]).astype(BF16)
    acc = None
    for c in range(wup_ref.shape[1] // ff_chunk):
        up = jnp.dot(h, wup_ref[:, c * ff_chunk:(c + 1) * ff_chunk], preferred_element_type=F32)
        act = jnp.square(jnp.maximum(up, 0.0)).astype(BF16)
        dn = jnp.dot(act, wdn_ref[c * ff_chunk:(c + 1) * ff_chunk, :], preferred_element_type=F32)
        acc = dn if acc is None else acc + dn
    acc = acc + x1
    if final_norm:
        acc = _rms(acc, gf_ref[...])
    o_ref[...] = acc


def _mlp(x2, yab, yc, wo, g2, wup, wdn, gf, *, final_norm):
    N, D = x2.shape
    T = ROW_TILE
    const = lambda i: (0, 0)
    row = lambda i: (i, 0)
    hbm = pl.BlockSpec(memory_space=pl.ANY)
    return pl.pallas_call(
        functools.partial(_mlp_kernel, final_norm=final_norm, ff_chunk=1024),
        grid=(N // T,),
        in_specs=[
            pl.BlockSpec((T, D), row),
            pl.BlockSpec((T, yab.shape[1]), row),
            pl.BlockSpec((T, yc.shape[1]), row),
            pl.BlockSpec((1, D), const),
            pl.BlockSpec((1, D), const),
            hbm, hbm, hbm,
        ],
        out_specs=pl.BlockSpec((T, D), row),
        out_shape=jax.ShapeDtypeStruct((N, D), F32),
        scratch_shapes=[pltpu.VMEM(wo.shape, BF16), pltpu.VMEM(wup.shape, BF16),
                        pltpu.VMEM(wdn.shape, BF16), pltpu.SemaphoreType.DMA((3,))],
        compiler_params=pltpu.CompilerParams(
            dimension_semantics=("arbitrary",), vmem_limit_bytes=VMEM_LIMIT),
        name="outproj_mlp",
    )(x2, yab, yc, g2, gf, wo, wup, wdn)


def kernel(x, norm1, w_in, pool_w, pool_scale, sg_norm, sg_w, sg_b, w_out, norm2, w_up, w_down, final_norm):
    B, S, D = x.shape
    depth = norm1.shape[0]
    n_grp, gw = pool_w.shape[1], pool_w.shape[2]
    pw, sw = pool_scale.shape[1], sg_norm.shape[1]
    sbw = (w_in.shape[2] - pw - 2 * sw) // 3
    assert S % ROW_TILE == 0 and S % ATT_ROWS == 0 and ROW_TILE % CHUNK == 0 and ROW_TILE % ATT_BLOCK == 0
    assert sg_w.shape[1] == SG_HEADS and sg_w.shape[2] == CHUNK and n_grp == len(POOL_WINDOWS)
    assert sbw % LANES == 0 and ATT_BLOCK == LANES == 2 * ATT_SUB == 2 * SB_HD and (ATT_ROWS // ATT_SUB) % 2 == 0

    x2 = x.reshape(B * S, D)
    ck = pw + 2 * sw + sbw
    for l in range(depth):
        win = jnp.concatenate([w_in[l][:, :ck], w_in[l][:, ck + sbw:]], axis=1).astype(BF16)
        wkt = w_in[l][:, ck:ck + sbw].T.astype(BF16)
        poolw = jax.scipy.linalg.block_diag(*[pool_w[l, g] for g in range(n_grp)]).astype(BF16)
        sgb = jnp.repeat(sg_b[l].T, sw // SG_HEADS, axis=1)
        yab, q, kt, v = _inproj(x2, norm1[l][None], win, wkt, poolw, pool_scale[l][None],
                                sg_norm[l][None], sg_w[l], sgb, batch=B, seq=S)
        yc = _attention(q, kt, v, batch=B, seq=S)
        x2 = _mlp(x2, yab, yc, w_out[l].astype(BF16), norm2[l][None],
                  w_up[l].astype(BF16), w_down[l].astype(BF16), final_norm[None],
                  final_norm=(l == depth - 1))
    return x2.reshape(B, S, D)
```

```python
import functools

import jax
import jax.numpy as jnp
import numpy as np
from jax import lax
from jax.experimental import pallas as pl
from jax.experimental.pallas import tpu as pltpu

EPS = 1e-6
POOL_WINDOWS = (2, 4, 8, 16)
POOL_HALO = 16
CHUNK = 128
SG_HEADS = 4
SB_HD = 64
LANES = 128

ROW_TILE = 512
ATT_BLOCK = 128
ATT_SUB = 64
ATT_ROWS = 4096
ATT_SKEW = 3
STICK_CUTOFF = 30.0
MASK_LOGIT = -1e30
NEG_LOG2E = -1.4426950408889634
VMEM_LIMIT = 56 * 1024 * 1024

F32 = jnp.float32
BF16 = jnp.bfloat16


def _rms(x, g):
    ms = jnp.mean(x * x, axis=-1, keepdims=True)
    return x * lax.rsqrt(ms + EPS) * g


def _gelu_tanh(x):
    c = np.float32(np.sqrt(2.0 / np.pi))
    return 0.5 * x * (1.0 + jnp.tanh(c * (x + 0.044715 * (x * x * x))))


def _inproj_kernel(x_ref, g1_ref, win_ref, wkt_ref, poolw_ref, pscale_ref, sgn_ref, sgw_ref, sgb_ref,
                   yab_ref, q_ref, kt_ref, v_ref, lvl_ref, *, tiles_per_seq):
    T = x_ref.shape[0]
    pw = pscale_ref.shape[1]
    sw = sgn_ref.shape[1]
    i = pl.program_id(0)
    tile_in_seq = i % tiles_per_seq

    h = _rms(x_ref[...], g1_ref[...]).astype(BF16)
    a = jnp.dot(h, win_ref[:, 0:pw], preferred_element_type=F32)
    bpre = jnp.dot(h, win_ref[:, pw:pw + 2 * sw], preferred_element_type=F32)
    c2 = pw + 2 * sw
    sbw = q_ref.shape[1]
    q_ref[...] = (jnp.dot(h, win_ref[:, c2:c2 + sbw], preferred_element_type=F32)
                  * float(SB_HD ** -0.5)).astype(BF16)
    v_ref[...] = jnp.dot(h, win_ref[:, c2 + sbw:c2 + 2 * sbw], preferred_element_type=F32).astype(BF16)
    kt = lax.dot_general(wkt_ref[...], h, (((1,), (1,)), ((), ())),
                         preferred_element_type=F32).astype(BF16)
    for hp in range(kt_ref.shape[0]):
        for c in range(kt_ref.shape[1]):
            kt_ref[hp, c] = kt[hp * LANES:(hp + 1) * LANES, c * ATT_BLOCK:(c + 1) * ATT_BLOCK]

    H = POOL_HALO
    lo, n = H, T + H

    @pl.when(i == 0)
    def _():
        lvl_ref[:, 0:H, :] = jnp.zeros((lvl_ref.shape[0], H, pw), F32)

    @pl.when(tile_in_seq == 0)
    def _():
        lvl_ref[0, H:2 * H, :] = jnp.zeros((H, pw), F32)

    @pl.when(tile_in_seq != 0)
    def _():
        lvl_ref[0, H:2 * H, :] = lvl_ref[0, T + H:T + 2 * H, :]

    lvl_ref[0, 2 * H:2 * H + T, :] = a
    for s, shift in enumerate((1, 2, 4, 8)):
        lvl_ref[s + 1, lo:lo + n, :] = (lvl_ref[s, lo:lo + n, :]
                                        + lvl_ref[s, lo - shift:lo - shift + n, :])
    lane = lax.broadcasted_iota(jnp.int32, (1, pw), 1)
    grp = lane // (pw // len(POOL_WINDOWS))
    win = jnp.where(grp == 0, 2, jnp.where(grp == 1, 4, jnp.where(grp == 2, 8, 16)))
    psum = jnp.where(grp == 0, lvl_ref[1, 2 * H:2 * H + T, :],
                     jnp.where(grp == 1, lvl_ref[2, 2 * H:2 * H + T, :],
                               jnp.where(grp == 2, lvl_ref[3, 2 * H:2 * H + T, :],
                                         lvl_ref[4, 2 * H:2 * H + T, :])))
    pos = tile_in_seq * T + lax.broadcasted_iota(jnp.int32, (T, 1), 0)
    cnt = jnp.minimum(pos + 1, win).astype(F32)
    d = psum / cnt - a
    ya = jnp.dot(d.astype(BF16), poolw_ref[...], preferred_element_type=F32) * pscale_ref[...]
    yab_ref[:, 0:pw] = ya.astype(BF16)

    bz = _gelu_tanh(bpre)
    u = bz[:, 0:sw]
    vn = _rms(bz[:, sw:2 * sw], sgn_ref[...]).astype(BF16)
    r_i = lax.broadcasted_iota(jnp.int32, (CHUNK, CHUNK), 0)
    c_i = lax.broadcasted_iota(jnp.int32, (CHUNK, CHUNK), 1)
    tril = c_i <= r_i
    wm = [jnp.where(tril, sgw_ref[hh], 0.0).astype(BF16) for hh in range(SG_HEADS)]
    head = lax.broadcasted_iota(jnp.int32, (1, sw), 1) // (sw // SG_HEADS)
    for c in range(T // CHUNK):
        vc = vn[c * CHUNK:(c + 1) * CHUNK, :]
        sv = sgb_ref[...]
        for hh in range(SG_HEADS):
            r = jnp.dot(wm[hh], vc, preferred_element_type=F32)
            sv = sv + jnp.where(head == hh, r, 0.0)
        yab_ref[c * CHUNK:(c + 1) * CHUNK, pw:pw + sw] = (u[c * CHUNK:(c + 1) * CHUNK, :] * sv).astype(BF16)


def _inproj(x2, g1, win, wkt, poolw, pscale, sgn, sgw, sgb, *, batch, seq):
    N, D = x2.shape
    T = ROW_TILE
    pw, sw = pscale.shape[1], sgn.shape[1]
    sbw = wkt.shape[0]
    tps = seq // T
    const = lambda i: (0, 0)
    row = lambda i: (i, 0)
    return pl.pallas_call(
        functools.partial(_inproj_kernel, tiles_per_seq=tps),
        grid=(N // T,),
        in_specs=[
            pl.BlockSpec((T, D), row),
            pl.BlockSpec((1, D), const),
            pl.BlockSpec(win.shape, const),
            pl.BlockSpec(wkt.shape, const),
            pl.BlockSpec(poolw.shape, const),
            pl.BlockSpec((1, pw), const),
            pl.BlockSpec((1, sw), const),
            pl.BlockSpec(sgw.shape, lambda i: (0, 0, 0)),
            pl.BlockSpec(sgb.shape, const),
        ],
        out_specs=[
            pl.BlockSpec((T, pw + sw), row),
            pl.BlockSpec((T, sbw), row),
            pl.BlockSpec((None, sbw // LANES, T // ATT_BLOCK, LANES, ATT_BLOCK),
                         lambda i: (i // tps, 0, i % tps, 0, 0)),
            pl.BlockSpec((T, sbw), row),
        ],
        out_shape=[
            jax.ShapeDtypeStruct((N, pw + sw), BF16),
            jax.ShapeDtypeStruct((N, sbw), BF16),
            jax.ShapeDtypeStruct((batch, sbw // LANES, seq // ATT_BLOCK, LANES, ATT_BLOCK), BF16),
            jax.ShapeDtypeStruct((N, sbw), BF16),
        ],
        scratch_shapes=[pltpu.VMEM((len(POOL_WINDOWS) + 1, T + 2 * POOL_HALO, pw), F32)],
        compiler_params=pltpu.CompilerParams(
            dimension_semantics=("arbitrary",), vmem_limit_bytes=VMEM_LIMIT),
        name="inproj_mixers",
    )(x2, g1, win, wkt, poolw, pscale, sgn, sgw, sgb)


def _attn_kernel(q_ref, kt_ref, v_ref, tri_ref, o_ref, acc_ref, carry_ref, cmax_ref):
    SB = ATT_SUB
    NS = o_ref.shape[0] // SB
    i = pl.program_id(2)
    kb0 = i * NS
    head0_lane = lax.broadcasted_iota(jnp.int32, (1, LANES), 1) < SB_HD
    row = lax.broadcasted_iota(jnp.int32, (SB, LANES), 0)
    col = lax.broadcasted_iota(jnp.int32, (SB, LANES), 1) & (SB - 1)
    bias_diag = jnp.where(col < row, 0.0, MASK_LOGIT)
    tri = tri_ref[...]

    def key_rhs(ktile, rolled, half):
        h0, h1 = (ktile, rolled) if half == 0 else (rolled, ktile)
        zero = jnp.zeros((SB_HD, LANES), ktile.dtype)
        return jnp.concatenate([jnp.where(head0_lane, h0[:SB_HD], zero),
                                jnp.where(head0_lane, zero, h1[SB_HD:])], axis=0)

    def value_rhs(kb):
        vrows = v_ref[pl.ds(pl.multiple_of(kb * SB, SB), SB), :]
        zero = jnp.zeros_like(vrows)
        return jnp.concatenate([jnp.where(head0_lane, vrows, zero),
                                jnp.where(head0_lane, zero, vrows)], axis=0)

    def logits_stage(qrows, krhs, diagonal_rows):
        z = jnp.dot(qrows, krhs, preferred_element_type=F32)
        if diagonal_rows == z.shape[0]:
            z = z + bias_diag
        elif diagonal_rows:
            z = jnp.concatenate([z[:diagonal_rows] + bias_diag, z[diagonal_rows:]], axis=0)
        neg_log_1m = jnp.maximum(z, 0.0) + jnp.log(1.0 + jnp.exp2(jnp.abs(z) * NEG_LOG2E))
        hi = neg_log_1m.astype(BF16)
        lo = (neg_log_1m - hi.astype(F32)).astype(BF16)
        return z, jnp.concatenate([hi, lo], axis=1)

    def weights_stage(z, hilo):
        ext = jnp.dot(hilo, tri, preferred_element_type=F32)
        return jnp.exp(z + ext[:, :LANES]).astype(BF16), ext[:, LANES:]

    def output_stage(a, vrhs):
        return jnp.dot(a, vrhs, preferred_element_type=F32)

    def tile(qrows, krhs, vrhs):
        a, rs = weights_stage(*logits_stage(qrows, krhs, 0))
        return output_stage(a, vrhs), rs

    def q_rows(first, count):
        return q_ref[first * SB:(first + count) * SB, :]

    ktiles = {}
    for t in range(-1, NS // 2):
        kt_tile = kt_ref[jnp.maximum(kb0 // 2 + t, 0)]
        ktiles[t] = (kt_tile, pltpu.roll(kt_tile, SB, 1))

    def first_stage(m):
        krhs = key_rhs(*ktiles[m // 2], m % 2)
        if m == -1:
            return logits_stage(q_rows(0, 1), krhs, 0)
        if m == NS - 1:
            return logits_stage(q_rows(m, 1), krhs, SB)
        return logits_stage(q_rows(m, 2), krhs, SB)

    blocks = list(range(-1, NS))
    stage1, stage2, parts = {}, {}, []
    for s in range(len(blocks) + 2 * ATT_SKEW):
        if s < len(blocks):
            stage1[s] = first_stage(blocks[s])
        if 0 <= s - ATT_SKEW < len(blocks):
            stage2[s - ATT_SKEW] = weights_stage(*stage1.pop(s - ATT_SKEW))
        if 0 <= s - 2 * ATT_SKEW < len(blocks):
            a, rs = stage2.pop(s - 2 * ATT_SKEW)
            vrhs = value_rhs(jnp.maximum(kb0 + blocks[s - 2 * ATT_SKEW], 0))
            parts.append((output_stage(a, vrhs), rs))

    carry_max = None
    for r in range(NS):
        pv_p, rs_p = parts[r] if r == 0 else (parts[r][0][SB:], parts[r][1][SB:])
        pv_d, rs_d = parts[r + 1] if r == NS - 1 else (parts[r + 1][0][:SB], parts[r + 1][1][:SB])
        scale = jnp.exp(rs_d)
        if r == 0:
            scale = jnp.where(kb0 > 0, scale, 0.0)
        acc_ref[r] = pv_d + scale * pv_p
        carry_ref[r] = rs_d + rs_p
        carry_max = rs_d + rs_p if carry_max is None else jnp.maximum(carry_max, rs_d + rs_p)

    def sub_block(r, _):
        qr = q_ref[pl.ds(pl.multiple_of(r * SB, SB), SB), :]

        def cond(st):
            j, cmax = st
            return jnp.logical_and(j >= 0, cmax > -STICK_CUTOFF)

        def body(st):
            j, _ = st
            kt_tile = kt_ref[lax.shift_right_logical(j, 1)]
            rolled = pltpu.roll(kt_tile, SB, 1)
            krhs = jnp.where((j & 1) == 0, key_rhs(kt_tile, rolled, 0), key_rhs(kt_tile, rolled, 1))
            pv, rs = tile(qr, krhs, value_rhs(j))
            carry = carry_ref[r]
            acc_ref[r] += jnp.exp(carry) * pv
            carry_ref[r] = carry + rs
            return j - 1, jnp.max(carry + rs)

        @pl.when(cmax_ref[r] > -STICK_CUTOFF)
        def _():
            lax.while_loop(cond, body, (kb0 + r - 2, cmax_ref[r]))

        return 0

    @pl.when(jnp.max(carry_max) > -STICK_CUTOFF)
    def _():
        for r in range(NS):
            cmax_ref[r] = jnp.max(carry_ref[r])
        lax.fori_loop(0, NS, sub_block, 0)

    for r in range(NS):
        o_ref[r * SB:(r + 1) * SB, :] = acc_ref[r].astype(o_ref.dtype)


def _stick_matrix():
    key = np.arange(LANES) % ATT_SUB
    head = np.arange(LANES) // ATT_SUB
    same = head[:, None] == head[None, :]
    suffix = np.logical_and(same, key[:, None] >= key[None, :])
    half = -np.concatenate([suffix, same], axis=1).astype(np.float32)
    return jnp.asarray(np.concatenate([half, half], axis=0), BF16)


def _attention(q, kt, v, *, batch, seq):
    N, W = q.shape
    R = ATT_ROWS
    q3, v3 = q.reshape(batch, seq, W), v.reshape(batch, seq, W)
    tri = _stick_matrix()
    resident = lambda b, hp, i: (b, 0, hp)
    out = pl.pallas_call(
        _attn_kernel,
        grid=(batch, W // LANES, seq // R),
        in_specs=[
            pl.BlockSpec((None, R, LANES), lambda b, hp, i: (b, i, hp)),
            pl.BlockSpec((None, None, seq // ATT_BLOCK, LANES, ATT_BLOCK), lambda b, hp, i: (b, hp, 0, 0, 0)),
            pl.BlockSpec((None, seq, LANES), resident),
            pl.BlockSpec(tri.shape, lambda b, hp, i: (0, 0)),
        ],
        out_specs=pl.BlockSpec((None, R, LANES), lambda b, hp, i: (b, i, hp)),
        out_shape=jax.ShapeDtypeStruct((batch, seq, W), BF16),
        scratch_shapes=[pltpu.VMEM((R // ATT_SUB, ATT_SUB, LANES), F32),
                        pltpu.VMEM((R // ATT_SUB, ATT_SUB, LANES), F32),
                        pltpu.SMEM((R // ATT_SUB,), F32)],
        compiler_params=pltpu.CompilerParams(
            dimension_semantics=("arbitrary", "arbitrary", "arbitrary"),
            vmem_limit_bytes=VMEM_LIMIT),
        name="stickbreak_attn",
    )(q3, kt, v3, tri)
    return out.reshape(N, W)


def _mlp_kernel(x_ref, yab_ref, yc_ref, g2_ref, gf_ref, wo_hbm, wup_hbm, wdn_hbm, o_ref,
                wo_ref, wup_ref, wdn_ref, sem, *, final_norm, ff_chunk):
    @pl.when(pl.program_id(0) == 0)
    def _():
        copies = [pltpu.make_async_copy(src, dst, sem.at[n])
                  for n, (src, dst) in enumerate(((wo_hbm, wo_ref), (wup_hbm, wup_ref), (wdn_hbm, wdn_ref)))]
        for c in copies:
            c.start()
        for c in copies:
            c.wait()

    nab = yab_ref.shape[1]
    x1 = (x_ref[...]
          + jnp.dot(yab_ref[...], wo_ref[0:nab, :], preferred_element_type=F32)
          + jnp.dot(yc_ref[...], wo_ref[nab:, :], preferred_element_type=F32))
    h = _rms(x1, g2_ref[...]).astype(BF16)
    acc = None
    for c in range(wup_ref.shape[1] // ff_chunk):
        up = jnp.dot(h, wup_ref[:, c * ff_chunk:(c + 1) * ff_chunk], preferred_element_type=F32)
        act = jnp.square(jnp.maximum(up, 0.0)).astype(BF16)
        dn = jnp.dot(act, wdn_ref[c * ff_chunk:(c + 1) * ff_chunk, :], preferred_element_type=F32)
        acc = dn if acc is None else acc + dn
    acc = acc + x1
    if final_norm:
        acc = _rms(acc, gf_ref[...])
    o_ref[...] = acc


def _mlp(x2, yab, yc, wo, g2, wup, wdn, gf, *, final_norm):
    N, D = x2.shape
    T = ROW_TILE
    const = lambda i: (0, 0)
    row = lambda i: (i, 0)
    hbm = pl.BlockSpec(memory_space=pl.ANY)
    return pl.pallas_call(
        functools.partial(_mlp_kernel, final_norm=final_norm, ff_chunk=1024),
        grid=(N // T,),
        in_specs=[
            pl.BlockSpec((T, D), row),
            pl.BlockSpec((T, yab.shape[1]), row),
            pl.BlockSpec((T, yc.shape[1]), row),
            pl.BlockSpec((1, D), const),
            pl.BlockSpec((1, D), const),
            hbm, hbm, hbm,
        ],
        out_specs=pl.BlockSpec((T, D), row),
        out_shape=jax.ShapeDtypeStruct((N, D), F32),
        scratch_shapes=[pltpu.VMEM(wo.shape, BF16), pltpu.VMEM(wup.shape, BF16),
                        pltpu.VMEM(wdn.shape, BF16), pltpu.SemaphoreType.DMA((3,))],
        compiler_params=pltpu.CompilerParams(
            dimension_semantics=("arbitrary",), vmem_limit_bytes=VMEM_LIMIT),
        name="outproj_mlp",
    )(x2, yab, yc, g2, gf, wo, wup, wdn)


def kernel(x, norm1, w_in, pool_w, pool_scale, sg_norm, sg_w, sg_b, w_out, norm2, w_up, w_down, final_norm):
    B, S, D = x.shape
    depth = norm1.shape[0]
    n_grp, gw = pool_w.shape[1], pool_w.shape[2]
    pw, sw = pool_scale.shape[1], sg_norm.shape[1]
    sbw = (w_in.shape[2] - pw - 2 * sw) // 3
    assert S % ROW_TILE == 0 and S % ATT_ROWS == 0 and ROW_TILE % CHUNK == 0 and ROW_TILE % ATT_BLOCK == 0
    assert sg_w.shape[1] == SG_HEADS and sg_w.shape[2] == CHUNK and n_grp == len(POOL_WINDOWS)
    assert sbw % LANES == 0 and ATT_BLOCK == LANES == 2 * ATT_SUB == 2 * SB_HD and (ATT_ROWS // ATT_SUB) % 2 == 0

    x2 = x.reshape(B * S, D)
    ck = pw + 2 * sw + sbw
    for l in range(depth):
        win = jnp.concatenate([w_in[l][:, :ck], w_in[l][:, ck + sbw:]], axis=1).astype(BF16)
        wkt = w_in[l][:, ck:ck + sbw].T.astype(BF16)
        poolw = jax.scipy.linalg.block_diag(*[pool_w[l, g] for g in range(n_grp)]).astype(BF16)
        sgb = jnp.repeat(sg_b[l].T, sw // SG_HEADS, axis=1)
        yab, q, kt, v = _inproj(x2, norm1[l][None], win, wkt, poolw, pool_scale[l][None],
                                sg_norm[l][None], sg_w[l], sgb, batch=B, seq=S)
        yc = _attention(q, kt, v, batch=B, seq=S)
        x2 = _mlp(x2, yab, yc, w_out[l].astype(BF16), norm2[l][None],
                  w_up[l].astype(BF16), w_down[l].astype(BF16), final_norm[None],
                  final_norm=(l == depth - 1))
    return x2.reshape(B, S, D)
```

```python
import functools

import jax
import jax.numpy as jnp
import numpy as np
from jax import lax
from jax.experimental import pallas as pl
from jax.experimental.pallas import tpu as pltpu

EPS = 1e-6
POOL_WINDOWS = (2, 4, 8, 16)
POOL_HALO = 16
CHUNK = 128
SG_HEADS = 4
SB_HD = 64
LANES = 128

ROW_TILE = 512
ATT_BLOCK = 128
ATT_SUB = 64
ATT_ROWS = 4096
ATT_SKEW = 3
STICK_CUTOFF = 30.0
MASK_LOGIT = -1e30
NEG_LOG2E = -1.4426950408889634
VMEM_LIMIT = 56 * 1024 * 1024

F32 = jnp.float32
BF16 = jnp.bfloat16


def _rms(x, g):
    ms = jnp.mean(x * x, axis=-1, keepdims=True)
    return x * lax.rsqrt(ms + EPS) * g


def _gelu_tanh(x):
    c = np.float32(np.sqrt(2.0 / np.pi))
    return 0.5 * x * (1.0 + jnp.tanh(c * (x + 0.044715 * (x * x * x))))


def _inproj_kernel(x_ref, g1_ref, win_ref, wkt_ref, poolw_ref, pscale_ref, sgn_ref, sgw_ref, sgb_ref,
                   yab_ref, q_ref, kt_ref, v_ref, lvl_ref, *, tiles_per_seq):
    T = x_ref.shape[0]
    pw = pscale_ref.shape[1]
    sw = sgn_ref.shape[1]
    i = pl.program_id(0)
    tile_in_seq = i % tiles_per_seq

    H = POOL_HALO

    @pl.when(i == 0)
    def _():
        lvl_ref[:, 0:H, :] = jnp.zeros((lvl_ref.shape[0], H, pw), F32)

    @pl.when(tile_in_seq == 0)
    def _():
        lvl_ref[0, H:2 * H, :] = jnp.zeros((H, pw), F32)

    @pl.when(tile_in_seq != 0)
    def _():
        lvl_ref[0, H:2 * H, :] = lvl_ref[0, T + H:T + 2 * H, :]

    h = _rms(x_ref[...], g1_ref[...]).astype(BF16)
    bpre = jnp.dot(h, win_ref[:, pw:pw + 2 * sw], preferred_element_type=F32)
    a = jnp.dot(h, win_ref[:, 0:pw], preferred_element_type=F32)
    c2 = pw + 2 * sw
    sbw = q_ref.shape[1]
    half = sbw // 2

    def project(out_ref, col, out_col, scale=None):
        r = jnp.dot(h, win_ref[:, col:col + half], preferred_element_type=F32)
        out_ref[:, out_col:out_col + half] = (r if scale is None else r * scale).astype(BF16)

    r_i = lax.broadcasted_iota(jnp.int32, (CHUNK, CHUNK), 0)
    c_i = lax.broadcasted_iota(jnp.int32, (CHUNK, CHUNK), 1)
    tril = c_i <= r_i
    wm = jnp.concatenate([jnp.where(tril, sgw_ref[hh], 0.0).astype(BF16) for hh in range(SG_HEADS)], axis=1)
    head = lax.broadcasted_iota(jnp.int32, (1, sw), 1) // (sw // SG_HEADS)

    def gate_chunk(c):
        bz = _gelu_tanh(bpre[c * CHUNK:(c + 1) * CHUNK, :])
        vc = _rms(bz[:, sw:2 * sw], sgn_ref[...]).astype(BF16)
        vstack = jnp.concatenate([jnp.where(head == hh, vc, jnp.zeros_like(vc)) for hh in range(SG_HEADS)], axis=0)
        sv = jnp.dot(wm, vstack, preferred_element_type=F32) + sgb_ref[...]
        yab_ref[c * CHUNK:(c + 1) * CHUNK, pw:pw + sw] = (bz[:, 0:sw] * sv).astype(BF16)

    q_scale = float(SB_HD ** -0.5)
    pieces = [(q_ref, c2, 0, q_scale), (q_ref, c2 + half, half, q_scale),
              (v_ref, c2 + sbw, 0, None), (v_ref, c2 + sbw + half, half, None)]
    n_chunks = T // CHUNK
    for c in range(max(n_chunks, len(pieces))):
        if c < len(pieces):
            project(*pieces[c])
        if c < n_chunks:
            gate_chunk(c)

    lo, n = H, T + H
    lvl_ref[0, 2 * H:2 * H + T, :] = a
    for s, shift in enumerate((1, 2, 4, 8)):
        lvl_ref[s + 1, lo:lo + n, :] = (lvl_ref[s, lo:lo + n, :]
                                        + lvl_ref[s, lo - shift:lo - shift + n, :])
    lane = lax.broadcasted_iota(jnp.int32, (1, pw), 1)
    grp = lane // (pw // len(POOL_WINDOWS))
    win = jnp.where(grp == 0, 2, jnp.where(grp == 1, 4, jnp.where(grp == 2, 8, 16)))
    psum = jnp.where(grp == 0, lvl_ref[1, 2 * H:2 * H + T, :],
                     jnp.where(grp == 1, lvl_ref[2, 2 * H:2 * H + T, :],
                               jnp.where(grp == 2, lvl_ref[3, 2 * H:2 * H + T, :],
                                         lvl_ref[4, 2 * H:2 * H + T, :])))
    pos = tile_in_seq * T + lax.broadcasted_iota(jnp.int32, (T, 1), 0)
    cnt = jnp.minimum(pos + 1, win).astype(F32)
    d = psum / cnt - a
    ya = jnp.dot(d.astype(BF16), poolw_ref[...], preferred_element_type=F32) * pscale_ref[...]
    yab_ref[:, 0:pw] = ya.astype(BF16)

    kt = lax.dot_general(wkt_ref[...], h, (((1,), (1,)), ((), ())),
                         preferred_element_type=F32).astype(BF16)
    for hp in range(kt_ref.shape[0]):
        for c in range(kt_ref.shape[1]):
            kt_ref[hp, c] = kt[hp * LANES:(hp + 1) * LANES, c * ATT_BLOCK:(c + 1) * ATT_BLOCK]


def _inproj(x2, g1, win, wkt, poolw, pscale, sgn, sgw, sgb, *, batch, seq):
    N, D = x2.shape
    T = ROW_TILE
    pw, sw = pscale.shape[1], sgn.shape[1]
    sbw = wkt.shape[0]
    tps = seq // T
    const = lambda i: (0, 0)
    row = lambda i: (i, 0)
    return pl.pallas_call(
        functools.partial(_inproj_kernel, tiles_per_seq=tps),
        grid=(N // T,),
        in_specs=[
            pl.BlockSpec((T, D), row),
            pl.BlockSpec((1, D), const),
            pl.BlockSpec(win.shape, const),
            pl.BlockSpec(wkt.shape, const),
            pl.BlockSpec(poolw.shape, const),
            pl.BlockSpec((1, pw), const),
            pl.BlockSpec((1, sw), const),
            pl.BlockSpec(sgw.shape, lambda i: (0, 0, 0)),
            pl.BlockSpec(sgb.shape, const),
        ],
        out_specs=[
            pl.BlockSpec((T, pw + sw), row),
            pl.BlockSpec((T, sbw), row),
            pl.BlockSpec((None, sbw // LANES, T // ATT_BLOCK, LANES, ATT_BLOCK),
                         lambda i: (i // tps, 0, i % tps, 0, 0)),
            pl.BlockSpec((T, sbw), row),
        ],
        out_shape=[
            jax.ShapeDtypeStruct((N, pw + sw), BF16),
            jax.ShapeDtypeStruct((N, sbw), BF16),
            jax.ShapeDtypeStruct((batch, sbw // LANES, seq // ATT_BLOCK, LANES, ATT_BLOCK), BF16),
            jax.ShapeDtypeStruct((N, sbw), BF16),
        ],
        scratch_shapes=[pltpu.VMEM((len(POOL_WINDOWS) + 1, T + 2 * POOL_HALO, pw), F32)],
        compiler_params=pltpu.CompilerParams(
            dimension_semantics=("arbitrary",), vmem_limit_bytes=VMEM_LIMIT),
        name="inproj_mixers",
    )(x2, g1, win, wkt, poolw, pscale, sgn, sgw, sgb)


def _attn_kernel(q_ref, kt_ref, v_ref, tri_ref, o_ref, acc_ref, carry_ref, cmax_ref):
    SB = ATT_SUB
    NS = o_ref.shape[0] // SB
    i = pl.program_id(2)
    kb0 = i * NS
    head0_lane = lax.broadcasted_iota(jnp.int32, (1, LANES), 1) < SB_HD
    row = lax.broadcasted_iota(jnp.int32, (SB, LANES), 0)
    col = lax.broadcasted_iota(jnp.int32, (SB, LANES), 1) & (SB - 1)
    bias_diag = jnp.where(col < row, 0.0, MASK_LOGIT)
    tri = tri_ref[...]

    def key_rhs(ktile, rolled, half):
        h0, h1 = (ktile, rolled) if half == 0 else (rolled, ktile)
        zero = jnp.zeros((SB_HD, LANES), ktile.dtype)
        return jnp.concatenate([jnp.where(head0_lane, h0[:SB_HD], zero),
                                jnp.where(head0_lane, zero, h1[SB_HD:])], axis=0)

    def value_rhs(kb):
        vrows = v_ref[pl.ds(pl.multiple_of(kb * SB, SB), SB), :]
        zero = jnp.zeros_like(vrows)
        return jnp.concatenate([jnp.where(head0_lane, vrows, zero),
                                jnp.where(head0_lane, zero, vrows)], axis=0)

    def logits_stage(qrows, krhs, diagonal_rows):
        z = jnp.dot(qrows, krhs, preferred_element_type=F32)
        if diagonal_rows == z.shape[0]:
            z = z + bias_diag
        elif diagonal_rows:
            z = jnp.concatenate([z[:diagonal_rows] + bias_diag, z[diagonal_rows:]], axis=0)
        neg_log_1m = jnp.maximum(z, 0.0) + jnp.log(1.0 + jnp.exp2(jnp.abs(z) * NEG_LOG2E))
        hi = neg_log_1m.astype(BF16)
        lo = (neg_log_1m - hi.astype(F32)).astype(BF16)
        return z, jnp.concatenate([hi, lo], axis=1)

    def weights_stage(z, hilo):
        ext = jnp.dot(hilo, tri, preferred_element_type=F32)
        return jnp.exp(z + ext[:, :LANES]).astype(BF16), ext[:, LANES:]

    def output_stage(a, vrhs):
        return jnp.dot(a, vrhs, preferred_element_type=F32)

    def tile(qrows, krhs, vrhs):
        a, rs = weights_stage(*logits_stage(qrows, krhs, 0))
        return output_stage(a, vrhs), rs

    def q_rows(first, count):
        return q_ref[first * SB:(first + count) * SB, :]

    ktiles = {}
    for t in range(-1, NS // 2):
        kt_tile = kt_ref[jnp.maximum(kb0 // 2 + t, 0)]
        ktiles[t] = (kt_tile, pltpu.roll(kt_tile, SB, 1))

    def first_stage(m):
        krhs = key_rhs(*ktiles[m // 2], m % 2)
        if m == -1:
            return logits_stage(q_rows(0, 1), krhs, 0)
        if m == NS - 1:
            return logits_stage(q_rows(m, 1), krhs, SB)
        return logits_stage(q_rows(m, 2), krhs, SB)

    blocks = list(range(-1, NS))
    stage1, stage2, parts = {}, {}, []
    for s in range(len(blocks) + 2 * ATT_SKEW):
        if s < len(blocks):
            stage1[s] = first_stage(blocks[s])
        if 0 <= s - ATT_SKEW < len(blocks):
            stage2[s - ATT_SKEW] = weights_stage(*stage1.pop(s - ATT_SKEW))
        if 0 <= s - 2 * ATT_SKEW < len(blocks):
            a, rs = stage2.pop(s - 2 * ATT_SKEW)
            vrhs = value_rhs(jnp.maximum(kb0 + blocks[s - 2 * ATT_SKEW], 0))
            parts.append((output_stage(a, vrhs), rs))

    carry_max = None
    for r in range(NS):
        pv_p, rs_p = parts[r] if r == 0 else (parts[r][0][SB:], parts[r][1][SB:])
        pv_d, rs_d = parts[r + 1] if r == NS - 1 else (parts[r + 1][0][:SB], parts[r + 1][1][:SB])
        scale = jnp.exp(rs_d)
        if r == 0:
            scale = jnp.where(kb0 > 0, scale, 0.0)
        acc_ref[r] = pv_d + scale * pv_p
        carry_ref[r] = rs_d + rs_p
        carry_max = rs_d + rs_p if carry_max is None else jnp.maximum(carry_max, rs_d + rs_p)

    def sub_block(r, _):
        qr = q_ref[pl.ds(pl.multiple_of(r * SB, SB), SB), :]

        def cond(st):
            j, cmax = st
            return jnp.logical_and(j >= 0, cmax > -STICK_CUTOFF)

        def body(st):
            j, _ = st
            kt_tile = kt_ref[lax.shift_right_logical(j, 1)]
            rolled = pltpu.roll(kt_tile, SB, 1)
            krhs = jnp.where((j & 1) == 0, key_rhs(kt_tile, rolled, 0), key_rhs(kt_tile, rolled, 1))
            pv, rs = tile(qr, krhs, value_rhs(j))
            carry = carry_ref[r]
            acc_ref[r] += jnp.exp(carry) * pv
            carry_ref[r] = carry + rs
            return j - 1, jnp.max(carry + rs)

        @pl.when(cmax_ref[r] > -STICK_CUTOFF)
        def _():
            lax.while_loop(cond, body, (kb0 + r - 2, cmax_ref[r]))

        return 0

    @pl.when(jnp.max(carry_max) > -STICK_CUTOFF)
    def _():
        for r in range(NS):
            cmax_ref[r] = jnp.max(carry_ref[r])
        lax.fori_loop(0, NS, sub_block, 0)

    for r in range(NS):
        o_ref[r * SB:(r + 1) * SB, :] = acc_ref[r].astype(o_ref.dtype)


def _stick_matrix():
    key = np.arange(LANES) % ATT_SUB
    head = np.arange(LANES) // ATT_SUB
    same = head[:, None] == head[None, :]
    suffix = np.logical_and(same, key[:, None] >= key[None, :])
    half = -np.concatenate([suffix, same], axis=1).astype(np.float32)
    return jnp.asarray(np.concatenate([half, half], axis=0), BF16)


def _attention(q, kt, v, *, batch, seq):
    N, W = q.shape
    R = ATT_ROWS
    q3, v3 = q.reshape(batch, seq, W), v.reshape(batch, seq, W)
    tri = _stick_matrix()
    resident = lambda b, hp, i: (b, 0, hp)
    out = pl.pallas_call(
        _attn_kernel,
        grid=(batch, W // LANES, seq // R),
        in_specs=[
            pl.BlockSpec((None, R, LANES), lambda b, hp, i: (b, i, hp)),
            pl.BlockSpec((None, None, seq // ATT_BLOCK, LANES, ATT_BLOCK), lambda b, hp, i: (b, hp, 0, 0, 0)),
            pl.BlockSpec((None, seq, LANES), resident),
            pl.BlockSpec(tri.shape, lambda b, hp, i: (0, 0)),
        ],
        out_specs=pl.BlockSpec((None, R, LANES), lambda b, hp, i: (b, i, hp)),
        out_shape=jax.ShapeDtypeStruct((batch, seq, W), BF16),
        scratch_shapes=[pltpu.VMEM((R // ATT_SUB, ATT_SUB, LANES), F32),
                        pltpu.VMEM((R // ATT_SUB, ATT_SUB, LANES), F32),
                        pltpu.SMEM((R // ATT_SUB,), F32)],
        compiler_params=pltpu.CompilerParams(
            dimension_semantics=("arbitrary", "arbitrary", "arbitrary"),
            vmem_limit_bytes=VMEM_LIMIT),
        name="stickbreak_attn",
    )(q3, kt, v3, tri)
    return out.reshape(N, W)


def _mlp_kernel(x_ref, yab_ref, yc_ref, g2_ref, gf_ref, wo_hbm, wup_hbm, wdn_hbm, o_ref,
                wo_ref, wup_ref, wdn_ref, sem, *, final_norm, ff_chunk):
    @pl.when(pl.program_id(0) == 0)
    def _():
        copies = [pltpu.make_async_copy(src, dst, sem.at[n])
                  for n, (src, dst) in enumerate(((wo_hbm, wo_ref), (wup_hbm, wup_ref), (wdn_hbm, wdn_ref)))]
        for c in copies:
            c.start()
        for c in copies:
            c.wait()

    nab = yab_ref.shape[1]
    x1 = (x_ref[...]
          + jnp.dot(yab_ref[...], wo_ref[0:nab, :], preferred_element_type=F32)
          + jnp.dot(yc_ref[...], wo_ref[nab:, :], preferred_element_type=F32))
    h = _rms(x1, g2_ref[...]).astype(BF16)
    acc = None
    for c in range(wup_ref.shape[1] // ff_chunk):
        up = jnp.dot(h, wup_ref[:, c * ff_chunk:(c + 1) * ff_chunk], preferred_element_type=F32)
        act = jnp.square(jnp.maximum(up, 0.0)).astype(BF16)
        dn = jnp.dot(act, wdn_ref[c * ff_chunk:(c + 1) * ff_chunk, :], preferred_element_type=F32)
        acc = dn if acc is None else acc + dn
    acc = acc + x1
    if final_norm:
        acc = _rms(acc, gf_ref[...])
    o_ref[...] = acc


def _mlp(x2, yab, yc, wo, g2, wup, wdn, gf, *, final_norm):
    N, D = x2.shape
    T = ROW_TILE
    const = lambda i: (0, 0)
    row = lambda i: (i, 0)
    hbm = pl.BlockSpec(memory_space=pl.ANY)
    return pl.pallas_call(
        functools.partial(_mlp_kernel, final_norm=final_norm, ff_chunk=1024),
        grid=(N // T,),
        in_specs=[
            pl.BlockSpec((T, D), row),
            pl.BlockSpec((T, yab.shape[1]), row),
            pl.BlockSpec((T, yc.shape[1]), row),
            pl.BlockSpec((1, D), const),
            pl.BlockSpec((1, D), const),
            hbm, hbm, hbm,
        ],
        out_specs=pl.BlockSpec((T, D), row),
        out_shape=jax.ShapeDtypeStruct((N, D), F32),
        scratch_shapes=[pltpu.VMEM(wo.shape, BF16), pltpu.VMEM(wup.shape, BF16),
                        pltpu.VMEM(wdn.shape, BF16), pltpu.SemaphoreType.DMA((3,))],
        compiler_params=pltpu.CompilerParams(
            dimension_semantics=("arbitrary",), vmem_limit_bytes=VMEM_LIMIT),
        name="outproj_mlp",
    )(x2, yab, yc, g2, gf, wo, wup, wdn)


def kernel(x, norm1, w_in, pool_w, pool_scale, sg_norm, sg_w, sg_b, w_out, norm2, w_up, w_down, final_norm):
    B, S, D = x.shape
    depth = norm1.shape[0]
    n_grp, gw = pool_w.shape[1], pool_w.shape[2]
    pw, sw = pool_scale.shape[1], sg_norm.shape[1]
    sbw = (w_in.shape[2] - pw - 2 * sw) // 3
    assert S % ROW_TILE == 0 and S % ATT_ROWS == 0 and ROW_TILE % CHUNK == 0 and ROW_TILE % ATT_BLOCK == 0
    assert sg_w.shape[1] == SG_HEADS and sg_w.shape[2] == CHUNK and n_grp == len(POOL_WINDOWS)
    assert sbw % LANES == 0 and ATT_BLOCK == LANES == 2 * ATT_SUB == 2 * SB_HD and (ATT_ROWS // ATT_SUB) % 2 == 0

    x2 = x.reshape(B * S, D)
    ck = pw + 2 * sw + sbw
    for l in range(depth):
        win = jnp.concatenate([w_in[l][:, :ck], w_in[l][:, ck + sbw:]], axis=1).astype(BF16)
        wkt = w_in[l][:, ck:ck + sbw].T.astype(BF16)
        poolw = jax.scipy.linalg.block_diag(*[pool_w[l, g] for g in range(n_grp)]).astype(BF16)
        sgb = jnp.repeat(sg_b[l].T, sw // SG_HEADS, axis=1)
        yab, q, kt, v = _inproj(x2, norm1[l][None], win, wkt, poolw, pool_scale[l][None],
                                sg_norm[l][None], sg_w[l], sgb, batch=B, seq=S)
        yc = _attention(q, kt, v, batch=B, seq=S)
        x2 = _mlp(x2, yab, yc, w_out[l].astype(BF16), norm2[l][None],
                  w_up[l].astype(BF16), w_down[l].astype(BF16), final_norm[None],
                  final_norm=(l == depth - 1))
    return x2.reshape(B, S, D)
```

```python
import functools

import jax
import jax.numpy as jnp
import numpy as np
from jax import lax
from jax.experimental import pallas as pl
from jax.experimental.pallas import tpu as pltpu

EPS = 1e-6
POOL_WINDOWS = (2, 4, 8, 16)
POOL_HALO = 16
CHUNK = 128
SG_HEADS = 4
SB_HD = 64
LANES = 128

ROW_TILE = 1024
ATT_BLOCK = 128
ATT_SUB = 64
ATT_ROWS = 4096
ATT_SKEW = 3
STICK_CUTOFF = 30.0
MASK_LOGIT = -1e30
NEG_LOG2E = -1.4426950408889634
VMEM_LIMIT = 56 * 1024 * 1024

F32 = jnp.float32
BF16 = jnp.bfloat16


def _rms(x, g):
    ms = jnp.mean(x * x, axis=-1, keepdims=True)
    return x * lax.rsqrt(ms + EPS) * g


def _gelu_tanh(x):
    c = np.float32(np.sqrt(2.0 / np.pi))
    return 0.5 * x * (1.0 + jnp.tanh(c * (x + 0.044715 * (x * x * x))))


def _inproj_kernel(x_ref, g1_ref, win_ref, wkt_ref, poolw_ref, pscale_ref, sgn_ref, sgw_ref, sgb_ref,
                   yab_ref, q_ref, kt_ref, v_ref, lvl_ref, *, tiles_per_seq):
    T = x_ref.shape[0]
    pw = pscale_ref.shape[1]
    sw = sgn_ref.shape[1]
    i = pl.program_id(0)
    tile_in_seq = i % tiles_per_seq

    H = POOL_HALO

    @pl.when(i == 0)
    def _():
        lvl_ref[:, 0:H, :] = jnp.zeros((lvl_ref.shape[0], H, pw), F32)

    @pl.when(tile_in_seq == 0)
    def _():
        lvl_ref[0, H:2 * H, :] = jnp.zeros((H, pw), F32)

    @pl.when(tile_in_seq != 0)
    def _():
        lvl_ref[0, H:2 * H, :] = lvl_ref[0, T + H:T + 2 * H, :]

    h = _rms(x_ref[...], g1_ref[...]).astype(BF16)
    bpre = jnp.dot(h, win_ref[:, pw:pw + 2 * sw], preferred_element_type=F32)
    a = jnp.dot(h, win_ref[:, 0:pw], preferred_element_type=F32)
    c2 = pw + 2 * sw
    sbw = q_ref.shape[1]
    half = sbw // 2

    def project(out_ref, col, out_col, scale=None):
        r = jnp.dot(h, win_ref[:, col:col + half], preferred_element_type=F32)
        out_ref[:, out_col:out_col + half] = (r if scale is None else r * scale).astype(BF16)

    r_i = lax.broadcasted_iota(jnp.int32, (CHUNK, CHUNK), 0)
    c_i = lax.broadcasted_iota(jnp.int32, (CHUNK, CHUNK), 1)
    tril = c_i <= r_i
    wm = jnp.concatenate([jnp.where(tril, sgw_ref[hh], 0.0).astype(BF16) for hh in range(SG_HEADS)], axis=1)
    head = lax.broadcasted_iota(jnp.int32, (1, sw), 1) // (sw // SG_HEADS)

    def gate_chunk(c):
        bz = _gelu_tanh(bpre[c * CHUNK:(c + 1) * CHUNK, :])
        vc = _rms(bz[:, sw:2 * sw], sgn_ref[...]).astype(BF16)
        vstack = jnp.concatenate([jnp.where(head == hh, vc, jnp.zeros_like(vc)) for hh in range(SG_HEADS)], axis=0)
        sv = jnp.dot(wm, vstack, preferred_element_type=F32) + sgb_ref[...]
        yab_ref[c * CHUNK:(c + 1) * CHUNK, pw:pw + sw] = (bz[:, 0:sw] * sv).astype(BF16)

    q_scale = float(SB_HD ** -0.5)
    pieces = [(q_ref, c2, 0, q_scale), (q_ref, c2 + half, half, q_scale),
              (v_ref, c2 + sbw, 0, None), (v_ref, c2 + sbw + half, half, None)]
    n_chunks = T // CHUNK
    per_piece = -(-n_chunks // len(pieces))
    for p, piece in enumerate(pieces):
        project(*piece)
        for c in range(p * per_piece, min((p + 1) * per_piece, n_chunks)):
            gate_chunk(c)

    lo, n = H, T + H
    lvl_ref[0, 2 * H:2 * H + T, :] = a
    for s, shift in enumerate((1, 2, 4, 8)):
        lvl_ref[s + 1, lo:lo + n, :] = (lvl_ref[s, lo:lo + n, :]
                                        + lvl_ref[s, lo - shift:lo - shift + n, :])
    lane = lax.broadcasted_iota(jnp.int32, (1, pw), 1)
    grp = lane // (pw // len(POOL_WINDOWS))
    win = jnp.where(grp == 0, 2, jnp.where(grp == 1, 4, jnp.where(grp == 2, 8, 16)))
    psum = jnp.where(grp == 0, lvl_ref[1, 2 * H:2 * H + T, :],
                     jnp.where(grp == 1, lvl_ref[2, 2 * H:2 * H + T, :],
                               jnp.where(grp == 2, lvl_ref[3, 2 * H:2 * H + T, :],
                                         lvl_ref[4, 2 * H:2 * H + T, :])))
    pos = tile_in_seq * T + lax.broadcasted_iota(jnp.int32, (T, 1), 0)
    cnt = jnp.minimum(pos + 1, win).astype(F32)
    d = psum / cnt - a
    ya = jnp.dot(d.astype(BF16), poolw_ref[...], preferred_element_type=F32) * pscale_ref[...]
    yab_ref[:, 0:pw] = ya.astype(BF16)

    kt = lax.dot_general(wkt_ref[...], h, (((1,), (1,)), ((), ())),
                         preferred_element_type=F32).astype(BF16)
    for hp in range(kt_ref.shape[0]):
        for c in range(kt_ref.shape[1]):
            kt_ref[hp, c] = kt[hp * LANES:(hp + 1) * LANES, c * ATT_BLOCK:(c + 1) * ATT_BLOCK]


def _inproj(x2, g1, win, wkt, poolw, pscale, sgn, sgw, sgb, *, batch, seq):
    N, D = x2.shape
    T = ROW_TILE
    pw, sw = pscale.shape[1], sgn.shape[1]
    sbw = wkt.shape[0]
    tps = seq // T
    const = lambda i: (0, 0)
    row = lambda i: (i, 0)
    return pl.pallas_call(
        functools.partial(_inproj_kernel, tiles_per_seq=tps),
        grid=(N // T,),
        in_specs=[
            pl.BlockSpec((T, D), row),
            pl.BlockSpec((1, D), const),
            pl.BlockSpec(win.shape, const),
            pl.BlockSpec(wkt.shape, const),
            pl.BlockSpec(poolw.shape, const),
            pl.BlockSpec((1, pw), const),
            pl.BlockSpec((1, sw), const),
            pl.BlockSpec(sgw.shape, lambda i: (0, 0, 0)),
            pl.BlockSpec(sgb.shape, const),
        ],
        out_specs=[
            pl.BlockSpec((T, pw + sw), row),
            pl.BlockSpec((T, sbw), row),
            pl.BlockSpec((None, sbw // LANES, T // ATT_BLOCK, LANES, ATT_BLOCK),
                         lambda i: (i // tps, 0, i % tps, 0, 0)),
            pl.BlockSpec((T, sbw), row),
        ],
        out_shape=[
            jax.ShapeDtypeStruct((N, pw + sw), BF16),
            jax.ShapeDtypeStruct((N, sbw), BF16),
            jax.ShapeDtypeStruct((batch, sbw // LANES, seq // ATT_BLOCK, LANES, ATT_BLOCK), BF16),
            jax.ShapeDtypeStruct((N, sbw), BF16),
        ],
        scratch_shapes=[pltpu.VMEM((len(POOL_WINDOWS) + 1, T + 2 * POOL_HALO, pw), F32)],
        compiler_params=pltpu.CompilerParams(
            dimension_semantics=("arbitrary",), vmem_limit_bytes=VMEM_LIMIT),
        name="inproj_mixers",
    )(x2, g1, win, wkt, poolw, pscale, sgn, sgw, sgb)


def _attn_kernel(q_ref, kt_ref, v_ref, tri_ref, o_ref, acc_ref, carry_ref, cmax_ref):
    SB = ATT_SUB
    NS = o_ref.shape[0] // SB
    i = pl.program_id(2)
    kb0 = i * NS
    head0_lane = lax.broadcasted_iota(jnp.int32, (1, LANES), 1) < SB_HD
    row = lax.broadcasted_iota(jnp.int32, (SB, LANES), 0)
    col = lax.broadcasted_iota(jnp.int32, (SB, LANES), 1) & (SB - 1)
    bias_diag = jnp.where(col < row, 0.0, MASK_LOGIT)
    tri = tri_ref[...]

    def key_rhs(ktile, rolled, half):
        h0, h1 = (ktile, rolled) if half == 0 else (rolled, ktile)
        zero = jnp.zeros((SB_HD, LANES), ktile.dtype)
        return jnp.concatenate([jnp.where(head0_lane, h0[:SB_HD], zero),
                                jnp.where(head0_lane, zero, h1[SB_HD:])], axis=0)

    def value_rhs(kb):
        vrows = v_ref[pl.ds(pl.multiple_of(kb * SB, SB), SB), :]
        zero = jnp.zeros_like(vrows)
        return jnp.concatenate([jnp.where(head0_lane, vrows, zero),
                                jnp.where(head0_lane, zero, vrows)], axis=0)

    def logits_stage(qrows, krhs, diagonal_rows):
        z = jnp.dot(qrows, krhs, preferred_element_type=F32)
        if diagonal_rows == z.shape[0]:
            z = z + bias_diag
        elif diagonal_rows:
            z = jnp.concatenate([z[:diagonal_rows] + bias_diag, z[diagonal_rows:]], axis=0)
        neg_log_1m = jnp.maximum(z, 0.0) + jnp.log(1.0 + jnp.exp2(jnp.abs(z) * NEG_LOG2E))
        hi = neg_log_1m.astype(BF16)
        lo = (neg_log_1m - hi.astype(F32)).astype(BF16)
        return z, jnp.concatenate([hi, lo], axis=1)

    def weights_stage(z, hilo):
        ext = jnp.dot(hilo, tri, preferred_element_type=F32)
        return jnp.exp(z + ext[:, :LANES]).astype(BF16), ext[:, LANES:]

    def output_stage(a, vrhs):
        return jnp.dot(a, vrhs, preferred_element_type=F32)

    def tile(qrows, krhs, vrhs):
        a, rs = weights_stage(*logits_stage(qrows, krhs, 0))
        return output_stage(a, vrhs), rs

    def q_rows(first, count):
        return q_ref[first * SB:(first + count) * SB, :]

    ktiles = {}
    for t in range(-1, NS // 2):
        kt_tile = kt_ref[jnp.maximum(kb0 // 2 + t, 0)]
        ktiles[t] = (kt_tile, pltpu.roll(kt_tile, SB, 1))

    def first_stage(m):
        krhs = key_rhs(*ktiles[m // 2], m % 2)
        if m == -1:
            return logits_stage(q_rows(0, 1), krhs, 0)
        if m == NS - 1:
            return logits_stage(q_rows(m, 1), krhs, SB)
        return logits_stage(q_rows(m, 2), krhs, SB)

    blocks = list(range(-1, NS))
    stage1, stage2, parts = {}, {}, []
    for s in range(len(blocks) + 2 * ATT_SKEW):
        if s < len(blocks):
            stage1[s] = first_stage(blocks[s])
        if 0 <= s - ATT_SKEW < len(blocks):
            stage2[s - ATT_SKEW] = weights_stage(*stage1.pop(s - ATT_SKEW))
        if 0 <= s - 2 * ATT_SKEW < len(blocks):
            a, rs = stage2.pop(s - 2 * ATT_SKEW)
            vrhs = value_rhs(jnp.maximum(kb0 + blocks[s - 2 * ATT_SKEW], 0))
            parts.append((output_stage(a, vrhs), rs))

    carry_max = None
    for r in range(NS):
        pv_p, rs_p = parts[r] if r == 0 else (parts[r][0][SB:], parts[r][1][SB:])
        pv_d, rs_d = parts[r + 1] if r == NS - 1 else (parts[r + 1][0][:SB], parts[r + 1][1][:SB])
        scale = jnp.exp(rs_d)
        if r == 0:
            scale = jnp.where(kb0 > 0, scale, 0.0)
        acc_ref[r] = pv_d + scale * pv_p
        carry_ref[r] = rs_d + rs_p
        carry_max = rs_d + rs_p if carry_max is None else jnp.maximum(carry_max, rs_d + rs_p)

    def sub_block(r, _):
        qr = q_ref[pl.ds(pl.multiple_of(r * SB, SB), SB), :]

        def cond(st):
            j, cmax = st
            return jnp.logical_and(j >= 0, cmax > -STICK_CUTOFF)

        def body(st):
            j, _ = st
            kt_tile = kt_ref[lax.shift_right_logical(j, 1)]
            rolled = pltpu.roll(kt_tile, SB, 1)
            krhs = jnp.where((j & 1) == 0, key_rhs(kt_tile, rolled, 0), key_rhs(kt_tile, rolled, 1))
            pv, rs = tile(qr, krhs, value_rhs(j))
            carry = carry_ref[r]
            acc_ref[r] += jnp.exp(carry) * pv
            carry_ref[r] = carry + rs
            return j - 1, jnp.max(carry + rs)

        @pl.when(cmax_ref[r] > -STICK_CUTOFF)
        def _():
            lax.while_loop(cond, body, (kb0 + r - 2, cmax_ref[r]))

        return 0

    @pl.when(jnp.max(carry_max) > -STICK_CUTOFF)
    def _():
        for r in range(NS):
            cmax_ref[r] = jnp.max(carry_ref[r])
        lax.fori_loop(0, NS, sub_block, 0)

    for r in range(NS):
        o_ref[r * SB:(r + 1) * SB, :] = acc_ref[r].astype(o_ref.dtype)


def _stick_matrix():
    key = np.arange(LANES) % ATT_SUB
    head = np.arange(LANES) // ATT_SUB
    same = head[:, None] == head[None, :]
    suffix = np.logical_and(same, key[:, None] >= key[None, :])
    half = -np.concatenate([suffix, same], axis=1).astype(np.float32)
    return jnp.asarray(np.concatenate([half, half], axis=0), BF16)


def _attention(q, kt, v, *, batch, seq):
    N, W = q.shape
    R = ATT_ROWS
    q3, v3 = q.reshape(batch, seq, W), v.reshape(batch, seq, W)
    tri = _stick_matrix()
    resident = lambda b, hp, i: (b, 0, hp)
    out = pl.pallas_call(
        _attn_kernel,
        grid=(batch, W // LANES, seq // R),
        in_specs=[
            pl.BlockSpec((None, R, LANES), lambda b, hp, i: (b, i, hp)),
            pl.BlockSpec((None, None, seq // ATT_BLOCK, LANES, ATT_BLOCK), lambda b, hp, i: (b, hp, 0, 0, 0)),
            pl.BlockSpec((None, seq, LANES), resident),
            pl.BlockSpec(tri.shape, lambda b, hp, i: (0, 0)),
        ],
        out_specs=pl.BlockSpec((None, R, LANES), lambda b, hp, i: (b, i, hp)),
        out_shape=jax.ShapeDtypeStruct((batch, seq, W), BF16),
        scratch_shapes=[pltpu.VMEM((R // ATT_SUB, ATT_SUB, LANES), F32),
                        pltpu.VMEM((R // ATT_SUB, ATT_SUB, LANES), F32),
                        pltpu.SMEM((R // ATT_SUB,), F32)],
        compiler_params=pltpu.CompilerParams(
            dimension_semantics=("arbitrary", "arbitrary", "arbitrary"),
            vmem_limit_bytes=VMEM_LIMIT),
        name="stickbreak_attn",
    )(q3, kt, v3, tri)
    return out.reshape(N, W)


def _mlp_kernel(x_ref, yab_ref, yc_ref, g2_ref, gf_ref, wo_hbm, wup_hbm, wdn_hbm, o_ref,
                wo_ref, wup_ref, wdn_ref, sem, *, final_norm, ff_chunk):
    @pl.when(pl.program_id(0) == 0)
    def _():
        copies = [pltpu.make_async_copy(src, dst, sem.at[n])
                  for n, (src, dst) in enumerate(((wo_hbm, wo_ref), (wup_hbm, wup_ref), (wdn_hbm, wdn_ref)))]
        for c in copies:
            c.start()
        for c in copies:
            c.wait()

    nab = yab_ref.shape[1]
    x1 = (x_ref[...]
          + jnp.dot(yab_ref[...], wo_ref[0:nab, :], preferred_element_type=F32)
          + jnp.dot(yc_ref[...], wo_ref[nab:, :], preferred_element_type=F32))
    h = _rms(x1, g2_ref[...]).astype(BF16)
    acc = None
    for c in range(wup_ref.shape[1] // ff_chunk):
        up = jnp.dot(h, wup_ref[:, c * ff_chunk:(c + 1) * ff_chunk], preferred_element_type=F32)
        act = jnp.square(jnp.maximum(up, 0.0)).astype(BF16)
        dn = jnp.dot(act, wdn_ref[c * ff_chunk:(c + 1) * ff_chunk, :], preferred_element_type=F32)
        acc = dn if acc is None else acc + dn
    acc = acc + x1
    if final_norm:
        acc = _rms(acc, gf_ref[...])
    o_ref[...] = acc


def _mlp(x2, yab, yc, wo, g2, wup, wdn, gf, *, final_norm):
    N, D = x2.shape
    T = ROW_TILE
    const = lambda i: (0, 0)
    row = lambda i: (i, 0)
    hbm = pl.BlockSpec(memory_space=pl.ANY)
    return pl.pallas_call(
        functools.partial(_mlp_kernel, final_norm=final_norm, ff_chunk=1024),
        grid=(N // T,),
        in_specs=[
            pl.BlockSpec((T, D), row),
            pl.BlockSpec((T, yab.shape[1]), row),
            pl.BlockSpec((T, yc.shape[1]), row),
            pl.BlockSpec((1, D), const),
            pl.BlockSpec((1, D), const),
            hbm, hbm, hbm,
        ],
        out_specs=pl.BlockSpec((T, D), row),
        out_shape=jax.ShapeDtypeStruct((N, D), F32),
        scratch_shapes=[pltpu.VMEM(wo.shape, BF16), pltpu.VMEM(wup.shape, BF16),
                        pltpu.VMEM(wdn.shape, BF16), pltpu.SemaphoreType.DMA((3,))],
        compiler_params=pltpu.CompilerParams(
            dimension_semantics=("arbitrary",), vmem_limit_bytes=VMEM_LIMIT),
        name="outproj_mlp",
    )(x2, yab, yc, g2, gf, wo, wup, wdn)


def kernel(x, norm1, w_in, pool_w, pool_scale, sg_norm, sg_w, sg_b, w_out, norm2, w_up, w_down, final_norm):
    B, S, D = x.shape
    depth = norm1.shape[0]
    n_grp, gw = pool_w.shape[1], pool_w.shape[2]
    pw, sw = pool_scale.shape[1], sg_norm.shape[1]
    sbw = (w_in.shape[2] - pw - 2 * sw) // 3
    assert S % ROW_TILE == 0 and S % ATT_ROWS == 0 and ROW_TILE % CHUNK == 0 and ROW_TILE % ATT_BLOCK == 0
    assert sg_w.shape[1] == SG_HEADS and sg_w.shape[2] == CHUNK and n_grp == len(POOL_WINDOWS)
    assert sbw % LANES == 0 and ATT_BLOCK == LANES == 2 * ATT_SUB == 2 * SB_HD and (ATT_ROWS // ATT_SUB) % 2 == 0

    x2 = x.reshape(B * S, D)
    ck = pw + 2 * sw + sbw
    for l in range(depth):
        win = jnp.concatenate([w_in[l][:, :ck], w_in[l][:, ck + sbw:]], axis=1).astype(BF16)
        wkt = w_in[l][:, ck:ck + sbw].T.astype(BF16)
        poolw = jax.scipy.linalg.block_diag(*[pool_w[l, g] for g in range(n_grp)]).astype(BF16)
        sgb = jnp.repeat(sg_b[l].T, sw // SG_HEADS, axis=1)
        yab, q, kt, v = _inproj(x2, norm1[l][None], win, wkt, poolw, pool_scale[l][None],
                                sg_norm[l][None], sg_w[l], sgb, batch=B, seq=S)
        yc = _attention(q, kt, v, batch=B, seq=S)
        x2 = _mlp(x2, yab, yc, w_out[l].astype(BF16), norm2[l][None],
                  w_up[l].astype(BF16), w_down[l].astype(BF16), final_norm[None],
                  final_norm=(l == depth - 1))
    return x2.reshape(B, S, D)
```

```python
import functools

import jax
import jax.numpy as jnp
import numpy as np
from jax import lax
from jax.experimental import pallas as pl
from jax.experimental.pallas import tpu as pltpu

EPS = 1e-6
POOL_WINDOWS = (2, 4, 8, 16)
POOL_HALO = 16
CHUNK = 128
SG_HEADS = 4
SB_HD = 64
LANES = 128

ROW_TILE = 1024
ATT_BLOCK = 128
ATT_SUB = 64
ATT_ROWS = 4096
ATT_SKEW = 3
STICK_CUTOFF = 30.0
MASK_LOGIT = -1e30
LOGIT_CAP = 80.0
LOG2E = 1.4426950408889634
VMEM_LIMIT = 56 * 1024 * 1024

F32 = jnp.float32
BF16 = jnp.bfloat16


def _rms(x, g):
    ms = jnp.mean(x * x, axis=-1, keepdims=True)
    return x * lax.rsqrt(ms + EPS) * g


def _gelu_tanh(x):
    c = np.float32(np.sqrt(2.0 / np.pi))
    return 0.5 * x * (1.0 + jnp.tanh(c * (x + 0.044715 * (x * x * x))))


def _inproj_kernel(x_ref, g1_ref, win_ref, wkt_ref, poolw_ref, pscale_ref, sgn_ref, sgw_ref, sgb_ref,
                   yab_ref, q_ref, kt_ref, v_ref, lvl_ref, *, tiles_per_seq):
    T = x_ref.shape[0]
    pw = pscale_ref.shape[1]
    sw = sgn_ref.shape[1]
    i = pl.program_id(0)
    tile_in_seq = i % tiles_per_seq

    H = POOL_HALO

    @pl.when(i == 0)
    def _():
        lvl_ref[:, 0:H, :] = jnp.zeros((lvl_ref.shape[0], H, pw), F32)

    @pl.when(tile_in_seq == 0)
    def _():
        lvl_ref[0, H:2 * H, :] = jnp.zeros((H, pw), F32)

    @pl.when(tile_in_seq != 0)
    def _():
        lvl_ref[0, H:2 * H, :] = lvl_ref[0, T + H:T + 2 * H, :]

    h = _rms(x_ref[...], g1_ref[...]).astype(BF16)
    bpre = jnp.dot(h, win_ref[:, pw:pw + 2 * sw], preferred_element_type=F32)
    a = jnp.dot(h, win_ref[:, 0:pw], preferred_element_type=F32)
    c2 = pw + 2 * sw
    sbw = q_ref.shape[1]
    half = sbw // 2

    def project(out_ref, col, out_col, scale=None):
        r = jnp.dot(h, win_ref[:, col:col + half], preferred_element_type=F32)
        out_ref[:, out_col:out_col + half] = (r if scale is None else r * scale).astype(BF16)

    r_i = lax.broadcasted_iota(jnp.int32, (CHUNK, CHUNK), 0)
    c_i = lax.broadcasted_iota(jnp.int32, (CHUNK, CHUNK), 1)
    tril = c_i <= r_i
    wm = jnp.concatenate([jnp.where(tril, sgw_ref[hh], 0.0).astype(BF16) for hh in range(SG_HEADS)], axis=1)
    head = lax.broadcasted_iota(jnp.int32, (1, sw), 1) // (sw // SG_HEADS)

    def gate_chunk(c):
        bz = _gelu_tanh(bpre[c * CHUNK:(c + 1) * CHUNK, :])
        vc = _rms(bz[:, sw:2 * sw], sgn_ref[...]).astype(BF16)
        vstack = jnp.concatenate([jnp.where(head == hh, vc, jnp.zeros_like(vc)) for hh in range(SG_HEADS)], axis=0)
        sv = jnp.dot(wm, vstack, preferred_element_type=F32) + sgb_ref[...]
        yab_ref[c * CHUNK:(c + 1) * CHUNK, pw:pw + sw] = (bz[:, 0:sw] * sv).astype(BF16)

    q_scale = float(SB_HD ** -0.5)
    pieces = [(q_ref, c2, 0, q_scale), (q_ref, c2 + half, half, q_scale),
              (v_ref, c2 + sbw, 0, None), (v_ref, c2 + sbw + half, half, None)]
    n_chunks = T // CHUNK
    per_piece = -(-n_chunks // len(pieces))
    for p, piece in enumerate(pieces):
        project(*piece)
        for c in range(p * per_piece, min((p + 1) * per_piece, n_chunks)):
            gate_chunk(c)

    lo, n = H, T + H
    lvl_ref[0, 2 * H:2 * H + T, :] = a
    for s, shift in enumerate((1, 2, 4, 8)):
        lvl_ref[s + 1, lo:lo + n, :] = (lvl_ref[s, lo:lo + n, :]
                                        + lvl_ref[s, lo - shift:lo - shift + n, :])
    lane = lax.broadcasted_iota(jnp.int32, (1, pw), 1)
    grp = lane // (pw // len(POOL_WINDOWS))
    win = jnp.where(grp == 0, 2, jnp.where(grp == 1, 4, jnp.where(grp == 2, 8, 16)))
    psum = jnp.where(grp == 0, lvl_ref[1, 2 * H:2 * H + T, :],
                     jnp.where(grp == 1, lvl_ref[2, 2 * H:2 * H + T, :],
                               jnp.where(grp == 2, lvl_ref[3, 2 * H:2 * H + T, :],
                                         lvl_ref[4, 2 * H:2 * H + T, :])))
    pos = tile_in_seq * T + lax.broadcasted_iota(jnp.int32, (T, 1), 0)
    cnt = jnp.minimum(pos + 1, win).astype(F32)
    d = psum / cnt - a
    ya = jnp.dot(d.astype(BF16), poolw_ref[...], preferred_element_type=F32) * pscale_ref[...]
    yab_ref[:, 0:pw] = ya.astype(BF16)

    kt = lax.dot_general(wkt_ref[...], h, (((1,), (1,)), ((), ())),
                         preferred_element_type=F32).astype(BF16)
    for hp in range(kt_ref.shape[0]):
        for c in range(kt_ref.shape[1]):
            kt_ref[hp, c] = kt[hp * LANES:(hp + 1) * LANES, c * ATT_BLOCK:(c + 1) * ATT_BLOCK]


def _inproj(x2, g1, win, wkt, poolw, pscale, sgn, sgw, sgb, *, batch, seq):
    N, D = x2.shape
    T = ROW_TILE
    pw, sw = pscale.shape[1], sgn.shape[1]
    sbw = wkt.shape[0]
    tps = seq // T
    const = lambda i: (0, 0)
    row = lambda i: (i, 0)
    return pl.pallas_call(
        functools.partial(_inproj_kernel, tiles_per_seq=tps),
        grid=(N // T,),
        in_specs=[
            pl.BlockSpec((T, D), row),
            pl.BlockSpec((1, D), const),
            pl.BlockSpec(win.shape, const),
            pl.BlockSpec(wkt.shape, const),
            pl.BlockSpec(poolw.shape, const),
            pl.BlockSpec((1, pw), const),
            pl.BlockSpec((1, sw), const),
            pl.BlockSpec(sgw.shape, lambda i: (0, 0, 0)),
            pl.BlockSpec(sgb.shape, const),
        ],
        out_specs=[
            pl.BlockSpec((T, pw + sw), row),
            pl.BlockSpec((T, sbw), row),
            pl.BlockSpec((None, sbw // LANES, T // ATT_BLOCK, LANES, ATT_BLOCK),
                         lambda i: (i // tps, 0, i % tps, 0, 0)),
            pl.BlockSpec((T, sbw), row),
        ],
        out_shape=[
            jax.ShapeDtypeStruct((N, pw + sw), BF16),
            jax.ShapeDtypeStruct((N, sbw), BF16),
            jax.ShapeDtypeStruct((batch, sbw // LANES, seq // ATT_BLOCK, LANES, ATT_BLOCK), BF16),
            jax.ShapeDtypeStruct((N, sbw), BF16),
        ],
        scratch_shapes=[pltpu.VMEM((len(POOL_WINDOWS) + 1, T + 2 * POOL_HALO, pw), F32)],
        compiler_params=pltpu.CompilerParams(
            dimension_semantics=("arbitrary",), vmem_limit_bytes=VMEM_LIMIT),
        name="inproj_mixers",
    )(x2, g1, win, wkt, poolw, pscale, sgn, sgw, sgb)


def _attn_kernel(q_ref, kt_ref, v_ref, tri_ref, o_ref, acc_ref, carry_ref, cmax_ref):
    SB = ATT_SUB
    NS = o_ref.shape[0] // SB
    i = pl.program_id(2)
    kb0 = i * NS
    head0_lane = lax.broadcasted_iota(jnp.int32, (1, LANES), 1) < SB_HD
    row = lax.broadcasted_iota(jnp.int32, (SB, LANES), 0)
    col = lax.broadcasted_iota(jnp.int32, (SB, LANES), 1) & (SB - 1)
    cap_diag = jnp.where(col < row, LOGIT_CAP, MASK_LOGIT)
    tri = tri_ref[...]

    def key_rhs(ktile, rolled, half):
        h0, h1 = (ktile, rolled) if half == 0 else (rolled, ktile)
        zero = jnp.zeros((SB_HD, LANES), ktile.dtype)
        return jnp.concatenate([jnp.where(head0_lane, h0[:SB_HD], zero),
                                jnp.where(head0_lane, zero, h1[SB_HD:])], axis=0)

    def value_rhs(kb):
        vrows = v_ref[pl.ds(pl.multiple_of(kb * SB, SB), SB), :]
        zero = jnp.zeros_like(vrows)
        return jnp.concatenate([jnp.where(head0_lane, vrows, zero),
                                jnp.where(head0_lane, zero, vrows)], axis=0)

    def logits_stage(qrows, krhs, diagonal_rows):
        z = jnp.dot(qrows, krhs, preferred_element_type=F32)
        if diagonal_rows == z.shape[0]:
            z = jnp.minimum(z, cap_diag)
        elif diagonal_rows:
            z = jnp.concatenate([jnp.minimum(z[:diagonal_rows], cap_diag),
                                 jnp.minimum(z[diagonal_rows:], LOGIT_CAP)], axis=0)
        else:
            z = jnp.minimum(z, LOGIT_CAP)
        neg_log_1m = jnp.log(1.0 + jnp.exp2(z * LOG2E))
        hi = neg_log_1m.astype(BF16)
        lo = (neg_log_1m - hi.astype(F32)).astype(BF16)
        return z, jnp.concatenate([hi, lo], axis=1)

    def weights_stage(z, hilo):
        ext = jnp.dot(hilo, tri, preferred_element_type=F32)
        return jnp.exp(z + ext[:, :LANES]).astype(BF16), ext[:, LANES:]

    def output_stage(a, vrhs):
        return jnp.dot(a, vrhs, preferred_element_type=F32)

    def tile(qrows, krhs, vrhs):
        a, rs = weights_stage(*logits_stage(qrows, krhs, 0))
        return output_stage(a, vrhs), rs

    def q_rows(first, count):
        return q_ref[first * SB:(first + count) * SB, :]

    ktiles = {}
    for t in range(-1, NS // 2):
        kt_tile = kt_ref[jnp.maximum(kb0 // 2 + t, 0)]
        ktiles[t] = (kt_tile, pltpu.roll(kt_tile, SB, 1))

    def first_stage(m):
        krhs = key_rhs(*ktiles[m // 2], m % 2)
        if m == -1:
            return logits_stage(q_rows(0, 1), krhs, 0)
        if m == NS - 1:
            return logits_stage(q_rows(m, 1), krhs, SB)
        return logits_stage(q_rows(m, 2), krhs, SB)

    blocks = list(range(-1, NS))
    stage1, stage2, parts = {}, {}, []
    for s in range(len(blocks) + 2 * ATT_SKEW):
        if s < len(blocks):
            stage1[s] = first_stage(blocks[s])
        if 0 <= s - ATT_SKEW < len(blocks):
            stage2[s - ATT_SKEW] = weights_stage(*stage1.pop(s - ATT_SKEW))
        if 0 <= s - 2 * ATT_SKEW < len(blocks):
            a, rs = stage2.pop(s - 2 * ATT_SKEW)
            vrhs = value_rhs(jnp.maximum(kb0 + blocks[s - 2 * ATT_SKEW], 0))
            parts.append((output_stage(a, vrhs), rs))

    carry_max = None
    for r in range(NS):
        pv_p, rs_p = parts[r] if r == 0 else (parts[r][0][SB:], parts[r][1][SB:])
        pv_d, rs_d = parts[r + 1] if r == NS - 1 else (parts[r + 1][0][:SB], parts[r + 1][1][:SB])
        scale = jnp.exp(rs_d)
        if r == 0:
            scale = jnp.where(kb0 > 0, scale, 0.0)
        acc_ref[r] = pv_d + scale * pv_p
        carry_ref[r] = rs_d + rs_p
        carry_max = rs_d + rs_p if carry_max is None else jnp.maximum(carry_max, rs_d + rs_p)

    def sub_block(r, _):
        qr = q_ref[pl.ds(pl.multiple_of(r * SB, SB), SB), :]

        def cond(st):
            j, cmax = st
            return jnp.logical_and(j >= 0, cmax > -STICK_CUTOFF)

        def body(st):
            j, _ = st
            kt_tile = kt_ref[lax.shift_right_logical(j, 1)]
            rolled = pltpu.roll(kt_tile, SB, 1)
            krhs = jnp.where((j & 1) == 0, key_rhs(kt_tile, rolled, 0), key_rhs(kt_tile, rolled, 1))
            pv, rs = tile(qr, krhs, value_rhs(j))
            carry = carry_ref[r]
            acc_ref[r] += jnp.exp(carry) * pv
            carry_ref[r] = carry + rs
            return j - 1, jnp.max(carry + rs)

        @pl.when(cmax_ref[r] > -STICK_CUTOFF)
        def _():
            lax.while_loop(cond, body, (kb0 + r - 2, cmax_ref[r]))

        return 0

    @pl.when(jnp.max(carry_max) > -STICK_CUTOFF)
    def _():
        for r in range(NS):
            cmax_ref[r] = jnp.max(carry_ref[r])
        lax.fori_loop(0, NS, sub_block, 0)

    for r in range(NS):
        o_ref[r * SB:(r + 1) * SB, :] = acc_ref[r].astype(o_ref.dtype)


def _stick_matrix():
    key = np.arange(LANES) % ATT_SUB
    head = np.arange(LANES) // ATT_SUB
    same = head[:, None] == head[None, :]
    suffix = np.logical_and(same, key[:, None] >= key[None, :])
    half = -np.concatenate([suffix, same], axis=1).astype(np.float32)
    return jnp.asarray(np.concatenate([half, half], axis=0), BF16)


def _attention(q, kt, v, *, batch, seq):
    N, W = q.shape
    R = ATT_ROWS
    q3, v3 = q.reshape(batch, seq, W), v.reshape(batch, seq, W)
    tri = _stick_matrix()
    resident = lambda b, hp, i: (b, 0, hp)
    out = pl.pallas_call(
        _attn_kernel,
        grid=(batch, W // LANES, seq // R),
        in_specs=[
            pl.BlockSpec((None, R, LANES), lambda b, hp, i: (b, i, hp)),
            pl.BlockSpec((None, None, seq // ATT_BLOCK, LANES, ATT_BLOCK), lambda b, hp, i: (b, hp, 0, 0, 0)),
            pl.BlockSpec((None, seq, LANES), resident),
            pl.BlockSpec(tri.shape, lambda b, hp, i: (0, 0)),
        ],
        out_specs=pl.BlockSpec((None, R, LANES), lambda b, hp, i: (b, i, hp)),
        out_shape=jax.ShapeDtypeStruct((batch, seq, W), BF16),
        scratch_shapes=[pltpu.VMEM((R // ATT_SUB, ATT_SUB, LANES), F32),
                        pltpu.VMEM((R // ATT_SUB, ATT_SUB, LANES), F32),
                        pltpu.SMEM((R // ATT_SUB,), F32)],
        compiler_params=pltpu.CompilerParams(
            dimension_semantics=("arbitrary", "arbitrary", "arbitrary"),
            vmem_limit_bytes=VMEM_LIMIT),
        name="stickbreak_attn",
    )(q3, kt, v3, tri)
    return out.reshape(N, W)


def _mlp_kernel(x_ref, yab_ref, yc_ref, g2_ref, gf_ref, wo_hbm, wup_hbm, wdn_hbm, o_ref,
                wo_ref, wup_ref, wdn_ref, sem, *, final_norm, ff_chunk):
    @pl.when(pl.program_id(0) == 0)
    def _():
        copies = [pltpu.make_async_copy(src, dst, sem.at[n])
                  for n, (src, dst) in enumerate(((wo_hbm, wo_ref), (wup_hbm, wup_ref), (wdn_hbm, wdn_ref)))]
        for c in copies:
            c.start()
        for c in copies:
            c.wait()

    nab = yab_ref.shape[1]
    x1 = (x_ref[...]
          + jnp.dot(yab_ref[...], wo_ref[0:nab, :], preferred_element_type=F32)
          + jnp.dot(yc_ref[...], wo_ref[nab:, :], preferred_element_type=F32))
    h = _rms(x1, g2_ref[...]).astype(BF16)
    acc = None
    for c in range(wup_ref.shape[1] // ff_chunk):
        up = jnp.dot(h, wup_ref[:, c * ff_chunk:(c + 1) * ff_chunk], preferred_element_type=F32)
        act = jnp.square(jnp.maximum(up, 0.0)).astype(BF16)
        dn = jnp.dot(act, wdn_ref[c * ff_chunk:(c + 1) * ff_chunk, :], preferred_element_type=F32)
        acc = dn if acc is None else acc + dn
    acc = acc + x1
    if final_norm:
        acc = _rms(acc, gf_ref[...])
    o_ref[...] = acc


def _mlp(x2, yab, yc, wo, g2, wup, wdn, gf, *, final_norm):
    N, D = x2.shape
    T = ROW_TILE
    const = lambda i: (0, 0)
    row = lambda i: (i, 0)
    hbm = pl.BlockSpec(memory_space=pl.ANY)
    return pl.pallas_call(
        functools.partial(_mlp_kernel, final_norm=final_norm, ff_chunk=1024),
        grid=(N // T,),
        in_specs=[
            pl.BlockSpec((T, D), row),
            pl.BlockSpec((T, yab.shape[1]), row),
            pl.BlockSpec((T, yc.shape[1]), row),
            pl.BlockSpec((1, D), const),
            pl.BlockSpec((1, D), const),
            hbm, hbm, hbm,
        ],
        out_specs=pl.BlockSpec((T, D), row),
        out_shape=jax.ShapeDtypeStruct((N, D), F32),
        scratch_shapes=[pltpu.VMEM(wo.shape, BF16), pltpu.VMEM(wup.shape, BF16),
                        pltpu.VMEM(wdn.shape, BF16), pltpu.SemaphoreType.DMA((3,))],
        compiler_params=pltpu.CompilerParams(
            dimension_semantics=("arbitrary",), vmem_limit_bytes=VMEM_LIMIT),
        name="outproj_mlp",
    )(x2, yab, yc, g2, gf, wo, wup, wdn)


def kernel(x, norm1, w_in, pool_w, pool_scale, sg_norm, sg_w, sg_b, w_out, norm2, w_up, w_down, final_norm):
    B, S, D = x.shape
    depth = norm1.shape[0]
    n_grp, gw = pool_w.shape[1], pool_w.shape[2]
    pw, sw = pool_scale.shape[1], sg_norm.shape[1]
    sbw = (w_in.shape[2] - pw - 2 * sw) // 3
    assert S % ROW_TILE == 0 and S % ATT_ROWS == 0 and ROW_TILE % CHUNK == 0 and ROW_TILE % ATT_BLOCK == 0
    assert sg_w.shape[1] == SG_HEADS and sg_w.shape[2] == CHUNK and n_grp == len(POOL_WINDOWS)
    assert sbw % LANES == 0 and ATT_BLOCK == LANES == 2 * ATT_SUB == 2 * SB_HD and (ATT_ROWS // ATT_SUB) % 2 == 0

    x2 = x.reshape(B * S, D)
    ck = pw + 2 * sw + sbw
    for l in range(depth):
        win = jnp.concatenate([w_in[l][:, :ck], w_in[l][:, ck + sbw:]], axis=1).astype(BF16)
        wkt = w_in[l][:, ck:ck + sbw].T.astype(BF16)
        poolw = jax.scipy.linalg.block_diag(*[pool_w[l, g] for g in range(n_grp)]).astype(BF16)
        sgb = jnp.repeat(sg_b[l].T, sw // SG_HEADS, axis=1)
        yab, q, kt, v = _inproj(x2, norm1[l][None], win, wkt, poolw, pool_scale[l][None],
                                sg_norm[l][None], sg_w[l], sgb, batch=B, seq=S)
        yc = _attention(q, kt, v, batch=B, seq=S)
        x2 = _mlp(x2, yab, yc, w_out[l].astype(BF16), norm2[l][None],
                  w_up[l].astype(BF16), w_down[l].astype(BF16), final_norm[None],
                  final_norm=(l == depth - 1))
    return x2.reshape(B, S, D)
```

```python
import functools

import jax
import jax.numpy as jnp
import numpy as np
from jax import lax
from jax.experimental import pallas as pl
from jax.experimental.pallas import tpu as pltpu

EPS = 1e-6
POOL_WINDOWS = (2, 4, 8, 16)
POOL_HALO = 16
CHUNK = 128
SG_HEADS = 4
SB_HD = 64
LANES = 128

ROW_TILE = 1024
ATT_BLOCK = 128
ATT_SUB = 64
ATT_ROWS = 8192
ATT_SKEW = (3, 4)
STICK_CUTOFF = 30.0
MASK_LOGIT = -1e30
LOGIT_CAP = 80.0
LOG2E = 1.4426950408889634
VMEM_LIMIT = 56 * 1024 * 1024

F32 = jnp.float32
BF16 = jnp.bfloat16


def _rms(x, g):
    ms = jnp.mean(x * x, axis=-1, keepdims=True)
    return x * lax.rsqrt(ms + EPS) * g


def _gelu_tanh(x):
    c = np.float32(np.sqrt(2.0 / np.pi))
    return 0.5 * x * (1.0 + jnp.tanh(c * (x + 0.044715 * (x * x * x))))


def _inproj_kernel(x_ref, g1_ref, win_ref, wkt_ref, poolw_ref, pscale_ref, sgn_ref, sgw_ref, sgb_ref,
                   yab_ref, q_ref, kt_ref, v_ref, lvl_ref, *, tiles_per_seq):
    T = x_ref.shape[0]
    pw = pscale_ref.shape[1]
    sw = sgn_ref.shape[1]
    i = pl.program_id(0)
    tile_in_seq = i % tiles_per_seq

    H = POOL_HALO

    @pl.when(i == 0)
    def _():
        lvl_ref[:, 0:H, :] = jnp.zeros((lvl_ref.shape[0], H, pw), F32)

    @pl.when(tile_in_seq == 0)
    def _():
        lvl_ref[0, H:2 * H, :] = jnp.zeros((H, pw), F32)

    @pl.when(tile_in_seq != 0)
    def _():
        lvl_ref[0, H:2 * H, :] = lvl_ref[0, T + H:T + 2 * H, :]

    h = _rms(x_ref[...], g1_ref[...]).astype(BF16)
    bpre = jnp.dot(h, win_ref[:, pw:pw + 2 * sw], preferred_element_type=F32)
    a = jnp.dot(h, win_ref[:, 0:pw], preferred_element_type=F32)
    c2 = pw + 2 * sw
    sbw = q_ref.shape[1]
    half = sbw // 2

    def project(out_ref, col, out_col, scale=None):
        r = jnp.dot(h, win_ref[:, col:col + half], preferred_element_type=F32)
        out_ref[:, out_col:out_col + half] = (r if scale is None else r * scale).astype(BF16)

    r_i = lax.broadcasted_iota(jnp.int32, (CHUNK, CHUNK), 0)
    c_i = lax.broadcasted_iota(jnp.int32, (CHUNK, CHUNK), 1)
    tril = c_i <= r_i
    wm = jnp.concatenate([jnp.where(tril, sgw_ref[hh], 0.0).astype(BF16) for hh in range(SG_HEADS)], axis=1)
    head = lax.broadcasted_iota(jnp.int32, (1, sw), 1) // (sw // SG_HEADS)

    def gate_chunk(c):
        bz = _gelu_tanh(bpre[c * CHUNK:(c + 1) * CHUNK, :])
        vc = _rms(bz[:, sw:2 * sw], sgn_ref[...]).astype(BF16)
        vstack = jnp.concatenate([jnp.where(head == hh, vc, jnp.zeros_like(vc)) for hh in range(SG_HEADS)], axis=0)
        sv = jnp.dot(wm, vstack, preferred_element_type=F32) + sgb_ref[...]
        yab_ref[c * CHUNK:(c + 1) * CHUNK, pw:pw + sw] = (bz[:, 0:sw] * sv).astype(BF16)

    q_scale = float(SB_HD ** -0.5)
    pieces = [(q_ref, c2, 0, q_scale), (q_ref, c2 + half, half, q_scale),
              (v_ref, c2 + sbw, 0, None), (v_ref, c2 + sbw + half, half, None)]
    n_chunks = T // CHUNK
    per_piece = -(-n_chunks // len(pieces))
    for p, piece in enumerate(pieces):
        project(*piece)
        for c in range(p * per_piece, min((p + 1) * per_piece, n_chunks)):
            gate_chunk(c)

    lo, n = H, T + H
    lvl_ref[0, 2 * H:2 * H + T, :] = a
    for s, shift in enumerate((1, 2, 4, 8)):
        lvl_ref[s + 1, lo:lo + n, :] = (lvl_ref[s, lo:lo + n, :]
                                        + lvl_ref[s, lo - shift:lo - shift + n, :])
    lane = lax.broadcasted_iota(jnp.int32, (1, pw), 1)
    grp = lane // (pw // len(POOL_WINDOWS))
    win = jnp.where(grp == 0, 2, jnp.where(grp == 1, 4, jnp.where(grp == 2, 8, 16)))
    psum = jnp.where(grp == 0, lvl_ref[1, 2 * H:2 * H + T, :],
                     jnp.where(grp == 1, lvl_ref[2, 2 * H:2 * H + T, :],
                               jnp.where(grp == 2, lvl_ref[3, 2 * H:2 * H + T, :],
                                         lvl_ref[4, 2 * H:2 * H + T, :])))
    pos = tile_in_seq * T + lax.broadcasted_iota(jnp.int32, (T, 1), 0)
    cnt = jnp.minimum(pos + 1, win).astype(F32)
    d = psum / cnt - a
    ya = jnp.dot(d.astype(BF16), poolw_ref[...], preferred_element_type=F32) * pscale_ref[...]
    yab_ref[:, 0:pw] = ya.astype(BF16)

    kt = lax.dot_general(wkt_ref[...], h, (((1,), (1,)), ((), ())),
                         preferred_element_type=F32).astype(BF16)
    for hp in range(kt_ref.shape[0]):
        for c in range(kt_ref.shape[1]):
            kt_ref[hp, c] = kt[hp * LANES:(hp + 1) * LANES, c * ATT_BLOCK:(c + 1) * ATT_BLOCK]


def _inproj(x2, g1, win, wkt, poolw, pscale, sgn, sgw, sgb, *, batch, seq):
    N, D = x2.shape
    T = ROW_TILE
    pw, sw = pscale.shape[1], sgn.shape[1]
    sbw = wkt.shape[0]
    tps = seq // T
    const = lambda i: (0, 0)
    row = lambda i: (i, 0)
    return pl.pallas_call(
        functools.partial(_inproj_kernel, tiles_per_seq=tps),
        grid=(N // T,),
        in_specs=[
            pl.BlockSpec((T, D), row),
            pl.BlockSpec((1, D), const),
            pl.BlockSpec(win.shape, const),
            pl.BlockSpec(wkt.shape, const),
            pl.BlockSpec(poolw.shape, const),
            pl.BlockSpec((1, pw), const),
            pl.BlockSpec((1, sw), const),
            pl.BlockSpec(sgw.shape, lambda i: (0, 0, 0)),
            pl.BlockSpec(sgb.shape, const),
        ],
        out_specs=[
            pl.BlockSpec((T, pw + sw), row),
            pl.BlockSpec((T, sbw), row),
            pl.BlockSpec((None, sbw // LANES, T // ATT_BLOCK, LANES, ATT_BLOCK),
                         lambda i: (i // tps, 0, i % tps, 0, 0)),
            pl.BlockSpec((T, sbw), row),
        ],
        out_shape=[
            jax.ShapeDtypeStruct((N, pw + sw), BF16),
            jax.ShapeDtypeStruct((N, sbw), BF16),
            jax.ShapeDtypeStruct((batch, sbw // LANES, seq // ATT_BLOCK, LANES, ATT_BLOCK), BF16),
            jax.ShapeDtypeStruct((N, sbw), BF16),
        ],
        scratch_shapes=[pltpu.VMEM((len(POOL_WINDOWS) + 1, T + 2 * POOL_HALO, pw), F32)],
        compiler_params=pltpu.CompilerParams(
            dimension_semantics=("arbitrary",), vmem_limit_bytes=VMEM_LIMIT),
        name="inproj_mixers",
    )(x2, g1, win, wkt, poolw, pscale, sgn, sgw, sgb)


def _attn_kernel(q_ref, kt_ref, v_ref, tri_ref, o_ref, acc_ref, carry_ref, cmax_ref):
    SB = ATT_SUB
    NS = o_ref.shape[0] // SB
    i = pl.program_id(2)
    kb0 = i * NS
    head0_lane = lax.broadcasted_iota(jnp.int32, (1, LANES), 1) < SB_HD
    row = lax.broadcasted_iota(jnp.int32, (SB, LANES), 0)
    col = lax.broadcasted_iota(jnp.int32, (SB, LANES), 1) & (SB - 1)
    cap_diag = jnp.where(col < row, LOGIT_CAP, MASK_LOGIT)
    tri = tri_ref[...]

    def key_rhs(ktile, rolled, half):
        h0, h1 = (ktile, rolled) if half == 0 else (rolled, ktile)
        zero = jnp.zeros((SB_HD, LANES), ktile.dtype)
        return jnp.concatenate([jnp.where(head0_lane, h0[:SB_HD], zero),
                                jnp.where(head0_lane, zero, h1[SB_HD:])], axis=0)

    def value_rhs(kb):
        vrows = v_ref[pl.ds(pl.multiple_of(kb * SB, SB), SB), :]
        zero = jnp.zeros_like(vrows)
        return jnp.concatenate([jnp.where(head0_lane, vrows, zero),
                                jnp.where(head0_lane, zero, vrows)], axis=0)

    def logits_stage(qrows, krhs, diagonal_rows):
        z = jnp.dot(qrows, krhs, preferred_element_type=F32)
        if diagonal_rows == z.shape[0]:
            z = jnp.minimum(z, cap_diag)
        elif diagonal_rows:
            z = jnp.concatenate([jnp.minimum(z[:diagonal_rows], cap_diag),
                                 jnp.minimum(z[diagonal_rows:], LOGIT_CAP)], axis=0)
        else:
            z = jnp.minimum(z, LOGIT_CAP)
        neg_log_1m = jnp.log(1.0 + jnp.exp2(z * LOG2E))
        hi = neg_log_1m.astype(BF16)
        lo = (neg_log_1m - hi.astype(F32)).astype(BF16)
        return z, jnp.concatenate([hi, lo], axis=1)

    def weights_stage(z, hilo):
        ext = jnp.dot(hilo, tri, preferred_element_type=F32)
        return jnp.exp(z + ext[:, :LANES]).astype(BF16), ext[:, LANES:]

    def output_stage(a, vrhs):
        return jnp.dot(a, vrhs, preferred_element_type=F32)

    def tile(qrows, krhs, vrhs):
        a, rs = weights_stage(*logits_stage(qrows, krhs, 0))
        return output_stage(a, vrhs), rs

    def q_rows(first, count):
        return q_ref[first * SB:(first + count) * SB, :]

    ktiles = {}
    for t in range(-1, NS // 2):
        kt_tile = kt_ref[jnp.maximum(kb0 // 2 + t, 0)]
        ktiles[t] = (kt_tile, pltpu.roll(kt_tile, SB, 1))

    def first_stage(m):
        krhs = key_rhs(*ktiles[m // 2], m % 2)
        if m == -1:
            return logits_stage(q_rows(0, 1), krhs, 0)
        if m == NS - 1:
            return logits_stage(q_rows(m, 1), krhs, SB)
        return logits_stage(q_rows(m, 2), krhs, SB)

    blocks = list(range(-1, NS))
    stage1, stage2, parts = {}, {}, []
    lag2, lag3 = ATT_SKEW[0], ATT_SKEW[0] + ATT_SKEW[1]
    for s in range(len(blocks) + lag3):
        if s < len(blocks):
            stage1[s] = first_stage(blocks[s])
        if 0 <= s - lag2 < len(blocks):
            stage2[s - lag2] = weights_stage(*stage1.pop(s - lag2))
        if 0 <= s - lag3 < len(blocks):
            a, rs = stage2.pop(s - lag3)
            vrhs = value_rhs(jnp.maximum(kb0 + blocks[s - lag3], 0))
            parts.append((output_stage(a, vrhs), rs))

    carry_max = None
    for r in range(NS):
        pv_p, rs_p = parts[r] if r == 0 else (parts[r][0][SB:], parts[r][1][SB:])
        pv_d, rs_d = parts[r + 1] if r == NS - 1 else (parts[r + 1][0][:SB], parts[r + 1][1][:SB])
        scale = jnp.exp(rs_d)
        if r == 0:
            scale = jnp.where(kb0 > 0, scale, 0.0)
        acc_ref[r] = pv_d + scale * pv_p
        carry_ref[r] = rs_d + rs_p
        carry_max = rs_d + rs_p if carry_max is None else jnp.maximum(carry_max, rs_d + rs_p)

    def sub_block(r, _):
        qr = q_ref[pl.ds(pl.multiple_of(r * SB, SB), SB), :]

        def cond(st):
            j, cmax = st
            return jnp.logical_and(j >= 0, cmax > -STICK_CUTOFF)

        def body(st):
            j, _ = st
            kt_tile = kt_ref[lax.shift_right_logical(j, 1)]
            rolled = pltpu.roll(kt_tile, SB, 1)
            krhs = jnp.where((j & 1) == 0, key_rhs(kt_tile, rolled, 0), key_rhs(kt_tile, rolled, 1))
            pv, rs = tile(qr, krhs, value_rhs(j))
            carry = carry_ref[r]
            acc_ref[r] += jnp.exp(carry) * pv
            carry_ref[r] = carry + rs
            return j - 1, jnp.max(carry + rs)

        @pl.when(cmax_ref[r] > -STICK_CUTOFF)
        def _():
            lax.while_loop(cond, body, (kb0 + r - 2, cmax_ref[r]))

        return 0

    @pl.when(jnp.max(carry_max) > -STICK_CUTOFF)
    def _():
        for r in range(NS):
            cmax_ref[r] = jnp.max(carry_ref[r])
        lax.fori_loop(0, NS, sub_block, 0)

    for r in range(NS):
        o_ref[r * SB:(r + 1) * SB, :] = acc_ref[r].astype(o_ref.dtype)


def _stick_matrix():
    key = np.arange(LANES) % ATT_SUB
    head = np.arange(LANES) // ATT_SUB
    same = head[:, None] == head[None, :]
    suffix = np.logical_and(same, key[:, None] >= key[None, :])
    half = -np.concatenate([suffix, same], axis=1).astype(np.float32)
    return jnp.asarray(np.concatenate([half, half], axis=0), BF16)


def _attention(q, kt, v, *, batch, seq):
    N, W = q.shape
    R = ATT_ROWS
    q3, v3 = q.reshape(batch, seq, W), v.reshape(batch, seq, W)
    tri = _stick_matrix()
    resident = lambda b, hp, i: (b, 0, hp)
    out = pl.pallas_call(
        _attn_kernel,
        grid=(batch, W // LANES, seq // R),
        in_specs=[
            pl.BlockSpec((None, R, LANES), lambda b, hp, i: (b, i, hp)),
            pl.BlockSpec((None, None, seq // ATT_BLOCK, LANES, ATT_BLOCK), lambda b, hp, i: (b, hp, 0, 0, 0)),
            pl.BlockSpec((None, seq, LANES), resident),
            pl.BlockSpec(tri.shape, lambda b, hp, i: (0, 0)),
        ],
        out_specs=pl.BlockSpec((None, R, LANES), lambda b, hp, i: (b, i, hp)),
        out_shape=jax.ShapeDtypeStruct((batch, seq, W), BF16),
        scratch_shapes=[pltpu.VMEM((R // ATT_SUB, ATT_SUB, LANES), F32),
                        pltpu.VMEM((R // ATT_SUB, ATT_SUB, LANES), F32),
                        pltpu.SMEM((R // ATT_SUB,), F32)],
        compiler_params=pltpu.CompilerParams(
            dimension_semantics=("arbitrary", "arbitrary", "arbitrary"),
            vmem_limit_bytes=VMEM_LIMIT),
        name="stickbreak_attn",
    )(q3, kt, v3, tri)
    return out.reshape(N, W)


def _mlp_kernel(x_ref, yab_ref, yc_ref, g2_ref, gf_ref, wo_hbm, wup_hbm, wdn_hbm, o_ref,
                wo_ref, wup_ref, wdn_ref, sem, *, final_norm, ff_chunk):
    @pl.when(pl.program_id(0) == 0)
    def _():
        copies = [pltpu.make_async_copy(src, dst, sem.at[n])
                  for n, (src, dst) in enumerate(((wo_hbm, wo_ref), (wup_hbm, wup_ref), (wdn_hbm, wdn_ref)))]
        for c in copies:
            c.start()
        for c in copies:
            c.wait()

    nab = yab_ref.shape[1]
    x1 = (x_ref[...]
          + jnp.dot(yab_ref[...], wo_ref[0:nab, :], preferred_element_type=F32)
          + jnp.dot(yc_ref[...], wo_ref[nab:, :], preferred_element_type=F32))
    h = _rms(x1, g2_ref[...]).astype(BF16)
    acc = None
    for c in range(wup_ref.shape[1] // ff_chunk):
        up = jnp.dot(h, wup_ref[:, c * ff_chunk:(c + 1) * ff_chunk], preferred_element_type=F32)
        act = jnp.square(jnp.maximum(up, 0.0)).astype(BF16)
        dn = jnp.dot(act, wdn_ref[c * ff_chunk:(c + 1) * ff_chunk, :], preferred_element_type=F32)
        acc = dn if acc is None else acc + dn
    acc = acc + x1
    if final_norm:
        acc = _rms(acc, gf_ref[...])
    o_ref[...] = acc


def _mlp(x2, yab, yc, wo, g2, wup, wdn, gf, *, final_norm):
    N, D = x2.shape
    T = ROW_TILE
    const = lambda i: (0, 0)
    row = lambda i: (i, 0)
    hbm = pl.BlockSpec(memory_space=pl.ANY)
    return pl.pallas_call(
        functools.partial(_mlp_kernel, final_norm=final_norm, ff_chunk=1024),
        grid=(N // T,),
        in_specs=[
            pl.BlockSpec((T, D), row),
            pl.BlockSpec((T, yab.shape[1]), row),
            pl.BlockSpec((T, yc.shape[1]), row),
            pl.BlockSpec((1, D), const),
            pl.BlockSpec((1, D), const),
            hbm, hbm, hbm,
        ],
        out_specs=pl.BlockSpec((T, D), row),
        out_shape=jax.ShapeDtypeStruct((N, D), F32),
        scratch_shapes=[pltpu.VMEM(wo.shape, BF16), pltpu.VMEM(wup.shape, BF16),
                        pltpu.VMEM(wdn.shape, BF16), pltpu.SemaphoreType.DMA((3,))],
        compiler_params=pltpu.CompilerParams(
            dimension_semantics=("arbitrary",), vmem_limit_bytes=VMEM_LIMIT),
        name="outproj_mlp",
    )(x2, yab, yc, g2, gf, wo, wup, wdn)


def kernel(x, norm1, w_in, pool_w, pool_scale, sg_norm, sg_w, sg_b, w_out, norm2, w_up, w_down, final_norm):
    B, S, D = x.shape
    depth = norm1.shape[0]
    n_grp, gw = pool_w.shape[1], pool_w.shape[2]
    pw, sw = pool_scale.shape[1], sg_norm.shape[1]
    sbw = (w_in.shape[2] - pw - 2 * sw) // 3
    assert S % ROW_TILE == 0 and S % ATT_ROWS == 0 and ROW_TILE % CHUNK == 0 and ROW_TILE % ATT_BLOCK == 0
    assert sg_w.shape[1] == SG_HEADS and sg_w.shape[2] == CHUNK and n_grp == len(POOL_WINDOWS)
    assert sbw % LANES == 0 and ATT_BLOCK == LANES == 2 * ATT_SUB == 2 * SB_HD and (ATT_ROWS // ATT_SUB) % 2 == 0

    x2 = x.reshape(B * S, D)
    ck = pw + 2 * sw + sbw
    for l in range(depth):
        win = jnp.concatenate([w_in[l][:, :ck], w_in[l][:, ck + sbw:]], axis=1).astype(BF16)
        wkt = w_in[l][:, ck:ck + sbw].T.astype(BF16)
        poolw = jax.scipy.linalg.block_diag(*[pool_w[l, g] for g in range(n_grp)]).astype(BF16)
        sgb = jnp.repeat(sg_b[l].T, sw // SG_HEADS, axis=1)
        yab, q, kt, v = _inproj(x2, norm1[l][None], win, wkt, poolw, pool_scale[l][None],
                                sg_norm[l][None], sg_w[l], sgb, batch=B, seq=S)
        yc = _attention(q, kt, v, batch=B, seq=S)
        x2 = _mlp(x2, yab, yc, w_out[l].astype(BF16), norm2[l][None],
                  w_up[l].astype(BF16), w_down[l].astype(BF16), final_norm[None],
                  final_norm=(l == depth - 1))
    return x2.reshape(B, S, D)
```

```python
import functools

import jax
import jax.numpy as jnp
import numpy as np
from jax import lax
from jax.experimental import pallas as pl
from jax.experimental.pallas import tpu as pltpu

EPS = 1e-6
POOL_WINDOWS = (2, 4, 8, 16)
POOL_HALO = 16
CHUNK = 128
SG_HEADS = 4
SB_HD = 64
LANES = 128

ROW_TILE = 1024
MLP_ROW_GROUPS = 2
ATT_BLOCK = 128
ATT_SUB = 64
ATT_ROWS = 8192
ATT_SKEW = (3, 4)
STICK_CUTOFF = 30.0
MASK_LOGIT = -1e30
LOGIT_CAP = 80.0
LOG2E = 1.4426950408889634
VMEM_LIMIT = 56 * 1024 * 1024

F32 = jnp.float32
BF16 = jnp.bfloat16


def _rms(x, g):
    ms = jnp.mean(x * x, axis=-1, keepdims=True)
    return x * lax.rsqrt(ms + EPS) * g


def _gelu_tanh(x):
    c = np.float32(np.sqrt(2.0 / np.pi))
    return 0.5 * x * (1.0 + jnp.tanh(c * (x + 0.044715 * (x * x * x))))


def _inproj_kernel(x_ref, g1_ref, win_ref, wkt_ref, poolw_ref, pscale_ref, sgn_ref, sgw_ref, sgb_ref,
                   yab_ref, q_ref, kt_ref, v_ref, lvl_ref, *, tiles_per_seq):
    T = x_ref.shape[0]
    pw = pscale_ref.shape[1]
    sw = sgn_ref.shape[1]
    i = pl.program_id(0)
    tile_in_seq = i % tiles_per_seq

    H = POOL_HALO

    @pl.when(i == 0)
    def _():
        lvl_ref[:, 0:H, :] = jnp.zeros((lvl_ref.shape[0], H, pw), F32)

    @pl.when(tile_in_seq == 0)
    def _():
        lvl_ref[0, H:2 * H, :] = jnp.zeros((H, pw), F32)

    @pl.when(tile_in_seq != 0)
    def _():
        lvl_ref[0, H:2 * H, :] = lvl_ref[0, T + H:T + 2 * H, :]

    hs, bpres, a_s = [], [], []
    for r0 in range(0, T, T // 2):
        hg = _rms(x_ref[r0:r0 + T // 2, :], g1_ref[...]).astype(BF16)
        bpres.append(jnp.dot(hg, win_ref[:, pw:pw + 2 * sw], preferred_element_type=F32))
        a_s.append(jnp.dot(hg, win_ref[:, 0:pw], preferred_element_type=F32))
        hs.append(hg)
    h = jnp.concatenate(hs, axis=0)
    bpre = jnp.concatenate(bpres, axis=0)
    a = jnp.concatenate(a_s, axis=0)
    c2 = pw + 2 * sw
    sbw = q_ref.shape[1]
    half = sbw // 2

    def project(out_ref, col, out_col, scale=None):
        r = jnp.dot(h, win_ref[:, col:col + half], preferred_element_type=F32)
        out_ref[:, out_col:out_col + half] = (r if scale is None else r * scale).astype(BF16)

    r_i = lax.broadcasted_iota(jnp.int32, (CHUNK, CHUNK), 0)
    c_i = lax.broadcasted_iota(jnp.int32, (CHUNK, CHUNK), 1)
    tril = c_i <= r_i
    wm = jnp.concatenate([jnp.where(tril, sgw_ref[hh], 0.0).astype(BF16) for hh in range(SG_HEADS)], axis=1)
    head = lax.broadcasted_iota(jnp.int32, (1, sw), 1) // (sw // SG_HEADS)

    def gate_chunk(c):
        bz = _gelu_tanh(bpre[c * CHUNK:(c + 1) * CHUNK, :])
        vc = _rms(bz[:, sw:2 * sw], sgn_ref[...]).astype(BF16)
        vstack = jnp.concatenate([jnp.where(head == hh, vc, jnp.zeros_like(vc)) for hh in range(SG_HEADS)], axis=0)
        sv = jnp.dot(wm, vstack, preferred_element_type=F32) + sgb_ref[...]
        yab_ref[c * CHUNK:(c + 1) * CHUNK, pw:pw + sw] = (bz[:, 0:sw] * sv).astype(BF16)

    q_scale = float(SB_HD ** -0.5)
    pieces = [(q_ref, c2, 0, q_scale), (q_ref, c2 + half, half, q_scale),
              (v_ref, c2 + sbw, 0, None), (v_ref, c2 + sbw + half, half, None)]
    n_chunks = T // CHUNK
    per_piece = -(-n_chunks // len(pieces))
    for p, piece in enumerate(pieces):
        project(*piece)
        for c in range(p * per_piece, min((p + 1) * per_piece, n_chunks)):
            gate_chunk(c)

    lo, n = H, T + H
    lvl_ref[0, 2 * H:2 * H + T, :] = a
    for s, shift in enumerate((1, 2, 4, 8)):
        lvl_ref[s + 1, lo:lo + n, :] = (lvl_ref[s, lo:lo + n, :]
                                        + lvl_ref[s, lo - shift:lo - shift + n, :])
    lane = lax.broadcasted_iota(jnp.int32, (1, pw), 1)
    grp = lane // (pw // len(POOL_WINDOWS))
    win = jnp.where(grp == 0, 2, jnp.where(grp == 1, 4, jnp.where(grp == 2, 8, 16)))
    psum = jnp.where(grp == 0, lvl_ref[1, 2 * H:2 * H + T, :],
                     jnp.where(grp == 1, lvl_ref[2, 2 * H:2 * H + T, :],
                               jnp.where(grp == 2, lvl_ref[3, 2 * H:2 * H + T, :],
                                         lvl_ref[4, 2 * H:2 * H + T, :])))
    pos = tile_in_seq * T + lax.broadcasted_iota(jnp.int32, (T, 1), 0)
    cnt = jnp.minimum(pos + 1, win).astype(F32)
    d = psum / cnt - a
    ya = jnp.dot(d.astype(BF16), poolw_ref[...], preferred_element_type=F32) * pscale_ref[...]
    yab_ref[:, 0:pw] = ya.astype(BF16)

    kt = lax.dot_general(wkt_ref[...], h, (((1,), (1,)), ((), ())),
                         preferred_element_type=F32).astype(BF16)
    for hp in range(kt_ref.shape[0]):
        for c in range(kt_ref.shape[1]):
            kt_ref[hp, c] = kt[hp * LANES:(hp + 1) * LANES, c * ATT_BLOCK:(c + 1) * ATT_BLOCK]


def _inproj(x2, g1, win, wkt, poolw, pscale, sgn, sgw, sgb, *, batch, seq):
    N, D = x2.shape
    T = ROW_TILE
    pw, sw = pscale.shape[1], sgn.shape[1]
    sbw = wkt.shape[0]
    tps = seq // T
    const = lambda i: (0, 0)
    row = lambda i: (i, 0)
    return pl.pallas_call(
        functools.partial(_inproj_kernel, tiles_per_seq=tps),
        grid=(N // T,),
        in_specs=[
            pl.BlockSpec((T, D), row),
            pl.BlockSpec((1, D), const),
            pl.BlockSpec(win.shape, const),
            pl.BlockSpec(wkt.shape, const),
            pl.BlockSpec(poolw.shape, const),
            pl.BlockSpec((1, pw), const),
            pl.BlockSpec((1, sw), const),
            pl.BlockSpec(sgw.shape, lambda i: (0, 0, 0)),
            pl.BlockSpec(sgb.shape, const),
        ],
        out_specs=[
            pl.BlockSpec((T, pw + sw), row),
            pl.BlockSpec((T, sbw), row),
            pl.BlockSpec((None, sbw // LANES, T // ATT_BLOCK, LANES, ATT_BLOCK),
                         lambda i: (i // tps, 0, i % tps, 0, 0)),
            pl.BlockSpec((T, sbw), row),
        ],
        out_shape=[
            jax.ShapeDtypeStruct((N, pw + sw), BF16),
            jax.ShapeDtypeStruct((N, sbw), BF16),
            jax.ShapeDtypeStruct((batch, sbw // LANES, seq // ATT_BLOCK, LANES, ATT_BLOCK), BF16),
            jax.ShapeDtypeStruct((N, sbw), BF16),
        ],
        scratch_shapes=[pltpu.VMEM((len(POOL_WINDOWS) + 1, T + 2 * POOL_HALO, pw), F32)],
        compiler_params=pltpu.CompilerParams(
            dimension_semantics=("arbitrary",), vmem_limit_bytes=VMEM_LIMIT),
        name="inproj_mixers",
    )(x2, g1, win, wkt, poolw, pscale, sgn, sgw, sgb)


def _attn_kernel(q_ref, kt_ref, v_ref, tri_ref, o_ref, acc_ref, carry_ref, cmax_ref):
    SB = ATT_SUB
    NS = o_ref.shape[0] // SB
    i = pl.program_id(2)
    kb0 = i * NS
    head0_lane = lax.broadcasted_iota(jnp.int32, (1, LANES), 1) < SB_HD
    row = lax.broadcasted_iota(jnp.int32, (SB, LANES), 0)
    col = lax.broadcasted_iota(jnp.int32, (SB, LANES), 1) & (SB - 1)
    cap_diag = jnp.where(col < row, LOGIT_CAP, MASK_LOGIT)
    tri = tri_ref[...]

    def key_rhs(ktile, rolled, half):
        h0, h1 = (ktile, rolled) if half == 0 else (rolled, ktile)
        zero = jnp.zeros((SB_HD, LANES), ktile.dtype)
        return jnp.concatenate([jnp.where(head0_lane, h0[:SB_HD], zero),
                                jnp.where(head0_lane, zero, h1[SB_HD:])], axis=0)

    def value_rhs(kb):
        vrows = v_ref[pl.ds(pl.multiple_of(kb * SB, SB), SB), :]
        zero = jnp.zeros_like(vrows)
        return jnp.concatenate([jnp.where(head0_lane, vrows, zero),
                                jnp.where(head0_lane, zero, vrows)], axis=0)

    def logits_stage(qrows, krhs, diagonal_rows):
        z = jnp.dot(qrows, krhs, preferred_element_type=F32)
        if diagonal_rows == z.shape[0]:
            z = jnp.minimum(z, cap_diag)
        elif diagonal_rows:
            z = jnp.concatenate([jnp.minimum(z[:diagonal_rows], cap_diag),
                                 jnp.minimum(z[diagonal_rows:], LOGIT_CAP)], axis=0)
        else:
            z = jnp.minimum(z, LOGIT_CAP)
        neg_log_1m = jnp.log(1.0 + jnp.exp2(z * LOG2E))
        hi = neg_log_1m.astype(BF16)
        lo = (neg_log_1m - hi.astype(F32)).astype(BF16)
        return z, jnp.concatenate([hi, lo], axis=1)

    def weights_stage(z, hilo):
        ext = jnp.dot(hilo, tri, preferred_element_type=F32)
        return jnp.exp(z + ext[:, :LANES]).astype(BF16), ext[:, LANES:]

    def output_stage(a, vrhs):
        return jnp.dot(a, vrhs, preferred_element_type=F32)

    def tile(qrows, krhs, vrhs):
        a, rs = weights_stage(*logits_stage(qrows, krhs, 0))
        return output_stage(a, vrhs), rs

    def q_rows(first, count):
        return q_ref[first * SB:(first + count) * SB, :]

    ktiles = {}
    for t in range(-1, NS // 2):
        kt_tile = kt_ref[jnp.maximum(kb0 // 2 + t, 0)]
        ktiles[t] = (kt_tile, pltpu.roll(kt_tile, SB, 1))

    def first_stage(m):
        krhs = key_rhs(*ktiles[m // 2], m % 2)
        if m == -1:
            return logits_stage(q_rows(0, 1), krhs, 0)
        if m == NS - 1:
            return logits_stage(q_rows(m, 1), krhs, SB)
        return logits_stage(q_rows(m, 2), krhs, SB)

    blocks = list(range(-1, NS))
    stage1, stage2, parts = {}, {}, []
    lag2, lag3 = ATT_SKEW[0], ATT_SKEW[0] + ATT_SKEW[1]
    for s in range(len(blocks) + lag3):
        if s < len(blocks):
            stage1[s] = first_stage(blocks[s])
        if 0 <= s - lag2 < len(blocks):
            stage2[s - lag2] = weights_stage(*stage1.pop(s - lag2))
        if 0 <= s - lag3 < len(blocks):
            a, rs = stage2.pop(s - lag3)
            vrhs = value_rhs(jnp.maximum(kb0 + blocks[s - lag3], 0))
            parts.append((output_stage(a, vrhs), rs))

    carry_max = None
    for r in range(NS):
        pv_p, rs_p = parts[r] if r == 0 else (parts[r][0][SB:], parts[r][1][SB:])
        pv_d, rs_d = parts[r + 1] if r == NS - 1 else (parts[r + 1][0][:SB], parts[r + 1][1][:SB])
        scale = jnp.exp(rs_d)
        if r == 0:
            scale = jnp.where(kb0 > 0, scale, 0.0)
        acc_ref[r] = pv_d + scale * pv_p
        carry_ref[r] = rs_d + rs_p
        carry_max = rs_d + rs_p if carry_max is None else jnp.maximum(carry_max, rs_d + rs_p)

    def sub_block(r, _):
        qr = q_ref[pl.ds(pl.multiple_of(r * SB, SB), SB), :]

        def cond(st):
            j, cmax = st
            return jnp.logical_and(j >= 0, cmax > -STICK_CUTOFF)

        def body(st):
            j, _ = st
            kt_tile = kt_ref[lax.shift_right_logical(j, 1)]
            rolled = pltpu.roll(kt_tile, SB, 1)
            krhs = jnp.where((j & 1) == 0, key_rhs(kt_tile, rolled, 0), key_rhs(kt_tile, rolled, 1))
            pv, rs = tile(qr, krhs, value_rhs(j))
            carry = carry_ref[r]
            acc_ref[r] += jnp.exp(carry) * pv
            carry_ref[r] = carry + rs
            return j - 1, jnp.max(carry + rs)

        @pl.when(cmax_ref[r] > -STICK_CUTOFF)
        def _():
            lax.while_loop(cond, body, (kb0 + r - 2, cmax_ref[r]))

        return 0

    @pl.when(jnp.max(carry_max) > -STICK_CUTOFF)
    def _():
        for r in range(NS):
            cmax_ref[r] = jnp.max(carry_ref[r])
        lax.fori_loop(0, NS, sub_block, 0)

    for r in range(NS):
        o_ref[r * SB:(r + 1) * SB, :] = acc_ref[r].astype(o_ref.dtype)


def _stick_matrix():
    key = np.arange(LANES) % ATT_SUB
    head = np.arange(LANES) // ATT_SUB
    same = head[:, None] == head[None, :]
    suffix = np.logical_and(same, key[:, None] >= key[None, :])
    half = -np.concatenate([suffix, same], axis=1).astype(np.float32)
    return jnp.asarray(np.concatenate([half, half], axis=0), BF16)


def _attention(q, kt, v, *, batch, seq):
    N, W = q.shape
    R = ATT_ROWS
    q3, v3 = q.reshape(batch, seq, W), v.reshape(batch, seq, W)
    tri = _stick_matrix()
    resident = lambda b, hp, i: (b, 0, hp)
    out = pl.pallas_call(
        _attn_kernel,
        grid=(batch, W // LANES, seq // R),
        in_specs=[
            pl.BlockSpec((None, R, LANES), lambda b, hp, i: (b, i, hp)),
            pl.BlockSpec((None, None, seq // ATT_BLOCK, LANES, ATT_BLOCK), lambda b, hp, i: (b, hp, 0, 0, 0)),
            pl.BlockSpec((None, seq, LANES), resident),
            pl.BlockSpec(tri.shape, lambda b, hp, i: (0, 0)),
        ],
        out_specs=pl.BlockSpec((None, R, LANES), lambda b, hp, i: (b, i, hp)),
        out_shape=jax.ShapeDtypeStruct((batch, seq, W), BF16),
        scratch_shapes=[pltpu.VMEM((R // ATT_SUB, ATT_SUB, LANES), F32),
                        pltpu.VMEM((R // ATT_SUB, ATT_SUB, LANES), F32),
                        pltpu.SMEM((R // ATT_SUB,), F32)],
        compiler_params=pltpu.CompilerParams(
            dimension_semantics=("arbitrary", "arbitrary", "arbitrary"),
            vmem_limit_bytes=VMEM_LIMIT),
        name="stickbreak_attn",
    )(q3, kt, v3, tri)
    return out.reshape(N, W)


def _mlp_kernel(x_ref, yab_ref, yc_ref, g2_ref, gf_ref, wo_hbm, wup_hbm, wdn_hbm, o_ref,
                wo_ref, wup_ref, wdn_ref, sem, *, final_norm, ff_chunk):
    @pl.when(pl.program_id(0) == 0)
    def _():
        copies = [pltpu.make_async_copy(src, dst, sem.at[n])
                  for n, (src, dst) in enumerate(((wo_hbm, wo_ref), (wup_hbm, wup_ref), (wdn_hbm, wdn_ref)))]
        for c in copies:
            c.start()
        for c in copies:
            c.wait()

    nab = yab_ref.shape[1]
    T = x_ref.shape[0]
    rows = T // MLP_ROW_GROUPS
    x1s, hs = [], []
    for g in range(MLP_ROW_GROUPS):
        r = slice(g * rows, (g + 1) * rows)
        x1 = (x_ref[r, :]
              + jnp.dot(yab_ref[r, :], wo_ref[0:nab, :], preferred_element_type=F32)
              + jnp.dot(yc_ref[r, :], wo_ref[nab:, :], preferred_element_type=F32))
        x1s.append(x1)
        hs.append(_rms(x1, g2_ref[...]).astype(BF16))
    for g in range(MLP_ROW_GROUPS):
        acc = None
        for c in range(wup_ref.shape[1] // ff_chunk):
            up = jnp.dot(hs[g], wup_ref[:, c * ff_chunk:(c + 1) * ff_chunk], preferred_element_type=F32)
            act = jnp.square(jnp.maximum(up, 0.0)).astype(BF16)
            dn = jnp.dot(act, wdn_ref[c * ff_chunk:(c + 1) * ff_chunk, :], preferred_element_type=F32)
            acc = dn if acc is None else acc + dn
        acc = acc + x1s[g]
        if final_norm:
            acc = _rms(acc, gf_ref[...])
        o_ref[g * rows:(g + 1) * rows, :] = acc


def _mlp(x2, yab, yc, wo, g2, wup, wdn, gf, *, final_norm):
    N, D = x2.shape
    T = ROW_TILE
    const = lambda i: (0, 0)
    row = lambda i: (i, 0)
    hbm = pl.BlockSpec(memory_space=pl.ANY)
    return pl.pallas_call(
        functools.partial(_mlp_kernel, final_norm=final_norm, ff_chunk=1024),
        grid=(N // T,),
        in_specs=[
            pl.BlockSpec((T, D), row),
            pl.BlockSpec((T, yab.shape[1]), row),
            pl.BlockSpec((T, yc.shape[1]), row),
            pl.BlockSpec((1, D), const),
            pl.BlockSpec((1, D), const),
            hbm, hbm, hbm,
        ],
        out_specs=pl.BlockSpec((T, D), row),
        out_shape=jax.ShapeDtypeStruct((N, D), F32),
        scratch_shapes=[pltpu.VMEM(wo.shape, BF16), pltpu.VMEM(wup.shape, BF16),
                        pltpu.VMEM(wdn.shape, BF16), pltpu.SemaphoreType.DMA((3,))],
        compiler_params=pltpu.CompilerParams(
            dimension_semantics=("arbitrary",), vmem_limit_bytes=VMEM_LIMIT),
        name="outproj_mlp",
    )(x2, yab, yc, g2, gf, wo, wup, wdn)


def kernel(x, norm1, w_in, pool_w, pool_scale, sg_norm, sg_w, sg_b, w_out, norm2, w_up, w_down, final_norm):
    B, S, D = x.shape
    depth = norm1.shape[0]
    n_grp, gw = pool_w.shape[1], pool_w.shape[2]
    pw, sw = pool_scale.shape[1], sg_norm.shape[1]
    sbw = (w_in.shape[2] - pw - 2 * sw) // 3
    assert S % ROW_TILE == 0 and S % ATT_ROWS == 0 and ROW_TILE % CHUNK == 0 and ROW_TILE % ATT_BLOCK == 0
    assert sg_w.shape[1] == SG_HEADS and sg_w.shape[2] == CHUNK and n_grp == len(POOL_WINDOWS)
    assert sbw % LANES == 0 and ATT_BLOCK == LANES == 2 * ATT_SUB == 2 * SB_HD and (ATT_ROWS // ATT_SUB) % 2 == 0

    x2 = x.reshape(B * S, D)
    ck = pw + 2 * sw + sbw
    for l in range(depth):
        win = jnp.concatenate([w_in[l][:, :ck], w_in[l][:, ck + sbw:]], axis=1).astype(BF16)
        wkt = w_in[l][:, ck:ck + sbw].T.astype(BF16)
        poolw = jax.scipy.linalg.block_diag(*[pool_w[l, g] for g in range(n_grp)]).astype(BF16)
        sgb = jnp.repeat(sg_b[l].T, sw // SG_HEADS, axis=1)
        yab, q, kt, v = _inproj(x2, norm1[l][None], win, wkt, poolw, pool_scale[l][None],
                                sg_norm[l][None], sg_w[l], sgb, batch=B, seq=S)
        yc = _attention(q, kt, v, batch=B, seq=S)
        x2 = _mlp(x2, yab, yc, w_out[l].astype(BF16), norm2[l][None],
                  w_up[l].astype(BF16), w_down[l].astype(BF16), final_norm[None],
                  final_norm=(l == depth - 1))
    return x2.reshape(B, S, D)
```

```python
import functools

import jax
import jax.numpy as jnp
import numpy as np
from jax import lax
from jax.experimental import pallas as pl
from jax.experimental.pallas import tpu as pltpu

EPS = 1e-6
POOL_WINDOWS = (2, 4, 8, 16)
POOL_HALO = 16
CHUNK = 128
SG_HEADS = 4
SB_HD = 64
LANES = 128

ROW_TILE = 1024
MLP_ROW_GROUPS = 2
ATT_BLOCK = 128
ATT_SUB = 64
ATT_ROWS = 8192
ATT_SCREEN_GROUP = 16
ATT_SKEW = (3, 4)
STICK_CUTOFF = 30.0
MASK_LOGIT = -1e30
LOGIT_CAP = 80.0
LOG2E = 1.4426950408889634
VMEM_LIMIT = 56 * 1024 * 1024

F32 = jnp.float32
BF16 = jnp.bfloat16


def _rms(x, g):
    ms = jnp.mean(x * x, axis=-1, keepdims=True)
    return x * lax.rsqrt(ms + EPS) * g


def _gelu_tanh(x):
    c = np.float32(np.sqrt(2.0 / np.pi))
    return 0.5 * x * (1.0 + jnp.tanh(c * (x + 0.044715 * (x * x * x))))


def _inproj_kernel(x_ref, g1_ref, win_ref, wkt_ref, poolw_ref, pscale_ref, sgn_ref, sgw_ref, sgb_ref,
                   yab_ref, q_ref, kt_ref, v_ref, lvl_ref, *, tiles_per_seq):
    T = x_ref.shape[0]
    pw = pscale_ref.shape[1]
    sw = sgn_ref.shape[1]
    i = pl.program_id(0)
    tile_in_seq = i % tiles_per_seq

    H = POOL_HALO

    @pl.when(i == 0)
    def _():
        lvl_ref[:, 0:H, :] = jnp.zeros((lvl_ref.shape[0], H, pw), F32)

    @pl.when(tile_in_seq == 0)
    def _():
        lvl_ref[0, H:2 * H, :] = jnp.zeros((H, pw), F32)

    @pl.when(tile_in_seq != 0)
    def _():
        lvl_ref[0, H:2 * H, :] = lvl_ref[0, T + H:T + 2 * H, :]

    hs, bpres, a_s = [], [], []
    for r0 in range(0, T, T // 2):
        hg = _rms(x_ref[r0:r0 + T // 2, :], g1_ref[...]).astype(BF16)
        bpres.append(jnp.dot(hg, win_ref[:, pw:pw + 2 * sw], preferred_element_type=F32))
        a_s.append(jnp.dot(hg, win_ref[:, 0:pw], preferred_element_type=F32))
        hs.append(hg)
    h = jnp.concatenate(hs, axis=0)
    bpre = jnp.concatenate(bpres, axis=0)
    a = jnp.concatenate(a_s, axis=0)
    c2 = pw + 2 * sw
    sbw = q_ref.shape[1]
    half = sbw // 2

    def project(out_ref, col, out_col, scale=None):
        r = jnp.dot(h, win_ref[:, col:col + half], preferred_element_type=F32)
        out_ref[:, out_col:out_col + half] = (r if scale is None else r * scale).astype(BF16)

    r_i = lax.broadcasted_iota(jnp.int32, (CHUNK, CHUNK), 0)
    c_i = lax.broadcasted_iota(jnp.int32, (CHUNK, CHUNK), 1)
    tril = c_i <= r_i
    wm = jnp.concatenate([jnp.where(tril, sgw_ref[hh], 0.0).astype(BF16) for hh in range(SG_HEADS)], axis=1)
    head = lax.broadcasted_iota(jnp.int32, (1, sw), 1) // (sw // SG_HEADS)

    def gate_chunk(c):
        bz = _gelu_tanh(bpre[c * CHUNK:(c + 1) * CHUNK, :])
        vc = _rms(bz[:, sw:2 * sw], sgn_ref[...]).astype(BF16)
        vstack = jnp.concatenate([jnp.where(head == hh, vc, jnp.zeros_like(vc)) for hh in range(SG_HEADS)], axis=0)
        sv = jnp.dot(wm, vstack, preferred_element_type=F32) + sgb_ref[...]
        yab_ref[c * CHUNK:(c + 1) * CHUNK, pw:pw + sw] = (bz[:, 0:sw] * sv).astype(BF16)

    q_scale = float(SB_HD ** -0.5)
    pieces = [(q_ref, c2, 0, q_scale), (q_ref, c2 + half, half, q_scale),
              (v_ref, c2 + sbw, 0, None), (v_ref, c2 + sbw + half, half, None)]
    n_chunks = T // CHUNK
    per_piece = -(-n_chunks // len(pieces))
    for p, piece in enumerate(pieces):
        project(*piece)
        for c in range(p * per_piece, min((p + 1) * per_piece, n_chunks)):
            gate_chunk(c)

    lo, n = H, T + H
    lvl_ref[0, 2 * H:2 * H + T, :] = a
    for s, shift in enumerate((1, 2, 4, 8)):
        lvl_ref[s + 1, lo:lo + n, :] = (lvl_ref[s, lo:lo + n, :]
                                        + lvl_ref[s, lo - shift:lo - shift + n, :])
    lane = lax.broadcasted_iota(jnp.int32, (1, pw), 1)
    grp = lane // (pw // len(POOL_WINDOWS))
    win = jnp.where(grp == 0, 2, jnp.where(grp == 1, 4, jnp.where(grp == 2, 8, 16)))
    psum = jnp.where(grp == 0, lvl_ref[1, 2 * H:2 * H + T, :],
                     jnp.where(grp == 1, lvl_ref[2, 2 * H:2 * H + T, :],
                               jnp.where(grp == 2, lvl_ref[3, 2 * H:2 * H + T, :],
                                         lvl_ref[4, 2 * H:2 * H + T, :])))
    pos = tile_in_seq * T + lax.broadcasted_iota(jnp.int32, (T, 1), 0)
    cnt = jnp.minimum(pos + 1, win).astype(F32)
    d = psum / cnt - a
    ya = jnp.dot(d.astype(BF16), poolw_ref[...], preferred_element_type=F32) * pscale_ref[...]
    yab_ref[:, 0:pw] = ya.astype(BF16)

    kt = lax.dot_general(wkt_ref[...], h, (((1,), (1,)), ((), ())),
                         preferred_element_type=F32).astype(BF16)
    for hp in range(kt_ref.shape[0]):
        for c in range(kt_ref.shape[1]):
            kt_ref[hp, c] = kt[hp * LANES:(hp + 1) * LANES, c * ATT_BLOCK:(c + 1) * ATT_BLOCK]


def _inproj(x2, g1, win, wkt, poolw, pscale, sgn, sgw, sgb, *, batch, seq):
    N, D = x2.shape
    T = ROW_TILE
    pw, sw = pscale.shape[1], sgn.shape[1]
    sbw = wkt.shape[0]
    tps = seq // T
    const = lambda i: (0, 0)
    row = lambda i: (i, 0)
    return pl.pallas_call(
        functools.partial(_inproj_kernel, tiles_per_seq=tps),
        grid=(N // T,),
        in_specs=[
            pl.BlockSpec((T, D), row),
            pl.BlockSpec((1, D), const),
            pl.BlockSpec(win.shape, const),
            pl.BlockSpec(wkt.shape, const),
            pl.BlockSpec(poolw.shape, const),
            pl.BlockSpec((1, pw), const),
            pl.BlockSpec((1, sw), const),
            pl.BlockSpec(sgw.shape, lambda i: (0, 0, 0)),
            pl.BlockSpec(sgb.shape, const),
        ],
        out_specs=[
            pl.BlockSpec((T, pw + sw), row),
            pl.BlockSpec((T, sbw), row),
            pl.BlockSpec((None, sbw // LANES, T // ATT_BLOCK, LANES, ATT_BLOCK),
                         lambda i: (i // tps, 0, i % tps, 0, 0)),
            pl.BlockSpec((T, sbw), row),
        ],
        out_shape=[
            jax.ShapeDtypeStruct((N, pw + sw), BF16),
            jax.ShapeDtypeStruct((N, sbw), BF16),
            jax.ShapeDtypeStruct((batch, sbw // LANES, seq // ATT_BLOCK, LANES, ATT_BLOCK), BF16),
            jax.ShapeDtypeStruct((N, sbw), BF16),
        ],
        scratch_shapes=[pltpu.VMEM((len(POOL_WINDOWS) + 1, T + 2 * POOL_HALO, pw), F32)],
        compiler_params=pltpu.CompilerParams(
            dimension_semantics=("arbitrary",), vmem_limit_bytes=VMEM_LIMIT),
        name="inproj_mixers",
    )(x2, g1, win, wkt, poolw, pscale, sgn, sgw, sgb)


def _attn_kernel(q_ref, kt_ref, v_ref, tri_ref, o_ref, acc_ref, carry_ref, gmax_ref, cmax_ref):
    SB = ATT_SUB
    NS = o_ref.shape[0] // SB
    i = pl.program_id(2)
    kb0 = i * NS
    head0_lane = lax.broadcasted_iota(jnp.int32, (1, LANES), 1) < SB_HD
    row = lax.broadcasted_iota(jnp.int32, (SB, LANES), 0)
    col = lax.broadcasted_iota(jnp.int32, (SB, LANES), 1) & (SB - 1)
    cap_diag = jnp.where(col < row, LOGIT_CAP, MASK_LOGIT)
    tri = tri_ref[...]

    def key_rhs(ktile, rolled, half):
        h0, h1 = (ktile, rolled) if half == 0 else (rolled, ktile)
        zero = jnp.zeros((SB_HD, LANES), ktile.dtype)
        return jnp.concatenate([jnp.where(head0_lane, h0[:SB_HD], zero),
                                jnp.where(head0_lane, zero, h1[SB_HD:])], axis=0)

    def value_rhs(kb):
        vrows = v_ref[pl.ds(pl.multiple_of(kb * SB, SB), SB), :]
        zero = jnp.zeros_like(vrows)
        return jnp.concatenate([jnp.where(head0_lane, vrows, zero),
                                jnp.where(head0_lane, zero, vrows)], axis=0)

    def logits_stage(qrows, krhs, diagonal_rows):
        z = jnp.dot(qrows, krhs, preferred_element_type=F32)
        if diagonal_rows == z.shape[0]:
            z = jnp.minimum(z, cap_diag)
        elif diagonal_rows:
            z = jnp.concatenate([jnp.minimum(z[:diagonal_rows], cap_diag),
                                 jnp.minimum(z[diagonal_rows:], LOGIT_CAP)], axis=0)
        else:
            z = jnp.minimum(z, LOGIT_CAP)
        neg_log_1m = jnp.log(1.0 + jnp.exp2(z * LOG2E))
        hi = neg_log_1m.astype(BF16)
        lo = (neg_log_1m - hi.astype(F32)).astype(BF16)
        return z, jnp.concatenate([hi, lo], axis=1)

    def weights_stage(z, hilo):
        ext = jnp.dot(hilo, tri, preferred_element_type=F32)
        return jnp.exp(z + ext[:, :LANES]).astype(BF16), ext[:, LANES:]

    def output_stage(a, vrhs):
        return jnp.dot(a, vrhs, preferred_element_type=F32)

    def tile(qrows, krhs, vrhs):
        a, rs = weights_stage(*logits_stage(qrows, krhs, 0))
        return output_stage(a, vrhs), rs

    def q_rows(first, count):
        return q_ref[first * SB:(first + count) * SB, :]

    ktiles = {}
    for t in range(-1, NS // 2):
        kt_tile = kt_ref[jnp.maximum(kb0 // 2 + t, 0)]
        ktiles[t] = (kt_tile, pltpu.roll(kt_tile, SB, 1))

    def first_stage(m):
        krhs = key_rhs(*ktiles[m // 2], m % 2)
        if m == -1:
            return logits_stage(q_rows(0, 1), krhs, 0)
        if m == NS - 1:
            return logits_stage(q_rows(m, 1), krhs, SB)
        return logits_stage(q_rows(m, 2), krhs, SB)

    blocks = list(range(-1, NS))
    stage1, stage2, parts = {}, {}, []
    lag2, lag3 = ATT_SKEW[0], ATT_SKEW[0] + ATT_SKEW[1]
    for s in range(len(blocks) + lag3):
        if s < len(blocks):
            stage1[s] = first_stage(blocks[s])
        if 0 <= s - lag2 < len(blocks):
            stage2[s - lag2] = weights_stage(*stage1.pop(s - lag2))
        if 0 <= s - lag3 < len(blocks):
            a, rs = stage2.pop(s - lag3)
            vrhs = value_rhs(jnp.maximum(kb0 + blocks[s - lag3], 0))
            parts.append((output_stage(a, vrhs), rs))

    carry_max = None
    for r in range(NS):
        pv_p, rs_p = parts[r] if r == 0 else (parts[r][0][SB:], parts[r][1][SB:])
        pv_d, rs_d = parts[r + 1] if r == NS - 1 else (parts[r + 1][0][:SB], parts[r + 1][1][:SB])
        scale = jnp.exp(rs_d)
        if r == 0:
            scale = jnp.where(kb0 > 0, scale, 0.0)
        acc_ref[r] = pv_d + scale * pv_p
        carry_ref[r] = rs_d + rs_p
        carry_max = rs_d + rs_p if carry_max is None else jnp.maximum(carry_max, rs_d + rs_p)

    def sub_block(r, cmax0):
        qr = q_ref[pl.ds(pl.multiple_of(r * SB, SB), SB), :]

        def cond(st):
            j, cmax = st
            return jnp.logical_and(j >= 0, cmax > -STICK_CUTOFF)

        def body(st):
            j, _ = st
            kt_tile = kt_ref[lax.shift_right_logical(j, 1)]
            rolled = pltpu.roll(kt_tile, SB, 1)
            krhs = jnp.where((j & 1) == 0, key_rhs(kt_tile, rolled, 0), key_rhs(kt_tile, rolled, 1))
            pv, rs = tile(qr, krhs, value_rhs(j))
            carry = carry_ref[r]
            acc_ref[r] += jnp.exp(carry) * pv
            carry_ref[r] = carry + rs
            return j - 1, jnp.max(carry + rs)

        lax.while_loop(cond, body, (kb0 + r - 2, cmax0))

    G = ATT_SCREEN_GROUP
    n_groups = NS // G

    def group(g, _):
        @pl.when(gmax_ref[g] > -STICK_CUTOFF)
        def _():
            for k in range(G):
                cmax_ref[k] = jnp.max(carry_ref[g * G + k])

            def member(k, _):
                @pl.when(cmax_ref[k] > -STICK_CUTOFF)
                def _():
                    sub_block(g * G + k, cmax_ref[k])

                return 0

            lax.fori_loop(0, G, member, 0)

        return 0

    @pl.when(jnp.max(carry_max) > -STICK_CUTOFF)
    def _():
        for g in range(n_groups):
            gmax = carry_ref[g * G]
            for k in range(1, G):
                gmax = jnp.maximum(gmax, carry_ref[g * G + k])
            gmax_ref[g] = jnp.max(gmax)
        lax.fori_loop(0, n_groups, group, 0)

    for r in range(NS):
        o_ref[r * SB:(r + 1) * SB, :] = acc_ref[r].astype(o_ref.dtype)


def _stick_matrix():
    key = np.arange(LANES) % ATT_SUB
    head = np.arange(LANES) // ATT_SUB
    same = head[:, None] == head[None, :]
    suffix = np.logical_and(same, key[:, None] >= key[None, :])
    half = -np.concatenate([suffix, same], axis=1).astype(np.float32)
    return jnp.asarray(np.concatenate([half, half], axis=0), BF16)


def _attention(q, kt, v, *, batch, seq):
    N, W = q.shape
    R = ATT_ROWS
    q3, v3 = q.reshape(batch, seq, W), v.reshape(batch, seq, W)
    tri = _stick_matrix()
    resident = lambda b, hp, i: (b, 0, hp)
    out = pl.pallas_call(
        _attn_kernel,
        grid=(batch, W // LANES, seq // R),
        in_specs=[
            pl.BlockSpec((None, R, LANES), lambda b, hp, i: (b, i, hp)),
            pl.BlockSpec((None, None, seq // ATT_BLOCK, LANES, ATT_BLOCK), lambda b, hp, i: (b, hp, 0, 0, 0)),
            pl.BlockSpec((None, seq, LANES), resident),
            pl.BlockSpec(tri.shape, lambda b, hp, i: (0, 0)),
        ],
        out_specs=pl.BlockSpec((None, R, LANES), lambda b, hp, i: (b, i, hp)),
        out_shape=jax.ShapeDtypeStruct((batch, seq, W), BF16),
        scratch_shapes=[pltpu.VMEM((R // ATT_SUB, ATT_SUB, LANES), F32),
                        pltpu.VMEM((R // ATT_SUB, ATT_SUB, LANES), F32),
                        pltpu.SMEM((R // ATT_SUB // ATT_SCREEN_GROUP,), F32),
                        pltpu.SMEM((ATT_SCREEN_GROUP,), F32)],
        compiler_params=pltpu.CompilerParams(
            dimension_semantics=("arbitrary", "arbitrary", "arbitrary"),
            vmem_limit_bytes=VMEM_LIMIT),
        name="stickbreak_attn",
    )(q3, kt, v3, tri)
    return out.reshape(N, W)


def _mlp_kernel(x_ref, yab_ref, yc_ref, g2_ref, gf_ref, wo_hbm, wup_hbm, wdn_hbm, o_ref,
                wo_ref, wup_ref, wdn_ref, sem, *, layer, final_norm, ff_chunk):
    @pl.when(pl.program_id(0) == 0)
    def _():
        copies = [pltpu.make_async_copy(src, dst, sem.at[n])
                  for n, (src, dst) in enumerate(((wo_hbm.at[layer], wo_ref), (wup_hbm.at[layer], wup_ref),
                                                  (wdn_hbm.at[layer], wdn_ref)))]
        for c in copies:
            c.start()
        for c in copies:
            c.wait()

    nab = yab_ref.shape[1]
    T = x_ref.shape[0]
    rows = T // MLP_ROW_GROUPS
    x1s, hs = [], []
    for g in range(MLP_ROW_GROUPS):
        r = slice(g * rows, (g + 1) * rows)
        x1 = (x_ref[r, :]
              + jnp.dot(yab_ref[r, :], wo_ref[0:nab, :], preferred_element_type=F32)
              + jnp.dot(yc_ref[r, :], wo_ref[nab:, :], preferred_element_type=F32))
        x1s.append(x1)
        hs.append(_rms(x1, g2_ref[...]).astype(BF16))
    for g in range(MLP_ROW_GROUPS):
        acc = None
        for c in range(wup_ref.shape[1] // ff_chunk):
            up = jnp.dot(hs[g], wup_ref[:, c * ff_chunk:(c + 1) * ff_chunk], preferred_element_type=F32)
            act = jnp.square(jnp.maximum(up, 0.0)).astype(BF16)
            dn = jnp.dot(act, wdn_ref[c * ff_chunk:(c + 1) * ff_chunk, :], preferred_element_type=F32)
            acc = dn if acc is None else acc + dn
        acc = acc + x1s[g]
        if final_norm:
            acc = _rms(acc, gf_ref[...])
        o_ref[g * rows:(g + 1) * rows, :] = acc


def _mlp(x2, yab, yc, wo, g2, wup, wdn, gf, *, layer, final_norm):
    N, D = x2.shape
    T = ROW_TILE
    const = lambda i: (0, 0)
    row = lambda i: (i, 0)
    hbm = pl.BlockSpec(memory_space=pl.ANY)
    return pl.pallas_call(
        functools.partial(_mlp_kernel, layer=layer, final_norm=final_norm, ff_chunk=1024),
        grid=(N // T,),
        in_specs=[
            pl.BlockSpec((T, D), row),
            pl.BlockSpec((T, yab.shape[1]), row),
            pl.BlockSpec((T, yc.shape[1]), row),
            pl.BlockSpec((1, D), const),
            pl.BlockSpec((1, D), const),
            hbm, hbm, hbm,
        ],
        out_specs=pl.BlockSpec((T, D), row),
        out_shape=jax.ShapeDtypeStruct((N, D), F32),
        scratch_shapes=[pltpu.VMEM(wo.shape[1:], BF16), pltpu.VMEM(wup.shape[1:], BF16),
                        pltpu.VMEM(wdn.shape[1:], BF16), pltpu.SemaphoreType.DMA((3,))],
        compiler_params=pltpu.CompilerParams(
            dimension_semantics=("arbitrary",), vmem_limit_bytes=VMEM_LIMIT),
        name="outproj_mlp",
    )(x2, yab, yc, g2, gf, wo, wup, wdn)


def kernel(x, norm1, w_in, pool_w, pool_scale, sg_norm, sg_w, sg_b, w_out, norm2, w_up, w_down, final_norm):
    B, S, D = x.shape
    depth = norm1.shape[0]
    n_grp, gw = pool_w.shape[1], pool_w.shape[2]
    pw, sw = pool_scale.shape[1], sg_norm.shape[1]
    sbw = (w_in.shape[2] - pw - 2 * sw) // 3
    assert S % ROW_TILE == 0 and S % ATT_ROWS == 0 and ROW_TILE % CHUNK == 0 and ROW_TILE % ATT_BLOCK == 0
    assert sg_w.shape[1] == SG_HEADS and sg_w.shape[2] == CHUNK and n_grp == len(POOL_WINDOWS)
    assert sbw % LANES == 0 and ATT_BLOCK == LANES == 2 * ATT_SUB == 2 * SB_HD and (ATT_ROWS // ATT_SUB) % max(2, ATT_SCREEN_GROUP) == 0

    wo_all, wup_all, wdn_all = w_out.astype(BF16), w_up.astype(BF16), w_down.astype(BF16)
    x2 = x.reshape(B * S, D)
    ck = pw + 2 * sw + sbw
    for l in range(depth):
        win = jnp.concatenate([w_in[l][:, :ck], w_in[l][:, ck + sbw:]], axis=1).astype(BF16)
        wkt = w_in[l][:, ck:ck + sbw].T.astype(BF16)
        poolw = jax.scipy.linalg.block_diag(*[pool_w[l, g] for g in range(n_grp)]).astype(BF16)
        sgb = jnp.repeat(sg_b[l].T, sw // SG_HEADS, axis=1)
        yab, q, kt, v = _inproj(x2, norm1[l][None], win, wkt, poolw, pool_scale[l][None],
                                sg_norm[l][None], sg_w[l], sgb, batch=B, seq=S)
        yc = _attention(q, kt, v, batch=B, seq=S)
        x2 = _mlp(x2, yab, yc, wo_all, norm2[l][None], wup_all, wdn_all, final_norm[None],
                  layer=l, final_norm=(l == depth - 1))
    return x2.reshape(B, S, D)
```

```python
import functools

import jax
import jax.numpy as jnp
import numpy as np
from jax import lax
from jax.experimental import pallas as pl
from jax.experimental.pallas import tpu as pltpu

EPS = 1e-6
POOL_WINDOWS = (2, 4, 8, 16)
POOL_HALO = 16
CHUNK = 128
SG_HEADS = 4
SB_HD = 64
LANES = 128

ROW_TILE = 1024
MLP_ROW_GROUPS = 2
ATT_BLOCK = 128
ATT_SUB = 64
ATT_ROWS = 8192
ATT_SCREEN_GROUP = 16
ATT_SKEW = (3, 4)
STICK_CUTOFF = 30.0
MASK_LOGIT = -1e30
LOGIT_CAP = 80.0
LOG2E = 1.4426950408889634
VMEM_LIMIT = 56 * 1024 * 1024

F32 = jnp.float32
BF16 = jnp.bfloat16


def _rms(x, g):
    ms = jnp.mean(x * x, axis=-1, keepdims=True)
    return x * lax.rsqrt(ms + EPS) * g


def _gelu_tanh(x):
    c = np.float32(np.sqrt(2.0 / np.pi))
    return 0.5 * x * (1.0 + jnp.tanh(c * (x + 0.044715 * (x * x * x))))


def _inproj_kernel(x_ref, g1_ref, win_ref, wkt_ref, poolw_ref, pscale_ref, sgn_ref, sgw_ref, sgb_ref,
                   yab_ref, q_ref, kt_ref, v_ref, lvl_ref, *, tiles_per_seq):
    T = x_ref.shape[0]
    pw = pscale_ref.shape[1]
    sw = sgn_ref.shape[1]
    i = pl.program_id(0)
    tile_in_seq = i % tiles_per_seq

    H = POOL_HALO

    @pl.when(i == 0)
    def _():
        lvl_ref[:, 0:H, :] = jnp.zeros((lvl_ref.shape[0], H, pw), F32)

    @pl.when(tile_in_seq == 0)
    def _():
        lvl_ref[0, H:2 * H, :] = jnp.zeros((H, pw), F32)

    @pl.when(tile_in_seq != 0)
    def _():
        lvl_ref[0, H:2 * H, :] = lvl_ref[0, T + H:T + 2 * H, :]

    hs, bpres, a_s = [], [], []
    for r0 in range(0, T, T // 2):
        hg = _rms(x_ref[r0:r0 + T // 2, :], g1_ref[...]).astype(BF16)
        bpres.append(jnp.dot(hg, win_ref[:, pw:pw + 2 * sw], preferred_element_type=F32))
        a_s.append(jnp.dot(hg, win_ref[:, 0:pw], preferred_element_type=F32))
        hs.append(hg)
    h = jnp.concatenate(hs, axis=0)
    bpre = jnp.concatenate(bpres, axis=0)
    a = jnp.concatenate(a_s, axis=0)
    c2 = pw + 2 * sw
    sbw = q_ref.shape[1]
    half = sbw // 2

    def project(out_ref, col, out_col, scale=None):
        r = jnp.dot(h, win_ref[:, col:col + half], preferred_element_type=F32)
        out_ref[:, out_col:out_col + half] = (r if scale is None else r * scale).astype(BF16)

    r_i = lax.broadcasted_iota(jnp.int32, (CHUNK, CHUNK), 0)
    c_i = lax.broadcasted_iota(jnp.int32, (CHUNK, CHUNK), 1)
    tril = c_i <= r_i
    wm = jnp.concatenate([jnp.where(tril, sgw_ref[hh], 0.0).astype(BF16) for hh in range(SG_HEADS)], axis=1)
    head = lax.broadcasted_iota(jnp.int32, (1, sw), 1) // (sw // SG_HEADS)

    def gate_chunk(c):
        bz = _gelu_tanh(bpre[c * CHUNK:(c + 1) * CHUNK, :])
        vc = _rms(bz[:, sw:2 * sw], sgn_ref[...]).astype(BF16)
        vstack = jnp.concatenate([jnp.where(head == hh, vc, jnp.zeros_like(vc)) for hh in range(SG_HEADS)], axis=0)
        sv = jnp.dot(wm, vstack, preferred_element_type=F32) + sgb_ref[...]
        yab_ref[c * CHUNK:(c + 1) * CHUNK, pw:pw + sw] = (bz[:, 0:sw] * sv).astype(BF16)

    q_scale = float(SB_HD ** -0.5)
    pieces = [(q_ref, c2, 0, q_scale), (q_ref, c2 + half, half, q_scale),
              (v_ref, c2 + 2 * sbw, 0, None), (v_ref, c2 + 2 * sbw + half, half, None)]
    n_chunks = T // CHUNK
    per_piece = -(-n_chunks // len(pieces))
    for p, piece in enumerate(pieces):
        project(*piece)
        for c in range(p * per_piece, min((p + 1) * per_piece, n_chunks)):
            gate_chunk(c)

    lo, n = H, T + H
    lvl_ref[0, 2 * H:2 * H + T, :] = a
    for s, shift in enumerate((1, 2, 4, 8)):
        lvl_ref[s + 1, lo:lo + n, :] = (lvl_ref[s, lo:lo + n, :]
                                        + lvl_ref[s, lo - shift:lo - shift + n, :])
    lane = lax.broadcasted_iota(jnp.int32, (1, pw), 1)
    grp = lane // (pw // len(POOL_WINDOWS))
    win = jnp.where(grp == 0, 2, jnp.where(grp == 1, 4, jnp.where(grp == 2, 8, 16)))
    psum = jnp.where(grp == 0, lvl_ref[1, 2 * H:2 * H + T, :],
                     jnp.where(grp == 1, lvl_ref[2, 2 * H:2 * H + T, :],
                               jnp.where(grp == 2, lvl_ref[3, 2 * H:2 * H + T, :],
                                         lvl_ref[4, 2 * H:2 * H + T, :])))
    pos = tile_in_seq * T + lax.broadcasted_iota(jnp.int32, (T, 1), 0)
    cnt = jnp.minimum(pos + 1, win).astype(F32)
    d = psum / cnt - a
    ya = jnp.dot(d.astype(BF16), poolw_ref[...], preferred_element_type=F32) * pscale_ref[...]
    yab_ref[:, 0:pw] = ya.astype(BF16)

    kt = lax.dot_general(wkt_ref[...], h, (((1,), (1,)), ((), ())),
                         preferred_element_type=F32).astype(BF16)
    for hp in range(kt_ref.shape[0]):
        for c in range(kt_ref.shape[1]):
            kt_ref[hp, c] = kt[hp * LANES:(hp + 1) * LANES, c * ATT_BLOCK:(c + 1) * ATT_BLOCK]


def _inproj(x2, g1, win, wkt, poolw, pscale, sgn, sgw, sgb, *, layer, batch, seq):
    N, D = x2.shape
    T = ROW_TILE
    pw, sw = pscale.shape[2], sgn.shape[2]
    of_layer = lambda arr: pl.BlockSpec((None,) + arr.shape[1:], lambda i: (layer, 0, 0))
    sbw = wkt.shape[1]
    tps = seq // T
    const = lambda i: (0, 0)
    row = lambda i: (i, 0)
    return pl.pallas_call(
        functools.partial(_inproj_kernel, tiles_per_seq=tps),
        grid=(N // T,),
        in_specs=[
            pl.BlockSpec((T, D), row),
            of_layer(g1),
            of_layer(win),
            of_layer(wkt),
            of_layer(poolw),
            of_layer(pscale),
            of_layer(sgn),
            pl.BlockSpec((None,) + sgw.shape[1:], lambda i: (layer, 0, 0, 0)),
            of_layer(sgb),
        ],
        out_specs=[
            pl.BlockSpec((T, pw + sw), row),
            pl.BlockSpec((T, sbw), row),
            pl.BlockSpec((None, sbw // LANES, T // ATT_BLOCK, LANES, ATT_BLOCK),
                         lambda i: (i // tps, 0, i % tps, 0, 0)),
            pl.BlockSpec((T, sbw), row),
        ],
        out_shape=[
            jax.ShapeDtypeStruct((N, pw + sw), BF16),
            jax.ShapeDtypeStruct((N, sbw), BF16),
            jax.ShapeDtypeStruct((batch, sbw // LANES, seq // ATT_BLOCK, LANES, ATT_BLOCK), BF16),
            jax.ShapeDtypeStruct((N, sbw), BF16),
        ],
        scratch_shapes=[pltpu.VMEM((len(POOL_WINDOWS) + 1, T + 2 * POOL_HALO, pw), F32)],
        compiler_params=pltpu.CompilerParams(
            dimension_semantics=("arbitrary",), vmem_limit_bytes=VMEM_LIMIT),
        name="inproj_mixers",
    )(x2, g1, win, wkt, poolw, pscale, sgn, sgw, sgb)


def _attn_kernel(q_ref, kt_ref, v_ref, tri_ref, o_ref, acc_ref, carry_ref, gmax_ref, cmax_ref):
    SB = ATT_SUB
    NS = o_ref.shape[0] // SB
    i = pl.program_id(2)
    kb0 = i * NS
    head0_lane = lax.broadcasted_iota(jnp.int32, (1, LANES), 1) < SB_HD
    row = lax.broadcasted_iota(jnp.int32, (SB, LANES), 0)
    col = lax.broadcasted_iota(jnp.int32, (SB, LANES), 1) & (SB - 1)
    cap_diag = jnp.where(col < row, LOGIT_CAP, MASK_LOGIT)
    tri = tri_ref[...]

    def key_rhs(ktile, rolled, half):
        h0, h1 = (ktile, rolled) if half == 0 else (rolled, ktile)
        zero = jnp.zeros((SB_HD, LANES), ktile.dtype)
        return jnp.concatenate([jnp.where(head0_lane, h0[:SB_HD], zero),
                                jnp.where(head0_lane, zero, h1[SB_HD:])], axis=0)

    def value_rhs(kb):
        vrows = v_ref[pl.ds(pl.multiple_of(kb * SB, SB), SB), :]
        zero = jnp.zeros_like(vrows)
        return jnp.concatenate([jnp.where(head0_lane, vrows, zero),
                                jnp.where(head0_lane, zero, vrows)], axis=0)

    def logits_stage(qrows, krhs, diagonal_rows):
        z = jnp.dot(qrows, krhs, preferred_element_type=F32)
        if diagonal_rows == z.shape[0]:
            z = jnp.minimum(z, cap_diag)
        elif diagonal_rows:
            z = jnp.concatenate([jnp.minimum(z[:diagonal_rows], cap_diag),
                                 jnp.minimum(z[diagonal_rows:], LOGIT_CAP)], axis=0)
        else:
            z = jnp.minimum(z, LOGIT_CAP)
        neg_log_1m = jnp.log(1.0 + jnp.exp2(z * LOG2E))
        hi = neg_log_1m.astype(BF16)
        lo = (neg_log_1m - hi.astype(F32)).astype(BF16)
        return z, jnp.concatenate([hi, lo], axis=1)

    def weights_stage(z, hilo):
        ext = jnp.dot(hilo, tri, preferred_element_type=F32)
        return jnp.exp(z + ext[:, :LANES]).astype(BF16), ext[:, LANES:]

    def output_stage(a, vrhs):
        return jnp.dot(a, vrhs, preferred_element_type=F32)

    def tile(qrows, krhs, vrhs):
        a, rs = weights_stage(*logits_stage(qrows, krhs, 0))
        return output_stage(a, vrhs), rs

    def q_rows(first, count):
        return q_ref[first * SB:(first + count) * SB, :]

    ktiles = {}
    for t in range(-1, NS // 2):
        kt_tile = kt_ref[jnp.maximum(kb0 // 2 + t, 0)]
        ktiles[t] = (kt_tile, pltpu.roll(kt_tile, SB, 1))

    def first_stage(m):
        krhs = key_rhs(*ktiles[m // 2], m % 2)
        if m == -1:
            return logits_stage(q_rows(0, 1), krhs, 0)
        if m == NS - 1:
            return logits_stage(q_rows(m, 1), krhs, SB)
        return logits_stage(q_rows(m, 2), krhs, SB)

    blocks = list(range(-1, NS))
    stage1, stage2, parts = {}, {}, []
    lag2, lag3 = ATT_SKEW[0], ATT_SKEW[0] + ATT_SKEW[1]
    for s in range(len(blocks) + lag3):
        if s < len(blocks):
            stage1[s] = first_stage(blocks[s])
        if 0 <= s - lag2 < len(blocks):
            stage2[s - lag2] = weights_stage(*stage1.pop(s - lag2))
        if 0 <= s - lag3 < len(blocks):
            a, rs = stage2.pop(s - lag3)
            vrhs = value_rhs(jnp.maximum(kb0 + blocks[s - lag3], 0))
            parts.append((output_stage(a, vrhs), rs))

    carry_max = None
    for r in range(NS):
        pv_p, rs_p = parts[r] if r == 0 else (parts[r][0][SB:], parts[r][1][SB:])
        pv_d, rs_d = parts[r + 1] if r == NS - 1 else (parts[r + 1][0][:SB], parts[r + 1][1][:SB])
        scale = jnp.exp(rs_d)
        if r == 0:
            scale = jnp.where(kb0 > 0, scale, 0.0)
        acc_ref[r] = pv_d + scale * pv_p
        carry_ref[r] = rs_d + rs_p
        carry_max = rs_d + rs_p if carry_max is None else jnp.maximum(carry_max, rs_d + rs_p)

    def sub_block(r, cmax0):
        qr = q_ref[pl.ds(pl.multiple_of(r * SB, SB), SB), :]

        def cond(st):
            j, cmax = st
            return jnp.logical_and(j >= 0, cmax > -STICK_CUTOFF)

        def body(st):
            j, _ = st
            kt_tile = kt_ref[lax.shift_right_logical(j, 1)]
            rolled = pltpu.roll(kt_tile, SB, 1)
            krhs = jnp.where((j & 1) == 0, key_rhs(kt_tile, rolled, 0), key_rhs(kt_tile, rolled, 1))
            pv, rs = tile(qr, krhs, value_rhs(j))
            carry = carry_ref[r]
            acc_ref[r] += jnp.exp(carry) * pv
            carry_ref[r] = carry + rs
            return j - 1, jnp.max(carry + rs)

        lax.while_loop(cond, body, (kb0 + r - 2, cmax0))

    G = ATT_SCREEN_GROUP
    n_groups = NS // G

    def group(g, _):
        @pl.when(gmax_ref[g] > -STICK_CUTOFF)
        def _():
            for k in range(G):
                cmax_ref[k] = jnp.max(carry_ref[g * G + k])

            def member(k, _):
                @pl.when(cmax_ref[k] > -STICK_CUTOFF)
                def _():
                    sub_block(g * G + k, cmax_ref[k])

                return 0

            lax.fori_loop(0, G, member, 0)

        return 0

    @pl.when(jnp.max(carry_max) > -STICK_CUTOFF)
    def _():
        for g in range(n_groups):
            gmax = carry_ref[g * G]
            for k in range(1, G):
                gmax = jnp.maximum(gmax, carry_ref[g * G + k])
            gmax_ref[g] = jnp.max(gmax)
        lax.fori_loop(0, n_groups, group, 0)

    for r in range(NS):
        o_ref[r * SB:(r + 1) * SB, :] = acc_ref[r].astype(o_ref.dtype)


def _stick_matrix():
    key = np.arange(LANES) % ATT_SUB
    head = np.arange(LANES) // ATT_SUB
    same = head[:, None] == head[None, :]
    suffix = np.logical_and(same, key[:, None] >= key[None, :])
    half = -np.concatenate([suffix, same], axis=1).astype(np.float32)
    return jnp.asarray(np.concatenate([half, half], axis=0), BF16)


def _attention(q, kt, v, *, batch, seq):
    N, W = q.shape
    R = ATT_ROWS
    q3, v3 = q.reshape(batch, seq, W), v.reshape(batch, seq, W)
    tri = _stick_matrix()
    resident = lambda b, hp, i: (b, 0, hp)
    out = pl.pallas_call(
        _attn_kernel,
        grid=(batch, W // LANES, seq // R),
        in_specs=[
            pl.BlockSpec((None, R, LANES), lambda b, hp, i: (b, i, hp)),
            pl.BlockSpec((None, None, seq // ATT_BLOCK, LANES, ATT_BLOCK), lambda b, hp, i: (b, hp, 0, 0, 0)),
            pl.BlockSpec((None, seq, LANES), resident),
            pl.BlockSpec(tri.shape, lambda b, hp, i: (0, 0)),
        ],
        out_specs=pl.BlockSpec((None, R, LANES), lambda b, hp, i: (b, i, hp)),
        out_shape=jax.ShapeDtypeStruct((batch, seq, W), BF16),
        scratch_shapes=[pltpu.VMEM((R // ATT_SUB, ATT_SUB, LANES), F32),
                        pltpu.VMEM((R // ATT_SUB, ATT_SUB, LANES), F32),
                        pltpu.SMEM((R // ATT_SUB // ATT_SCREEN_GROUP,), F32),
                        pltpu.SMEM((ATT_SCREEN_GROUP,), F32)],
        compiler_params=pltpu.CompilerParams(
            dimension_semantics=("arbitrary", "arbitrary", "arbitrary"),
            vmem_limit_bytes=VMEM_LIMIT),
        name="stickbreak_attn",
    )(q3, kt, v3, tri)
    return out.reshape(N, W)


def _mlp_kernel(x_ref, yab_ref, yc_ref, g2_ref, gf_ref, wo_hbm, wup_hbm, wdn_hbm, o_ref,
                wo_ref, wup_ref, wdn_ref, sem, *, layer, final_norm, ff_chunk):
    @pl.when(pl.program_id(0) == 0)
    def _():
        copies = [pltpu.make_async_copy(src, dst, sem.at[n])
                  for n, (src, dst) in enumerate(((wo_hbm.at[layer], wo_ref), (wup_hbm.at[layer], wup_ref),
                                                  (wdn_hbm.at[layer], wdn_ref)))]
        for c in copies:
            c.start()
        for c in copies:
            c.wait()

    nab = yab_ref.shape[1]
    T = x_ref.shape[0]
    rows = T // MLP_ROW_GROUPS
    x1s, hs = [], []
    for g in range(MLP_ROW_GROUPS):
        r = slice(g * rows, (g + 1) * rows)
        x1 = (x_ref[r, :]
              + jnp.dot(yab_ref[r, :], wo_ref[0:nab, :], preferred_element_type=F32)
              + jnp.dot(yc_ref[r, :], wo_ref[nab:, :], preferred_element_type=F32))
        x1s.append(x1)
        hs.append(_rms(x1, g2_ref[...]).astype(BF16))
    for g in range(MLP_ROW_GROUPS):
        acc = None
        for c in range(wup_ref.shape[1] // ff_chunk):
            up = jnp.dot(hs[g], wup_ref[:, c * ff_chunk:(c + 1) * ff_chunk], preferred_element_type=F32)
            act = jnp.square(jnp.maximum(up, 0.0)).astype(BF16)
            dn = jnp.dot(act, wdn_ref[c * ff_chunk:(c + 1) * ff_chunk, :], preferred_element_type=F32)
            acc = dn if acc is None else acc + dn
        acc = acc + x1s[g]
        if final_norm:
            acc = _rms(acc, gf_ref[...])
        o_ref[g * rows:(g + 1) * rows, :] = acc


def _mlp(x2, yab, yc, wo, g2, wup, wdn, gf, *, layer, final_norm):
    N, D = x2.shape
    T = ROW_TILE
    const = lambda i: (0, 0)
    row = lambda i: (i, 0)
    hbm = pl.BlockSpec(memory_space=pl.ANY)
    return pl.pallas_call(
        functools.partial(_mlp_kernel, layer=layer, final_norm=final_norm, ff_chunk=1024),
        grid=(N // T,),
        in_specs=[
            pl.BlockSpec((T, D), row),
            pl.BlockSpec((T, yab.shape[1]), row),
            pl.BlockSpec((T, yc.shape[1]), row),
            pl.BlockSpec((None, 1, D), lambda i: (layer, 0, 0)),
            pl.BlockSpec((1, D), const),
            hbm, hbm, hbm,
        ],
        out_specs=pl.BlockSpec((T, D), row),
        out_shape=jax.ShapeDtypeStruct((N, D), F32),
        scratch_shapes=[pltpu.VMEM(wo.shape[1:], BF16), pltpu.VMEM(wup.shape[1:], BF16),
                        pltpu.VMEM(wdn.shape[1:], BF16), pltpu.SemaphoreType.DMA((3,))],
        compiler_params=pltpu.CompilerParams(
            dimension_semantics=("arbitrary",), vmem_limit_bytes=VMEM_LIMIT),
        name="outproj_mlp",
    )(x2, yab, yc, g2, gf, wo, wup, wdn)


def kernel(x, norm1, w_in, pool_w, pool_scale, sg_norm, sg_w, sg_b, w_out, norm2, w_up, w_down, final_norm):
    B, S, D = x.shape
    depth = norm1.shape[0]
    n_grp, gw = pool_w.shape[1], pool_w.shape[2]
    pw, sw = pool_scale.shape[1], sg_norm.shape[1]
    sbw = (w_in.shape[2] - pw - 2 * sw) // 3
    assert S % ROW_TILE == 0 and S % ATT_ROWS == 0 and ROW_TILE % CHUNK == 0 and ROW_TILE % ATT_BLOCK == 0
    assert sg_w.shape[1] == SG_HEADS and sg_w.shape[2] == CHUNK and n_grp == len(POOL_WINDOWS)
    assert sbw % LANES == 0 and ATT_BLOCK == LANES == 2 * ATT_SUB == 2 * SB_HD and (ATT_ROWS // ATT_SUB) % max(2, ATT_SCREEN_GROUP) == 0

    wo_all, wup_all, wdn_all = w_out.astype(BF16), w_up.astype(BF16), w_down.astype(BF16)
    ck = pw + 2 * sw + sbw
    win_all = w_in.astype(BF16)
    wkt_all = jnp.swapaxes(w_in[:, :, ck:ck + sbw], 1, 2).astype(BF16)
    poolw_all = jnp.einsum('lgcd,gh->lgchd', pool_w, jnp.eye(n_grp, dtype=pool_w.dtype)).reshape(depth, pw, pw).astype(BF16)
    sgb_all = jnp.repeat(jnp.swapaxes(sg_b, 1, 2), sw // SG_HEADS, axis=2)
    g1_all, g2_all = norm1[:, None, :], norm2[:, None, :]
    pscale_all, sgn_all = pool_scale[:, None, :], sg_norm[:, None, :]
    x2 = x.reshape(B * S, D)
    for l in range(depth):
        yab, q, kt, v = _inproj(x2, g1_all, win_all, wkt_all, poolw_all, pscale_all, sgn_all, sg_w, sgb_all,
                                layer=l, batch=B, seq=S)
        yc = _attention(q, kt, v, batch=B, seq=S)
        x2 = _mlp(x2, yab, yc, wo_all, g2_all, wup_all, wdn_all, final_norm[None],
                  layer=l, final_norm=(l == depth - 1))
    return x2.reshape(B, S, D)
```

```python
import functools

import jax
import jax.numpy as jnp
import numpy as np
from jax import lax
from jax.experimental import pallas as pl
from jax.experimental.pallas import tpu as pltpu

EPS = 1e-6
POOL_WINDOWS = (2, 4, 8, 16)
POOL_HALO = 16
CHUNK = 128
SG_HEADS = 4
SB_HD = 64
LANES = 128

ROW_TILE = 1024
MLP_ROW_GROUPS = 2
ATT_BLOCK = 128
ATT_SUB = 64
ATT_ROWS = 8192
ATT_SCREEN_GROUP = 16
ATT_SKEW = (3, 4)
STICK_CUTOFF = 30.0
MASK_LOGIT = -1e30
LOGIT_CAP = 80.0
LOG2E = 1.4426950408889634
VMEM_LIMIT = 56 * 1024 * 1024

F32 = jnp.float32
BF16 = jnp.bfloat16


def _rms(x, g):
    ms = jnp.mean(x * x, axis=-1, keepdims=True)
    return x * lax.rsqrt(ms + EPS) * g


def _gelu_tanh(x):
    c = np.float32(np.sqrt(2.0 / np.pi))
    return 0.5 * x * (1.0 + jnp.tanh(c * (x + 0.044715 * (x * x * x))))


def _inproj_kernel(x_ref, g1_ref, win_ref, poolw_ref, pscale_ref, sgn_ref, sgw_ref, sgb_ref,
                   yab_ref, q_ref, kt_ref, v_ref, lvl_ref, *, tiles_per_seq):
    T = x_ref.shape[0]
    pw = pscale_ref.shape[1]
    sw = sgn_ref.shape[1]
    i = pl.program_id(0)
    tile_in_seq = i % tiles_per_seq

    H = POOL_HALO

    @pl.when(i == 0)
    def _():
        lvl_ref[:, 0:H, :] = jnp.zeros((lvl_ref.shape[0], H, pw), F32)

    @pl.when(tile_in_seq == 0)
    def _():
        lvl_ref[0, H:2 * H, :] = jnp.zeros((H, pw), F32)

    @pl.when(tile_in_seq != 0)
    def _():
        lvl_ref[0, H:2 * H, :] = lvl_ref[0, T + H:T + 2 * H, :]

    hs, bpres, a_s = [], [], []
    for r0 in range(0, T, T // 2):
        hg = _rms(x_ref[r0:r0 + T // 2, :], g1_ref[...]).astype(BF16)
        bpres.append(jnp.dot(hg, win_ref[:, pw:pw + 2 * sw], preferred_element_type=F32))
        a_s.append(jnp.dot(hg, win_ref[:, 0:pw], preferred_element_type=F32))
        hs.append(hg)
    h = jnp.concatenate(hs, axis=0)
    bpre = jnp.concatenate(bpres, axis=0)
    a = jnp.concatenate(a_s, axis=0)
    c2 = pw + 2 * sw
    sbw = q_ref.shape[1]
    half = sbw // 2

    def project(out_ref, col, out_col, scale=None):
        r = jnp.dot(h, win_ref[:, col:col + half], preferred_element_type=F32)
        out_ref[:, out_col:out_col + half] = (r if scale is None else r * scale).astype(BF16)

    r_i = lax.broadcasted_iota(jnp.int32, (CHUNK, CHUNK), 0)
    c_i = lax.broadcasted_iota(jnp.int32, (CHUNK, CHUNK), 1)
    tril = c_i <= r_i
    wm = jnp.concatenate([jnp.where(tril, sgw_ref[hh], 0.0).astype(BF16) for hh in range(SG_HEADS)], axis=1)
    head = lax.broadcasted_iota(jnp.int32, (1, sw), 1) // (sw // SG_HEADS)

    def gate_chunk(c):
        bz = _gelu_tanh(bpre[c * CHUNK:(c + 1) * CHUNK, :])
        vc = _rms(bz[:, sw:2 * sw], sgn_ref[...]).astype(BF16)
        vstack = jnp.concatenate([jnp.where(head == hh, vc, jnp.zeros_like(vc)) for hh in range(SG_HEADS)], axis=0)
        sv = jnp.dot(wm, vstack, preferred_element_type=F32) + sgb_ref[...]
        yab_ref[c * CHUNK:(c + 1) * CHUNK, pw:pw + sw] = (bz[:, 0:sw] * sv).astype(BF16)

    q_scale = float(SB_HD ** -0.5)
    pieces = [(q_ref, c2, 0, q_scale), (q_ref, c2 + half, half, q_scale),
              (v_ref, c2 + 2 * sbw, 0, None), (v_ref, c2 + 2 * sbw + half, half, None)]
    n_chunks = T // CHUNK
    per_piece = -(-n_chunks // len(pieces))
    for p, piece in enumerate(pieces):
        project(*piece)
        for c in range(p * per_piece, min((p + 1) * per_piece, n_chunks)):
            gate_chunk(c)

    lo, n = H, T + H
    lvl_ref[0, 2 * H:2 * H + T, :] = a
    for s, shift in enumerate((1, 2, 4, 8)):
        lvl_ref[s + 1, lo:lo + n, :] = (lvl_ref[s, lo:lo + n, :]
                                        + lvl_ref[s, lo - shift:lo - shift + n, :])
    lane = lax.broadcasted_iota(jnp.int32, (1, pw), 1)
    grp = lane // (pw // len(POOL_WINDOWS))
    win = jnp.where(grp == 0, 2, jnp.where(grp == 1, 4, jnp.where(grp == 2, 8, 16)))
    psum = jnp.where(grp == 0, lvl_ref[1, 2 * H:2 * H + T, :],
                     jnp.where(grp == 1, lvl_ref[2, 2 * H:2 * H + T, :],
                               jnp.where(grp == 2, lvl_ref[3, 2 * H:2 * H + T, :],
                                         lvl_ref[4, 2 * H:2 * H + T, :])))
    pos = tile_in_seq * T + lax.broadcasted_iota(jnp.int32, (T, 1), 0)
    cnt = jnp.minimum(pos + 1, win).astype(F32)
    d = psum / cnt - a
    ya = jnp.dot(d.astype(BF16), poolw_ref[...], preferred_element_type=F32) * pscale_ref[...]
    yab_ref[:, 0:pw] = ya.astype(BF16)

    kt = lax.dot_general(win_ref[:, c2 + sbw:c2 + 2 * sbw], h, (((0,), (1,)), ((), ())),
                         preferred_element_type=F32).astype(BF16)
    for hp in range(kt_ref.shape[0]):
        for c in range(kt_ref.shape[1]):
            kt_ref[hp, c] = kt[hp * LANES:(hp + 1) * LANES, c * ATT_BLOCK:(c + 1) * ATT_BLOCK]


def _inproj(x2, g1, win, poolw, pscale, sgn, sgw, sgb, *, layer, batch, seq):
    N, D = x2.shape
    T = ROW_TILE
    pw, sw = pscale.shape[2], sgn.shape[2]
    of_layer = lambda arr: pl.BlockSpec((None,) + arr.shape[1:], lambda i: (layer, 0, 0))
    sbw = (win.shape[2] - pw - 2 * sw) // 3
    tps = seq // T
    const = lambda i: (0, 0)
    row = lambda i: (i, 0)
    return pl.pallas_call(
        functools.partial(_inproj_kernel, tiles_per_seq=tps),
        grid=(N // T,),
        in_specs=[
            pl.BlockSpec((T, D), row),
            of_layer(g1),
            of_layer(win),
            of_layer(poolw),
            of_layer(pscale),
            of_layer(sgn),
            pl.BlockSpec((None,) + sgw.shape[1:], lambda i: (layer, 0, 0, 0)),
            of_layer(sgb),
        ],
        out_specs=[
            pl.BlockSpec((T, pw + sw), row),
            pl.BlockSpec((T, sbw), row),
            pl.BlockSpec((None, sbw // LANES, T // ATT_BLOCK, LANES, ATT_BLOCK),
                         lambda i: (i // tps, 0, i % tps, 0, 0)),
            pl.BlockSpec((T, sbw), row),
        ],
        out_shape=[
            jax.ShapeDtypeStruct((N, pw + sw), BF16),
            jax.ShapeDtypeStruct((N, sbw), BF16),
            jax.ShapeDtypeStruct((batch, sbw // LANES, seq // ATT_BLOCK, LANES, ATT_BLOCK), BF16),
            jax.ShapeDtypeStruct((N, sbw), BF16),
        ],
        scratch_shapes=[pltpu.VMEM((len(POOL_WINDOWS) + 1, T + 2 * POOL_HALO, pw), F32)],
        compiler_params=pltpu.CompilerParams(
            dimension_semantics=("arbitrary",), vmem_limit_bytes=VMEM_LIMIT),
        name="inproj_mixers",
    )(x2, g1, win, poolw, pscale, sgn, sgw, sgb)


def _attn_kernel(q_ref, kt_ref, v_ref, tri_ref, o_ref, acc_ref, carry_ref, gmax_ref, cmax_ref):
    SB = ATT_SUB
    NS = o_ref.shape[0] // SB
    i = pl.program_id(2)
    kb0 = i * NS
    head0_lane = lax.broadcasted_iota(jnp.int32, (1, LANES), 1) < SB_HD
    row = lax.broadcasted_iota(jnp.int32, (SB, LANES), 0)
    col = lax.broadcasted_iota(jnp.int32, (SB, LANES), 1) & (SB - 1)
    cap_diag = jnp.where(col < row, LOGIT_CAP, MASK_LOGIT)
    tri = tri_ref[...]

    def key_rhs(ktile, rolled, half):
        h0, h1 = (ktile, rolled) if half == 0 else (rolled, ktile)
        zero = jnp.zeros((SB_HD, LANES), ktile.dtype)
        return jnp.concatenate([jnp.where(head0_lane, h0[:SB_HD], zero),
                                jnp.where(head0_lane, zero, h1[SB_HD:])], axis=0)

    def value_rhs(kb):
        vrows = v_ref[pl.ds(pl.multiple_of(kb * SB, SB), SB), :]
        zero = jnp.zeros_like(vrows)
        return jnp.concatenate([jnp.where(head0_lane, vrows, zero),
                                jnp.where(head0_lane, zero, vrows)], axis=0)

    def logits_stage(qrows, krhs, diagonal_rows):
        z = jnp.dot(qrows, krhs, preferred_element_type=F32)
        if diagonal_rows == z.shape[0]:
            z = jnp.minimum(z, cap_diag)
        elif diagonal_rows:
            z = jnp.concatenate([jnp.minimum(z[:diagonal_rows], cap_diag),
                                 jnp.minimum(z[diagonal_rows:], LOGIT_CAP)], axis=0)
        else:
            z = jnp.minimum(z, LOGIT_CAP)
        neg_log_1m = jnp.log(1.0 + jnp.exp2(z * LOG2E))
        hi = neg_log_1m.astype(BF16)
        lo = (neg_log_1m - hi.astype(F32)).astype(BF16)
        return z, jnp.concatenate([hi, lo], axis=1)

    def weights_stage(z, hilo):
        ext = jnp.dot(hilo, tri, preferred_element_type=F32)
        return jnp.exp(z + ext[:, :LANES]).astype(BF16), ext[:, LANES:]

    def output_stage(a, vrhs):
        return jnp.dot(a, vrhs, preferred_element_type=F32)

    def tile(qrows, krhs, vrhs):
        a, rs = weights_stage(*logits_stage(qrows, krhs, 0))
        return output_stage(a, vrhs), rs

    def q_rows(first, count):
        return q_ref[first * SB:(first + count) * SB, :]

    ktiles = {}
    for t in range(-1, NS // 2):
        kt_tile = kt_ref[jnp.maximum(kb0 // 2 + t, 0)]
        ktiles[t] = (kt_tile, pltpu.roll(kt_tile, SB, 1))

    def first_stage(m):
        krhs = key_rhs(*ktiles[m // 2], m % 2)
        if m == -1:
            return logits_stage(q_rows(0, 1), krhs, 0)
        if m == NS - 1:
            return logits_stage(q_rows(m, 1), krhs, SB)
        return logits_stage(q_rows(m, 2), krhs, SB)

    blocks = list(range(-1, NS))
    stage1, stage2, parts = {}, {}, []
    lag2, lag3 = ATT_SKEW[0], ATT_SKEW[0] + ATT_SKEW[1]
    for s in range(len(blocks) + lag3):
        if s < len(blocks):
            stage1[s] = first_stage(blocks[s])
        if 0 <= s - lag2 < len(blocks):
            stage2[s - lag2] = weights_stage(*stage1.pop(s - lag2))
        if 0 <= s - lag3 < len(blocks):
            a, rs = stage2.pop(s - lag3)
            vrhs = value_rhs(jnp.maximum(kb0 + blocks[s - lag3], 0))
            parts.append((output_stage(a, vrhs), rs))

    carry_max = None
    for r in range(NS):
        pv_p, rs_p = parts[r] if r == 0 else (parts[r][0][SB:], parts[r][1][SB:])
        pv_d, rs_d = parts[r + 1] if r == NS - 1 else (parts[r + 1][0][:SB], parts[r + 1][1][:SB])
        scale = jnp.exp(rs_d)
        if r == 0:
            scale = jnp.where(kb0 > 0, scale, 0.0)
        acc_ref[r] = pv_d + scale * pv_p
        carry_ref[r] = rs_d + rs_p
        carry_max = rs_d + rs_p if carry_max is None else jnp.maximum(carry_max, rs_d + rs_p)

    def sub_block(r, cmax0):
        qr = q_ref[pl.ds(pl.multiple_of(r * SB, SB), SB), :]

        def cond(st):
            j, cmax = st
            return jnp.logical_and(j >= 0, cmax > -STICK_CUTOFF)

        def body(st):
            j, _ = st
            kt_tile = kt_ref[lax.shift_right_logical(j, 1)]
            rolled = pltpu.roll(kt_tile, SB, 1)
            krhs = jnp.where((j & 1) == 0, key_rhs(kt_tile, rolled, 0), key_rhs(kt_tile, rolled, 1))
            pv, rs = tile(qr, krhs, value_rhs(j))
            carry = carry_ref[r]
            acc_ref[r] += jnp.exp(carry) * pv
            carry_ref[r] = carry + rs
            return j - 1, jnp.max(carry + rs)

        lax.while_loop(cond, body, (kb0 + r - 2, cmax0))

    G = ATT_SCREEN_GROUP
    n_groups = NS // G

    def group(g, _):
        @pl.when(gmax_ref[g] > -STICK_CUTOFF)
        def _():
            for k in range(G):
                cmax_ref[k] = jnp.max(carry_ref[g * G + k])

            def member(k, _):
                @pl.when(cmax_ref[k] > -STICK_CUTOFF)
                def _():
                    sub_block(g * G + k, cmax_ref[k])

                return 0

            lax.fori_loop(0, G, member, 0)

        return 0

    @pl.when(jnp.max(carry_max) > -STICK_CUTOFF)
    def _():
        for g in range(n_groups):
            gmax = carry_ref[g * G]
            for k in range(1, G):
                gmax = jnp.maximum(gmax, carry_ref[g * G + k])
            gmax_ref[g] = jnp.max(gmax)
        lax.fori_loop(0, n_groups, group, 0)

    for r in range(NS):
        o_ref[r * SB:(r + 1) * SB, :] = acc_ref[r].astype(o_ref.dtype)


def _stick_matrix():
    key = np.arange(LANES) % ATT_SUB
    head = np.arange(LANES) // ATT_SUB
    same = head[:, None] == head[None, :]
    suffix = np.logical_and(same, key[:, None] >= key[None, :])
    half = -np.concatenate([suffix, same], axis=1).astype(np.float32)
    return jnp.asarray(np.concatenate([half, half], axis=0), BF16)


def _attention(q, kt, v, *, batch, seq):
    N, W = q.shape
    R = ATT_ROWS
    q3, v3 = q.reshape(batch, seq, W), v.reshape(batch, seq, W)
    tri = _stick_matrix()
    resident = lambda b, hp, i: (b, 0, hp)
    out = pl.pallas_call(
        _attn_kernel,
        grid=(batch, W // LANES, seq // R),
        in_specs=[
            pl.BlockSpec((None, R, LANES), lambda b, hp, i: (b, i, hp)),
            pl.BlockSpec((None, None, seq // ATT_BLOCK, LANES, ATT_BLOCK), lambda b, hp, i: (b, hp, 0, 0, 0)),
            pl.BlockSpec((None, seq, LANES), resident),
            pl.BlockSpec(tri.shape, lambda b, hp, i: (0, 0)),
        ],
        out_specs=pl.BlockSpec((None, R, LANES), lambda b, hp, i: (b, i, hp)),
        out_shape=jax.ShapeDtypeStruct((batch, seq, W), BF16),
        scratch_shapes=[pltpu.VMEM((R // ATT_SUB, ATT_SUB, LANES), F32),
                        pltpu.VMEM((R // ATT_SUB, ATT_SUB, LANES), F32),
                        pltpu.SMEM((R // ATT_SUB // ATT_SCREEN_GROUP,), F32),
                        pltpu.SMEM((ATT_SCREEN_GROUP,), F32)],
        compiler_params=pltpu.CompilerParams(
            dimension_semantics=("arbitrary", "arbitrary", "arbitrary"),
            vmem_limit_bytes=VMEM_LIMIT),
        name="stickbreak_attn",
    )(q3, kt, v3, tri)
    return out.reshape(N, W)


def _mlp_kernel(x_ref, yab_ref, yc_ref, g2_ref, gf_ref, wo_hbm, wup_hbm, wdn_hbm, o_ref,
                wo_ref, wup_ref, wdn_ref, sem, *, layer, final_norm, ff_chunk):
    @pl.when(pl.program_id(0) == 0)
    def _():
        copies = [pltpu.make_async_copy(src, dst, sem.at[n])
                  for n, (src, dst) in enumerate(((wo_hbm.at[layer], wo_ref), (wup_hbm.at[layer], wup_ref),
                                                  (wdn_hbm.at[layer], wdn_ref)))]
        for c in copies:
            c.start()
        for c in copies:
            c.wait()

    nab = yab_ref.shape[1]
    T = x_ref.shape[0]
    rows = T // MLP_ROW_GROUPS
    x1s, hs = [], []
    for g in range(MLP_ROW_GROUPS):
        r = slice(g * rows, (g + 1) * rows)
        x1 = (x_ref[r, :]
              + jnp.dot(yab_ref[r, :], wo_ref[0:nab, :], preferred_element_type=F32)
              + jnp.dot(yc_ref[r, :], wo_ref[nab:, :], preferred_element_type=F32))
        x1s.append(x1)
        hs.append(_rms(x1, g2_ref[...]).astype(BF16))
    for g in range(MLP_ROW_GROUPS):
        acc = None
        for c in range(wup_ref.shape[1] // ff_chunk):
            up = jnp.dot(hs[g], wup_ref[:, c * ff_chunk:(c + 1) * ff_chunk], preferred_element_type=F32)
            act = jnp.square(jnp.maximum(up, 0.0)).astype(BF16)
            dn = jnp.dot(act, wdn_ref[c * ff_chunk:(c + 1) * ff_chunk, :], preferred_element_type=F32)
            acc = dn if acc is None else acc + dn
        acc = acc + x1s[g]
        if final_norm:
            acc = _rms(acc, gf_ref[...])
        o_ref[g * rows:(g + 1) * rows, :] = acc


def _mlp(x2, yab, yc, wo, g2, wup, wdn, gf, *, layer, final_norm):
    N, D = x2.shape
    T = ROW_TILE
    const = lambda i: (0, 0)
    row = lambda i: (i, 0)
    hbm = pl.BlockSpec(memory_space=pl.ANY)
    return pl.pallas_call(
        functools.partial(_mlp_kernel, layer=layer, final_norm=final_norm, ff_chunk=1024),
        grid=(N // T,),
        in_specs=[
            pl.BlockSpec((T, D), row),
            pl.BlockSpec((T, yab.shape[1]), row),
            pl.BlockSpec((T, yc.shape[1]), row),
            pl.BlockSpec((None, 1, D), lambda i: (layer, 0, 0)),
            pl.BlockSpec((1, D), const),
            hbm, hbm, hbm,
        ],
        out_specs=pl.BlockSpec((T, D), row),
        out_shape=jax.ShapeDtypeStruct((N, D), F32),
        scratch_shapes=[pltpu.VMEM(wo.shape[1:], BF16), pltpu.VMEM(wup.shape[1:], BF16),
                        pltpu.VMEM(wdn.shape[1:], BF16), pltpu.SemaphoreType.DMA((3,))],
        compiler_params=pltpu.CompilerParams(
            dimension_semantics=("arbitrary",), vmem_limit_bytes=VMEM_LIMIT),
        name="outproj_mlp",
    )(x2, yab, yc, g2, gf, wo, wup, wdn)


def kernel(x, norm1, w_in, pool_w, pool_scale, sg_norm, sg_w, sg_b, w_out, norm2, w_up, w_down, final_norm):
    B, S, D = x.shape
    depth = norm1.shape[0]
    n_grp, gw = pool_w.shape[1], pool_w.shape[2]
    pw, sw = pool_scale.shape[1], sg_norm.shape[1]
    sbw = (w_in.shape[2] - pw - 2 * sw) // 3
    assert S % ROW_TILE == 0 and S % ATT_ROWS == 0 and ROW_TILE % CHUNK == 0 and ROW_TILE % ATT_BLOCK == 0
    assert sg_w.shape[1] == SG_HEADS and sg_w.shape[2] == CHUNK and n_grp == len(POOL_WINDOWS)
    assert sbw % LANES == 0 and ATT_BLOCK == LANES == 2 * ATT_SUB == 2 * SB_HD and (ATT_ROWS // ATT_SUB) % max(2, ATT_SCREEN_GROUP) == 0

    wo_all, wup_all, wdn_all = w_out.astype(BF16), w_up.astype(BF16), w_down.astype(BF16)
    win_all = w_in.astype(BF16)
    poolw_all = jnp.einsum('lgcd,gh->lgchd', pool_w, jnp.eye(n_grp, dtype=pool_w.dtype)).reshape(depth, pw, pw).astype(BF16)
    sgb_all = jnp.repeat(jnp.swapaxes(sg_b, 1, 2), sw // SG_HEADS, axis=2)
    g1_all, g2_all = norm1[:, None, :], norm2[:, None, :]
    pscale_all, sgn_all = pool_scale[:, None, :], sg_norm[:, None, :]
    x2 = x.reshape(B * S, D)
    for l in range(depth):
        yab, q, kt, v = _inproj(x2, g1_all, win_all, poolw_all, pscale_all, sgn_all, sg_w, sgb_all,
                                layer=l, batch=B, seq=S)
        yc = _attention(q, kt, v, batch=B, seq=S)
        x2 = _mlp(x2, yab, yc, wo_all, g2_all, wup_all, wdn_all, final_norm[None],
                  layer=l, final_norm=(l == depth - 1))
    return x2.reshape(B, S, D)
```

```python
import functools

import jax
import jax.numpy as jnp
import numpy as np
from jax import lax
from jax.experimental import pallas as pl
from jax.experimental.pallas import tpu as pltpu

EPS = 1e-6
POOL_WINDOWS = (2, 4, 8, 16)
POOL_HALO = 16
CHUNK = 128
SG_HEADS = 4
SB_HD = 64
LANES = 128

ROW_TILE = 1024
MLP_ROW_GROUPS = 2
MLP_STAGE_ROWS = 512
ATT_BLOCK = 128
ATT_SUB = 64
ATT_ROWS = 8192
ATT_SCREEN_GROUP = 16
ATT_SKEW = (3, 4)
STICK_CUTOFF = 30.0
MASK_LOGIT = -1e30
LOGIT_CAP = 80.0
LOG2E = 1.4426950408889634
VMEM_LIMIT = 56 * 1024 * 1024

F32 = jnp.float32
BF16 = jnp.bfloat16


def _rms(x, g):
    ms = jnp.mean(x * x, axis=-1, keepdims=True)
    return x * lax.rsqrt(ms + EPS) * g


def _gelu_tanh(x):
    c = np.float32(np.sqrt(2.0 / np.pi))
    return 0.5 * x * (1.0 + jnp.tanh(c * (x + 0.044715 * (x * x * x))))


def _inproj_kernel(x_ref, g1_ref, win_ref, poolw_ref, pscale_ref, sgn_ref, sgw_ref, sgb_ref,
                   yab_ref, q_ref, kt_ref, v_ref, lvl_ref, *, tiles_per_seq):
    T = x_ref.shape[0]
    pw = pscale_ref.shape[1]
    sw = sgn_ref.shape[1]
    i = pl.program_id(0)
    tile_in_seq = i % tiles_per_seq

    H = POOL_HALO

    @pl.when(i == 0)
    def _():
        lvl_ref[:, 0:H, :] = jnp.zeros((lvl_ref.shape[0], H, pw), F32)

    @pl.when(tile_in_seq == 0)
    def _():
        lvl_ref[0, H:2 * H, :] = jnp.zeros((H, pw), F32)

    @pl.when(tile_in_seq != 0)
    def _():
        lvl_ref[0, H:2 * H, :] = lvl_ref[0, T + H:T + 2 * H, :]

    hs, bpres, a_s = [], [], []
    for r0 in range(0, T, T // 2):
        hg = _rms(x_ref[r0:r0 + T // 2, :], g1_ref[...]).astype(BF16)
        bpres.append(jnp.dot(hg, win_ref[:, pw:pw + 2 * sw], preferred_element_type=F32))
        a_s.append(jnp.dot(hg, win_ref[:, 0:pw], preferred_element_type=F32))
        hs.append(hg)
    h = jnp.concatenate(hs, axis=0)
    bpre = jnp.concatenate(bpres, axis=0)
    a = jnp.concatenate(a_s, axis=0)
    c2 = pw + 2 * sw
    sbw = q_ref.shape[1]
    half = sbw // 2

    def project(out_ref, col, out_col, scale=None):
        r = jnp.dot(h, win_ref[:, col:col + half], preferred_element_type=F32)
        out_ref[:, out_col:out_col + half] = (r if scale is None else r * scale).astype(BF16)

    r_i = lax.broadcasted_iota(jnp.int32, (CHUNK, CHUNK), 0)
    c_i = lax.broadcasted_iota(jnp.int32, (CHUNK, CHUNK), 1)
    tril = c_i <= r_i
    wm = jnp.concatenate([jnp.where(tril, sgw_ref[hh], 0.0).astype(BF16) for hh in range(SG_HEADS)], axis=1)
    head = lax.broadcasted_iota(jnp.int32, (1, sw), 1) // (sw // SG_HEADS)

    def gate_chunk(c):
        bz = _gelu_tanh(bpre[c * CHUNK:(c + 1) * CHUNK, :])
        vc = _rms(bz[:, sw:2 * sw], sgn_ref[...]).astype(BF16)
        vstack = jnp.concatenate([jnp.where(head == hh, vc, jnp.zeros_like(vc)) for hh in range(SG_HEADS)], axis=0)
        sv = jnp.dot(wm, vstack, preferred_element_type=F32) + sgb_ref[...]
        yab_ref[c * CHUNK:(c + 1) * CHUNK, pw:pw + sw] = (bz[:, 0:sw] * sv).astype(BF16)

    q_scale = float(SB_HD ** -0.5)
    pieces = [(q_ref, c2, 0, q_scale), (q_ref, c2 + half, half, q_scale),
              (v_ref, c2 + 2 * sbw, 0, None), (v_ref, c2 + 2 * sbw + half, half, None)]
    n_chunks = T // CHUNK
    per_piece = -(-n_chunks // len(pieces))
    for p, piece in enumerate(pieces):
        project(*piece)
        for c in range(p * per_piece, min((p + 1) * per_piece, n_chunks)):
            gate_chunk(c)

    lo, n = H, T + H
    lvl_ref[0, 2 * H:2 * H + T, :] = a
    for s, shift in enumerate((1, 2, 4, 8)):
        lvl_ref[s + 1, lo:lo + n, :] = (lvl_ref[s, lo:lo + n, :]
                                        + lvl_ref[s, lo - shift:lo - shift + n, :])
    lane = lax.broadcasted_iota(jnp.int32, (1, pw), 1)
    grp = lane // (pw // len(POOL_WINDOWS))
    win = jnp.where(grp == 0, 2, jnp.where(grp == 1, 4, jnp.where(grp == 2, 8, 16)))
    psum = jnp.where(grp == 0, lvl_ref[1, 2 * H:2 * H + T, :],
                     jnp.where(grp == 1, lvl_ref[2, 2 * H:2 * H + T, :],
                               jnp.where(grp == 2, lvl_ref[3, 2 * H:2 * H + T, :],
                                         lvl_ref[4, 2 * H:2 * H + T, :])))
    pos = tile_in_seq * T + lax.broadcasted_iota(jnp.int32, (T, 1), 0)
    cnt = jnp.minimum(pos + 1, win).astype(F32)
    d = psum / cnt - a
    ya = jnp.dot(d.astype(BF16), poolw_ref[...], preferred_element_type=F32) * pscale_ref[...]
    yab_ref[:, 0:pw] = ya.astype(BF16)

    kt = lax.dot_general(win_ref[:, c2 + sbw:c2 + 2 * sbw], h, (((0,), (1,)), ((), ())),
                         preferred_element_type=F32).astype(BF16)
    for hp in range(kt_ref.shape[0]):
        for c in range(kt_ref.shape[1]):
            kt_ref[hp, c] = kt[hp * LANES:(hp + 1) * LANES, c * ATT_BLOCK:(c + 1) * ATT_BLOCK]


def _inproj(x2, g1, win, poolw, pscale, sgn, sgw, sgb, *, layer, batch, seq):
    N, D = x2.shape
    T = ROW_TILE
    pw, sw = pscale.shape[2], sgn.shape[2]
    of_layer = lambda arr: pl.BlockSpec((None,) + arr.shape[1:], lambda i: (layer, 0, 0))
    sbw = (win.shape[2] - pw - 2 * sw) // 3
    tps = seq // T
    const = lambda i: (0, 0)
    row = lambda i: (i, 0)
    return pl.pallas_call(
        functools.partial(_inproj_kernel, tiles_per_seq=tps),
        grid=(N // T,),
        in_specs=[
            pl.BlockSpec((T, D), row),
            of_layer(g1),
            of_layer(win),
            of_layer(poolw),
            of_layer(pscale),
            of_layer(sgn),
            pl.BlockSpec((None,) + sgw.shape[1:], lambda i: (layer, 0, 0, 0)),
            of_layer(sgb),
        ],
        out_specs=[
            pl.BlockSpec((T, pw + sw), row),
            pl.BlockSpec((T, sbw), row),
            pl.BlockSpec((None, sbw // LANES, T // ATT_BLOCK, LANES, ATT_BLOCK),
                         lambda i: (i // tps, 0, i % tps, 0, 0)),
            pl.BlockSpec((T, sbw), row),
        ],
        out_shape=[
            jax.ShapeDtypeStruct((N, pw + sw), BF16),
            jax.ShapeDtypeStruct((N, sbw), BF16),
            jax.ShapeDtypeStruct((batch, sbw // LANES, seq // ATT_BLOCK, LANES, ATT_BLOCK), BF16),
            jax.ShapeDtypeStruct((N, sbw), BF16),
        ],
        scratch_shapes=[pltpu.VMEM((len(POOL_WINDOWS) + 1, T + 2 * POOL_HALO, pw), F32)],
        compiler_params=pltpu.CompilerParams(
            dimension_semantics=("arbitrary",), vmem_limit_bytes=VMEM_LIMIT),
        name="inproj_mixers",
    )(x2, g1, win, poolw, pscale, sgn, sgw, sgb)


def _attn_kernel(q_ref, kt_ref, v_ref, tri_ref, o_ref, acc_ref, carry_ref, gmax_ref, cmax_ref):
    SB = ATT_SUB
    NS = o_ref.shape[0] // SB
    i = pl.program_id(2)
    kb0 = i * NS
    head0_lane = lax.broadcasted_iota(jnp.int32, (1, LANES), 1) < SB_HD
    row = lax.broadcasted_iota(jnp.int32, (SB, LANES), 0)
    col = lax.broadcasted_iota(jnp.int32, (SB, LANES), 1) & (SB - 1)
    cap_diag = jnp.where(col < row, LOGIT_CAP, MASK_LOGIT)
    tri = tri_ref[...]

    def key_rhs(ktile, rolled, half):
        h0, h1 = (ktile, rolled) if half == 0 else (rolled, ktile)
        zero = jnp.zeros((SB_HD, LANES), ktile.dtype)
        return jnp.concatenate([jnp.where(head0_lane, h0[:SB_HD], zero),
                                jnp.where(head0_lane, zero, h1[SB_HD:])], axis=0)

    def value_rhs(kb):
        vrows = v_ref[pl.ds(pl.multiple_of(kb * SB, SB), SB), :]
        zero = jnp.zeros_like(vrows)
        return jnp.concatenate([jnp.where(head0_lane, vrows, zero),
                                jnp.where(head0_lane, zero, vrows)], axis=0)

    def logits_stage(qrows, krhs, diagonal_rows):
        z = jnp.dot(qrows, krhs, preferred_element_type=F32)
        if diagonal_rows == z.shape[0]:
            z = jnp.minimum(z, cap_diag)
        elif diagonal_rows:
            z = jnp.concatenate([jnp.minimum(z[:diagonal_rows], cap_diag),
                                 jnp.minimum(z[diagonal_rows:], LOGIT_CAP)], axis=0)
        else:
            z = jnp.minimum(z, LOGIT_CAP)
        neg_log_1m = jnp.log(1.0 + jnp.exp2(z * LOG2E))
        hi = neg_log_1m.astype(BF16)
        lo = (neg_log_1m - hi.astype(F32)).astype(BF16)
        return z, jnp.concatenate([hi, lo], axis=1)

    def weights_stage(z, hilo):
        ext = jnp.dot(hilo, tri, preferred_element_type=F32)
        return jnp.exp(z + ext[:, :LANES]).astype(BF16), ext[:, LANES:]

    def output_stage(a, vrhs):
        return jnp.dot(a, vrhs, preferred_element_type=F32)

    def tile(qrows, krhs, vrhs):
        a, rs = weights_stage(*logits_stage(qrows, krhs, 0))
        return output_stage(a, vrhs), rs

    def q_rows(first, count):
        return q_ref[first * SB:(first + count) * SB, :]

    ktiles = {}
    for t in range(-1, NS // 2):
        kt_tile = kt_ref[jnp.maximum(kb0 // 2 + t, 0)]
        ktiles[t] = (kt_tile, pltpu.roll(kt_tile, SB, 1))

    def first_stage(m):
        krhs = key_rhs(*ktiles[m // 2], m % 2)
        if m == -1:
            return logits_stage(q_rows(0, 1), krhs, 0)
        if m == NS - 1:
            return logits_stage(q_rows(m, 1), krhs, SB)
        return logits_stage(q_rows(m, 2), krhs, SB)

    blocks = list(range(-1, NS))
    stage1, stage2, parts = {}, {}, []
    lag2, lag3 = ATT_SKEW[0], ATT_SKEW[0] + ATT_SKEW[1]
    for s in range(len(blocks) + lag3):
        if s < len(blocks):
            stage1[s] = first_stage(blocks[s])
        if 0 <= s - lag2 < len(blocks):
            stage2[s - lag2] = weights_stage(*stage1.pop(s - lag2))
        if 0 <= s - lag3 < len(blocks):
            a, rs = stage2.pop(s - lag3)
            vrhs = value_rhs(jnp.maximum(kb0 + blocks[s - lag3], 0))
            parts.append((output_stage(a, vrhs), rs))

    carry_max = None
    for r in range(NS):
        pv_p, rs_p = parts[r] if r == 0 else (parts[r][0][SB:], parts[r][1][SB:])
        pv_d, rs_d = parts[r + 1] if r == NS - 1 else (parts[r + 1][0][:SB], parts[r + 1][1][:SB])
        scale = jnp.exp(rs_d)
        if r == 0:
            scale = jnp.where(kb0 > 0, scale, 0.0)
        acc_ref[r] = pv_d + scale * pv_p
        carry_ref[r] = rs_d + rs_p
        carry_max = rs_d + rs_p if carry_max is None else jnp.maximum(carry_max, rs_d + rs_p)

    def sub_block(r, cmax0):
        qr = q_ref[pl.ds(pl.multiple_of(r * SB, SB), SB), :]

        def cond(st):
            j, cmax = st
            return jnp.logical_and(j >= 0, cmax > -STICK_CUTOFF)

        def body(st):
            j, _ = st
            kt_tile = kt_ref[lax.shift_right_logical(j, 1)]
            rolled = pltpu.roll(kt_tile, SB, 1)
            krhs = jnp.where((j & 1) == 0, key_rhs(kt_tile, rolled, 0), key_rhs(kt_tile, rolled, 1))
            pv, rs = tile(qr, krhs, value_rhs(j))
            carry = carry_ref[r]
            acc_ref[r] += jnp.exp(carry) * pv
            carry_ref[r] = carry + rs
            return j - 1, jnp.max(carry + rs)

        lax.while_loop(cond, body, (kb0 + r - 2, cmax0))

    G = ATT_SCREEN_GROUP
    n_groups = NS // G

    def group(g, _):
        @pl.when(gmax_ref[g] > -STICK_CUTOFF)
        def _():
            for k in range(G):
                cmax_ref[k] = jnp.max(carry_ref[g * G + k])

            def member(k, _):
                @pl.when(cmax_ref[k] > -STICK_CUTOFF)
                def _():
                    sub_block(g * G + k, cmax_ref[k])

                return 0

            lax.fori_loop(0, G, member, 0)

        return 0

    @pl.when(jnp.max(carry_max) > -STICK_CUTOFF)
    def _():
        for g in range(n_groups):
            gmax = carry_ref[g * G]
            for k in range(1, G):
                gmax = jnp.maximum(gmax, carry_ref[g * G + k])
            gmax_ref[g] = jnp.max(gmax)
        lax.fori_loop(0, n_groups, group, 0)

    for r in range(NS):
        o_ref[r * SB:(r + 1) * SB, :] = acc_ref[r].astype(o_ref.dtype)


def _stick_matrix():
    key = np.arange(LANES) % ATT_SUB
    head = np.arange(LANES) // ATT_SUB
    same = head[:, None] == head[None, :]
    suffix = np.logical_and(same, key[:, None] >= key[None, :])
    half = -np.concatenate([suffix, same], axis=1).astype(np.float32)
    return jnp.asarray(np.concatenate([half, half], axis=0), BF16)


def _attention(q, kt, v, *, batch, seq):
    N, W = q.shape
    R = ATT_ROWS
    q3, v3 = q.reshape(batch, seq, W), v.reshape(batch, seq, W)
    tri = _stick_matrix()
    resident = lambda b, hp, i: (b, 0, hp)
    out = pl.pallas_call(
        _attn_kernel,
        grid=(batch, W // LANES, seq // R),
        in_specs=[
            pl.BlockSpec((None, R, LANES), lambda b, hp, i: (b, i, hp)),
            pl.BlockSpec((None, None, seq // ATT_BLOCK, LANES, ATT_BLOCK), lambda b, hp, i: (b, hp, 0, 0, 0)),
            pl.BlockSpec((None, seq, LANES), resident),
            pl.BlockSpec(tri.shape, lambda b, hp, i: (0, 0)),
        ],
        out_specs=pl.BlockSpec((None, R, LANES), lambda b, hp, i: (b, i, hp)),
        out_shape=jax.ShapeDtypeStruct((batch, seq, W), BF16),
        scratch_shapes=[pltpu.VMEM((R // ATT_SUB, ATT_SUB, LANES), F32),
                        pltpu.VMEM((R // ATT_SUB, ATT_SUB, LANES), F32),
                        pltpu.SMEM((R // ATT_SUB // ATT_SCREEN_GROUP,), F32),
                        pltpu.SMEM((ATT_SCREEN_GROUP,), F32)],
        compiler_params=pltpu.CompilerParams(
            dimension_semantics=("arbitrary", "arbitrary", "arbitrary"),
            vmem_limit_bytes=VMEM_LIMIT),
        name="stickbreak_attn",
    )(q3, kt, v3, tri)
    return out.reshape(N, W)


def _mlp_kernel(x_ref, yab_ref, yc_ref, g2_ref, gf_ref, wo_hbm, wup_hbm, wdn_hbm, o_ref,
                wo_ref, wup_ref, wdn_ref, stage_ref, sem, *, layer, final_norm, ff_chunk):
    D = x_ref.shape[1]

    @pl.when(pl.program_id(0) == 0)
    def _():
        chunks = []
        for src, dst in ((wo_hbm, wo_ref), (wup_hbm, wup_ref), (wdn_hbm, wdn_ref)):
            for col in range(0, dst.shape[1], D):
                for r0 in range(0, dst.shape[0], MLP_STAGE_ROWS):
                    chunks.append((src.at[layer, pl.ds(r0, MLP_STAGE_ROWS), pl.ds(col, D)],
                                   dst.at[pl.ds(r0, MLP_STAGE_ROWS), pl.ds(col, D)]))
        copy = lambda n: pltpu.make_async_copy(chunks[n][0], stage_ref.at[n % 2], sem.at[n % 2])
        copy(0).start()
        for n in range(len(chunks)):
            if n + 1 < len(chunks):
                copy(n + 1).start()
            copy(n).wait()
            chunks[n][1][...] = stage_ref[n % 2].astype(BF16)

    nab = yab_ref.shape[1]
    T = x_ref.shape[0]
    rows = T // MLP_ROW_GROUPS
    x1s, hs = [], []
    for g in range(MLP_ROW_GROUPS):
        r = slice(g * rows, (g + 1) * rows)
        x1 = (x_ref[r, :]
              + jnp.dot(yab_ref[r, :], wo_ref[0:nab, :], preferred_element_type=F32)
              + jnp.dot(yc_ref[r, :], wo_ref[nab:, :], preferred_element_type=F32))
        x1s.append(x1)
        hs.append(_rms(x1, g2_ref[...]).astype(BF16))
    for g in range(MLP_ROW_GROUPS):
        acc = None
        for c in range(wup_ref.shape[1] // ff_chunk):
            up = jnp.dot(hs[g], wup_ref[:, c * ff_chunk:(c + 1) * ff_chunk], preferred_element_type=F32)
            act = jnp.square(jnp.maximum(up, 0.0)).astype(BF16)
            dn = jnp.dot(act, wdn_ref[c * ff_chunk:(c + 1) * ff_chunk, :], preferred_element_type=F32)
            acc = dn if acc is None else acc + dn
        acc = acc + x1s[g]
        if final_norm:
            acc = _rms(acc, gf_ref[...])
        o_ref[g * rows:(g + 1) * rows, :] = acc


def _mlp(x2, yab, yc, wo, g2, wup, wdn, gf, *, layer, final_norm):
    N, D = x2.shape
    T = ROW_TILE
    const = lambda i: (0, 0)
    row = lambda i: (i, 0)
    hbm = pl.BlockSpec(memory_space=pl.ANY)
    return pl.pallas_call(
        functools.partial(_mlp_kernel, layer=layer, final_norm=final_norm, ff_chunk=1024),
        grid=(N // T,),
        in_specs=[
            pl.BlockSpec((T, D), row),
            pl.BlockSpec((T, yab.shape[1]), row),
            pl.BlockSpec((T, yc.shape[1]), row),
            pl.BlockSpec((None, 1, D), lambda i: (layer, 0, 0)),
            pl.BlockSpec((1, D), const),
            hbm, hbm, hbm,
        ],
        out_specs=pl.BlockSpec((T, D), row),
        out_shape=jax.ShapeDtypeStruct((N, D), F32),
        scratch_shapes=[pltpu.VMEM(wo.shape[1:], BF16), pltpu.VMEM(wup.shape[1:], BF16),
                        pltpu.VMEM(wdn.shape[1:], BF16), pltpu.VMEM((2, MLP_STAGE_ROWS, D), F32),
                        pltpu.SemaphoreType.DMA((2,))],
        compiler_params=pltpu.CompilerParams(
            dimension_semantics=("arbitrary",), vmem_limit_bytes=VMEM_LIMIT),
        name="outproj_mlp",
    )(x2, yab, yc, g2, gf, wo, wup, wdn)


def kernel(x, norm1, w_in, pool_w, pool_scale, sg_norm, sg_w, sg_b, w_out, norm2, w_up, w_down, final_norm):
    B, S, D = x.shape
    depth = norm1.shape[0]
    n_grp, gw = pool_w.shape[1], pool_w.shape[2]
    pw, sw = pool_scale.shape[1], sg_norm.shape[1]
    sbw = (w_in.shape[2] - pw - 2 * sw) // 3
    assert S % ROW_TILE == 0 and S % ATT_ROWS == 0 and ROW_TILE % CHUNK == 0 and ROW_TILE % ATT_BLOCK == 0
    assert sg_w.shape[1] == SG_HEADS and sg_w.shape[2] == CHUNK and n_grp == len(POOL_WINDOWS)
    assert sbw % LANES == 0 and ATT_BLOCK == LANES == 2 * ATT_SUB == 2 * SB_HD and (ATT_ROWS // ATT_SUB) % max(2, ATT_SCREEN_GROUP) == 0

    win_all = w_in.astype(BF16)
    poolw_all = jnp.einsum('lgcd,gh->lgchd', pool_w, jnp.eye(n_grp, dtype=pool_w.dtype)).reshape(depth, pw, pw).astype(BF16)
    sgb_all = jnp.repeat(jnp.swapaxes(sg_b, 1, 2), sw // SG_HEADS, axis=2)
    g1_all, g2_all = norm1[:, None, :], norm2[:, None, :]
    pscale_all, sgn_all = pool_scale[:, None, :], sg_norm[:, None, :]
    x2 = x.reshape(B * S, D)
    for l in range(depth):
        yab, q, kt, v = _inproj(x2, g1_all, win_all, poolw_all, pscale_all, sgn_all, sg_w, sgb_all,
                                layer=l, batch=B, seq=S)
        yc = _attention(q, kt, v, batch=B, seq=S)
        x2 = _mlp(x2, yab, yc, w_out, g2_all, w_up, w_down, final_norm[None],
                  layer=l, final_norm=(l == depth - 1))
    return x2.reshape(B, S, D)
```

```python
import functools

import jax
import jax.numpy as jnp
import numpy as np
from jax import lax
from jax.experimental import pallas as pl
from jax.experimental.pallas import tpu as pltpu

EPS = 1e-6
POOL_WINDOWS = (2, 4, 8, 16)
POOL_HALO = 16
CHUNK = 128
SG_HEADS = 4
SB_HD = 64
LANES = 128

ROW_TILE = 1024
MLP_ROW_GROUPS = 2
MLP_STAGE_ROWS = 512
INPROJ_STAGE_ROWS = 256
ATT_BLOCK = 128
ATT_SUB = 64
ATT_ROWS = 8192
ATT_SCREEN_GROUP = 16
ATT_SKEW = (3, 4)
STICK_CUTOFF = 30.0
MASK_LOGIT = -1e30
LOGIT_CAP = 80.0
LOG2E = 1.4426950408889634
VMEM_LIMIT = 56 * 1024 * 1024

F32 = jnp.float32
BF16 = jnp.bfloat16


def _rms(x, g):
    ms = jnp.mean(x * x, axis=-1, keepdims=True)
    return x * lax.rsqrt(ms + EPS) * g


def _gelu_tanh(x):
    c = np.float32(np.sqrt(2.0 / np.pi))
    return 0.5 * x * (1.0 + jnp.tanh(c * (x + 0.044715 * (x * x * x))))


def _inproj_kernel(x_ref, g1_ref, win_hbm, poolw_ref, pscale_ref, sgn_ref, sgw_ref, sgb_ref,
                   yab_ref, q_ref, kt_ref, v_ref, lvl_ref, win_ref, stage_ref, sem, *, layer, tiles_per_seq):
    T = x_ref.shape[0]
    pw = pscale_ref.shape[1]
    sw = sgn_ref.shape[1]
    i = pl.program_id(0)
    tile_in_seq = i % tiles_per_seq

    H = POOL_HALO

    @pl.when(i == 0)
    def _():
        lvl_ref[:, 0:H, :] = jnp.zeros((lvl_ref.shape[0], H, pw), F32)
        rows = stage_ref.shape[1]
        n_chunks = win_ref.shape[0] // rows
        copy = lambda n: pltpu.make_async_copy(win_hbm.at[layer, pl.ds(n * rows, rows), :],
                                               stage_ref.at[n % 2], sem.at[n % 2])
        copy(0).start()
        for n in range(n_chunks):
            if n + 1 < n_chunks:
                copy(n + 1).start()
            copy(n).wait()
            win_ref[n * rows:(n + 1) * rows, :] = stage_ref[n % 2].astype(BF16)

    @pl.when(tile_in_seq == 0)
    def _():
        lvl_ref[0, H:2 * H, :] = jnp.zeros((H, pw), F32)

    @pl.when(tile_in_seq != 0)
    def _():
        lvl_ref[0, H:2 * H, :] = lvl_ref[0, T + H:T + 2 * H, :]

    hs, bpres, a_s = [], [], []
    for r0 in range(0, T, T // 2):
        hg = _rms(x_ref[r0:r0 + T // 2, :], g1_ref[...]).astype(BF16)
        bpres.append(jnp.dot(hg, win_ref[:, pw:pw + 2 * sw], preferred_element_type=F32))
        a_s.append(jnp.dot(hg, win_ref[:, 0:pw], preferred_element_type=F32))
        hs.append(hg)
    h = jnp.concatenate(hs, axis=0)
    bpre = jnp.concatenate(bpres, axis=0)
    a = jnp.concatenate(a_s, axis=0)
    c2 = pw + 2 * sw
    sbw = q_ref.shape[1]
    half = sbw // 2

    def project(out_ref, col, out_col, scale=None):
        r = jnp.dot(h, win_ref[:, col:col + half], preferred_element_type=F32)
        out_ref[:, out_col:out_col + half] = (r if scale is None else r * scale).astype(BF16)

    r_i = lax.broadcasted_iota(jnp.int32, (CHUNK, CHUNK), 0)
    c_i = lax.broadcasted_iota(jnp.int32, (CHUNK, CHUNK), 1)
    tril = c_i <= r_i
    wm = jnp.concatenate([jnp.where(tril, sgw_ref[hh], 0.0).astype(BF16) for hh in range(SG_HEADS)], axis=1)
    head = lax.broadcasted_iota(jnp.int32, (1, sw), 1) // (sw // SG_HEADS)

    def gate_chunk(c):
        bz = _gelu_tanh(bpre[c * CHUNK:(c + 1) * CHUNK, :])
        vc = _rms(bz[:, sw:2 * sw], sgn_ref[...]).astype(BF16)
        vstack = jnp.concatenate([jnp.where(head == hh, vc, jnp.zeros_like(vc)) for hh in range(SG_HEADS)], axis=0)
        sv = jnp.dot(wm, vstack, preferred_element_type=F32) + sgb_ref[...]
        yab_ref[c * CHUNK:(c + 1) * CHUNK, pw:pw + sw] = (bz[:, 0:sw] * sv).astype(BF16)

    q_scale = float(SB_HD ** -0.5)
    pieces = [(q_ref, c2, 0, q_scale), (q_ref, c2 + half, half, q_scale),
              (v_ref, c2 + 2 * sbw, 0, None), (v_ref, c2 + 2 * sbw + half, half, None)]
    n_chunks = T // CHUNK
    per_piece = -(-n_chunks // len(pieces))
    for p, piece in enumerate(pieces):
        project(*piece)
        for c in range(p * per_piece, min((p + 1) * per_piece, n_chunks)):
            gate_chunk(c)

    lo, n = H, T + H
    lvl_ref[0, 2 * H:2 * H + T, :] = a
    for s, shift in enumerate((1, 2, 4, 8)):
        lvl_ref[s + 1, lo:lo + n, :] = (lvl_ref[s, lo:lo + n, :]
                                        + lvl_ref[s, lo - shift:lo - shift + n, :])
    lane = lax.broadcasted_iota(jnp.int32, (1, pw), 1)
    grp = lane // (pw // len(POOL_WINDOWS))
    win = jnp.where(grp == 0, 2, jnp.where(grp == 1, 4, jnp.where(grp == 2, 8, 16)))
    psum = jnp.where(grp == 0, lvl_ref[1, 2 * H:2 * H + T, :],
                     jnp.where(grp == 1, lvl_ref[2, 2 * H:2 * H + T, :],
                               jnp.where(grp == 2, lvl_ref[3, 2 * H:2 * H + T, :],
                                         lvl_ref[4, 2 * H:2 * H + T, :])))
    pos = tile_in_seq * T + lax.broadcasted_iota(jnp.int32, (T, 1), 0)
    cnt = jnp.minimum(pos + 1, win).astype(F32)
    d = psum / cnt - a
    ya = jnp.dot(d.astype(BF16), poolw_ref[...], preferred_element_type=F32) * pscale_ref[...]
    yab_ref[:, 0:pw] = ya.astype(BF16)

    kt = lax.dot_general(win_ref[:, c2 + sbw:c2 + 2 * sbw], h, (((0,), (1,)), ((), ())),
                         preferred_element_type=F32).astype(BF16)
    for hp in range(kt_ref.shape[0]):
        for c in range(kt_ref.shape[1]):
            kt_ref[hp, c] = kt[hp * LANES:(hp + 1) * LANES, c * ATT_BLOCK:(c + 1) * ATT_BLOCK]


def _inproj(x2, g1, win, poolw, pscale, sgn, sgw, sgb, *, layer, batch, seq):
    N, D = x2.shape
    T = ROW_TILE
    pw, sw = pscale.shape[2], sgn.shape[2]
    of_layer = lambda arr: pl.BlockSpec((None,) + arr.shape[1:], lambda i: (layer, 0, 0))
    sbw = (win.shape[2] - pw - 2 * sw) // 3
    tps = seq // T
    const = lambda i: (0, 0)
    row = lambda i: (i, 0)
    return pl.pallas_call(
        functools.partial(_inproj_kernel, layer=layer, tiles_per_seq=tps),
        grid=(N // T,),
        in_specs=[
            pl.BlockSpec((T, D), row),
            of_layer(g1),
            pl.BlockSpec(memory_space=pl.ANY),
            of_layer(poolw),
            of_layer(pscale),
            of_layer(sgn),
            pl.BlockSpec((None,) + sgw.shape[1:], lambda i: (layer, 0, 0, 0)),
            of_layer(sgb),
        ],
        out_specs=[
            pl.BlockSpec((T, pw + sw), row),
            pl.BlockSpec((T, sbw), row),
            pl.BlockSpec((None, sbw // LANES, T // ATT_BLOCK, LANES, ATT_BLOCK),
                         lambda i: (i // tps, 0, i % tps, 0, 0)),
            pl.BlockSpec((T, sbw), row),
        ],
        out_shape=[
            jax.ShapeDtypeStruct((N, pw + sw), BF16),
            jax.ShapeDtypeStruct((N, sbw), BF16),
            jax.ShapeDtypeStruct((batch, sbw // LANES, seq // ATT_BLOCK, LANES, ATT_BLOCK), BF16),
            jax.ShapeDtypeStruct((N, sbw), BF16),
        ],
        scratch_shapes=[pltpu.VMEM((len(POOL_WINDOWS) + 1, T + 2 * POOL_HALO, pw), F32),
                        pltpu.VMEM(win.shape[1:], BF16), pltpu.VMEM((2, INPROJ_STAGE_ROWS, win.shape[2]), F32),
                        pltpu.SemaphoreType.DMA((2,))],
        compiler_params=pltpu.CompilerParams(
            dimension_semantics=("arbitrary",), vmem_limit_bytes=VMEM_LIMIT),
        name="inproj_mixers",
    )(x2, g1, win, poolw, pscale, sgn, sgw, sgb)


def _attn_kernel(q_ref, kt_ref, v_ref, tri_ref, o_ref, acc_ref, carry_ref, gmax_ref, cmax_ref):
    SB = ATT_SUB
    NS = o_ref.shape[0] // SB
    i = pl.program_id(2)
    kb0 = i * NS
    head0_lane = lax.broadcasted_iota(jnp.int32, (1, LANES), 1) < SB_HD
    row = lax.broadcasted_iota(jnp.int32, (SB, LANES), 0)
    col = lax.broadcasted_iota(jnp.int32, (SB, LANES), 1) & (SB - 1)
    cap_diag = jnp.where(col < row, LOGIT_CAP, MASK_LOGIT)
    tri = tri_ref[...]

    def key_rhs(ktile, rolled, half):
        h0, h1 = (ktile, rolled) if half == 0 else (rolled, ktile)
        zero = jnp.zeros((SB_HD, LANES), ktile.dtype)
        return jnp.concatenate([jnp.where(head0_lane, h0[:SB_HD], zero),
                                jnp.where(head0_lane, zero, h1[SB_HD:])], axis=0)

    def value_rhs(kb):
        vrows = v_ref[pl.ds(pl.multiple_of(kb * SB, SB), SB), :]
        zero = jnp.zeros_like(vrows)
        return jnp.concatenate([jnp.where(head0_lane, vrows, zero),
                                jnp.where(head0_lane, zero, vrows)], axis=0)

    def logits_stage(qrows, krhs, diagonal_rows):
        z = jnp.dot(qrows, krhs, preferred_element_type=F32)
        if diagonal_rows == z.shape[0]:
            z = jnp.minimum(z, cap_diag)
        elif diagonal_rows:
            z = jnp.concatenate([jnp.minimum(z[:diagonal_rows], cap_diag),
                                 jnp.minimum(z[diagonal_rows:], LOGIT_CAP)], axis=0)
        else:
            z = jnp.minimum(z, LOGIT_CAP)
        neg_log_1m = jnp.log(1.0 + jnp.exp2(z * LOG2E))
        hi = neg_log_1m.astype(BF16)
        lo = (neg_log_1m - hi.astype(F32)).astype(BF16)
        return z, jnp.concatenate([hi, lo], axis=1)

    def weights_stage(z, hilo):
        ext = jnp.dot(hilo, tri, preferred_element_type=F32)
        return jnp.exp(z + ext[:, :LANES]).astype(BF16), ext[:, LANES:]

    def output_stage(a, vrhs):
        return jnp.dot(a, vrhs, preferred_element_type=F32)

    def tile(qrows, krhs, vrhs):
        a, rs = weights_stage(*logits_stage(qrows, krhs, 0))
        return output_stage(a, vrhs), rs

    def q_rows(first, count):
        return q_ref[first * SB:(first + count) * SB, :]

    ktiles = {}
    for t in range(-1, NS // 2):
        kt_tile = kt_ref[jnp.maximum(kb0 // 2 + t, 0)]
        ktiles[t] = (kt_tile, pltpu.roll(kt_tile, SB, 1))

    def first_stage(m):
        krhs = key_rhs(*ktiles[m // 2], m % 2)
        if m == -1:
            return logits_stage(q_rows(0, 1), krhs, 0)
        if m == NS - 1:
            return logits_stage(q_rows(m, 1), krhs, SB)
        return logits_stage(q_rows(m, 2), krhs, SB)

    blocks = list(range(-1, NS))
    stage1, stage2, parts = {}, {}, []
    lag2, lag3 = ATT_SKEW[0], ATT_SKEW[0] + ATT_SKEW[1]
    for s in range(len(blocks) + lag3):
        if s < len(blocks):
            stage1[s] = first_stage(blocks[s])
        if 0 <= s - lag2 < len(blocks):
            stage2[s - lag2] = weights_stage(*stage1.pop(s - lag2))
        if 0 <= s - lag3 < len(blocks):
            a, rs = stage2.pop(s - lag3)
            vrhs = value_rhs(jnp.maximum(kb0 + blocks[s - lag3], 0))
            parts.append((output_stage(a, vrhs), rs))

    carry_max = None
    for r in range(NS):
        pv_p, rs_p = parts[r] if r == 0 else (parts[r][0][SB:], parts[r][1][SB:])
        pv_d, rs_d = parts[r + 1] if r == NS - 1 else (parts[r + 1][0][:SB], parts[r + 1][1][:SB])
        scale = jnp.exp(rs_d)
        if r == 0:
            scale = jnp.where(kb0 > 0, scale, 0.0)
        acc_ref[r] = pv_d + scale * pv_p
        carry_ref[r] = rs_d + rs_p
        carry_max = rs_d + rs_p if carry_max is None else jnp.maximum(carry_max, rs_d + rs_p)

    def sub_block(r, cmax0):
        qr = q_ref[pl.ds(pl.multiple_of(r * SB, SB), SB), :]

        def cond(st):
            j, cmax = st
            return jnp.logical_and(j >= 0, cmax > -STICK_CUTOFF)

        def body(st):
            j, _ = st
            kt_tile = kt_ref[lax.shift_right_logical(j, 1)]
            rolled = pltpu.roll(kt_tile, SB, 1)
            krhs = jnp.where((j & 1) == 0, key_rhs(kt_tile, rolled, 0), key_rhs(kt_tile, rolled, 1))
            pv, rs = tile(qr, krhs, value_rhs(j))
            carry = carry_ref[r]
            acc_ref[r] += jnp.exp(carry) * pv
            carry_ref[r] = carry + rs
            return j - 1, jnp.max(carry + rs)

        lax.while_loop(cond, body, (kb0 + r - 2, cmax0))

    G = ATT_SCREEN_GROUP
    n_groups = NS // G

    def group(g, _):
        @pl.when(gmax_ref[g] > -STICK_CUTOFF)
        def _():
            for k in range(G):
                cmax_ref[k] = jnp.max(carry_ref[g * G + k])

            def member(k, _):
                @pl.when(cmax_ref[k] > -STICK_CUTOFF)
                def _():
                    sub_block(g * G + k, cmax_ref[k])

                return 0

            lax.fori_loop(0, G, member, 0)

        return 0

    @pl.when(jnp.max(carry_max) > -STICK_CUTOFF)
    def _():
        for g in range(n_groups):
            gmax = carry_ref[g * G]
            for k in range(1, G):
                gmax = jnp.maximum(gmax, carry_ref[g * G + k])
            gmax_ref[g] = jnp.max(gmax)
        lax.fori_loop(0, n_groups, group, 0)

    for r in range(NS):
        o_ref[r * SB:(r + 1) * SB, :] = acc_ref[r].astype(o_ref.dtype)


def _stick_matrix():
    key = np.arange(LANES) % ATT_SUB
    head = np.arange(LANES) // ATT_SUB
    same = head[:, None] == head[None, :]
    suffix = np.logical_and(same, key[:, None] >= key[None, :])
    half = -np.concatenate([suffix, same], axis=1).astype(np.float32)
    return jnp.asarray(np.concatenate([half, half], axis=0), BF16)


def _attention(q, kt, v, *, batch, seq):
    N, W = q.shape
    R = ATT_ROWS
    q3, v3 = q.reshape(batch, seq, W), v.reshape(batch, seq, W)
    tri = _stick_matrix()
    resident = lambda b, hp, i: (b, 0, hp)
    out = pl.pallas_call(
        _attn_kernel,
        grid=(batch, W // LANES, seq // R),
        in_specs=[
            pl.BlockSpec((None, R, LANES), lambda b, hp, i: (b, i, hp)),
            pl.BlockSpec((None, None, seq // ATT_BLOCK, LANES, ATT_BLOCK), lambda b, hp, i: (b, hp, 0, 0, 0)),
            pl.BlockSpec((None, seq, LANES), resident),
            pl.BlockSpec(tri.shape, lambda b, hp, i: (0, 0)),
        ],
        out_specs=pl.BlockSpec((None, R, LANES), lambda b, hp, i: (b, i, hp)),
        out_shape=jax.ShapeDtypeStruct((batch, seq, W), BF16),
        scratch_shapes=[pltpu.VMEM((R // ATT_SUB, ATT_SUB, LANES), F32),
                        pltpu.VMEM((R // ATT_SUB, ATT_SUB, LANES), F32),
                        pltpu.SMEM((R // ATT_SUB // ATT_SCREEN_GROUP,), F32),
                        pltpu.SMEM((ATT_SCREEN_GROUP,), F32)],
        compiler_params=pltpu.CompilerParams(
            dimension_semantics=("arbitrary", "arbitrary", "arbitrary"),
            vmem_limit_bytes=VMEM_LIMIT),
        name="stickbreak_attn",
    )(q3, kt, v3, tri)
    return out.reshape(N, W)


def _mlp_kernel(x_ref, yab_ref, yc_ref, g2_ref, gf_ref, wo_hbm, wup_hbm, wdn_hbm, o_ref,
                wo_ref, wup_ref, wdn_ref, stage_ref, sem, *, layer, final_norm, ff_chunk):
    D = x_ref.shape[1]

    @pl.when(pl.program_id(0) == 0)
    def _():
        chunks = []
        for src, dst in ((wo_hbm, wo_ref), (wup_hbm, wup_ref), (wdn_hbm, wdn_ref)):
            for col in range(0, dst.shape[1], D):
                for r0 in range(0, dst.shape[0], MLP_STAGE_ROWS):
                    chunks.append((src.at[layer, pl.ds(r0, MLP_STAGE_ROWS), pl.ds(col, D)],
                                   dst.at[pl.ds(r0, MLP_STAGE_ROWS), pl.ds(col, D)]))
        copy = lambda n: pltpu.make_async_copy(chunks[n][0], stage_ref.at[n % 2], sem.at[n % 2])
        copy(0).start()
        for n in range(len(chunks)):
            if n + 1 < len(chunks):
                copy(n + 1).start()
            copy(n).wait()
            chunks[n][1][...] = stage_ref[n % 2].astype(BF16)

    nab = yab_ref.shape[1]
    T = x_ref.shape[0]
    rows = T // MLP_ROW_GROUPS
    x1s, hs = [], []
    for g in range(MLP_ROW_GROUPS):
        r = slice(g * rows, (g + 1) * rows)
        x1 = (x_ref[r, :]
              + jnp.dot(yab_ref[r, :], wo_ref[0:nab, :], preferred_element_type=F32)
              + jnp.dot(yc_ref[r, :], wo_ref[nab:, :], preferred_element_type=F32))
        x1s.append(x1)
        hs.append(_rms(x1, g2_ref[...]).astype(BF16))
    for g in range(MLP_ROW_GROUPS):
        acc = None
        for c in range(wup_ref.shape[1] // ff_chunk):
            up = jnp.dot(hs[g], wup_ref[:, c * ff_chunk:(c + 1) * ff_chunk], preferred_element_type=F32)
            act = jnp.square(jnp.maximum(up, 0.0)).astype(BF16)
            dn = jnp.dot(act, wdn_ref[c * ff_chunk:(c + 1) * ff_chunk, :], preferred_element_type=F32)
            acc = dn if acc is None else acc + dn
        acc = acc + x1s[g]
        if final_norm:
            acc = _rms(acc, gf_ref[...])
        o_ref[g * rows:(g + 1) * rows, :] = acc


def _mlp(x2, yab, yc, wo, g2, wup, wdn, gf, *, layer, final_norm):
    N, D = x2.shape
    T = ROW_TILE
    const = lambda i: (0, 0)
    row = lambda i: (i, 0)
    hbm = pl.BlockSpec(memory_space=pl.ANY)
    return pl.pallas_call(
        functools.partial(_mlp_kernel, layer=layer, final_norm=final_norm, ff_chunk=1024),
        grid=(N // T,),
        in_specs=[
            pl.BlockSpec((T, D), row),
            pl.BlockSpec((T, yab.shape[1]), row),
            pl.BlockSpec((T, yc.shape[1]), row),
            pl.BlockSpec((None, 1, D), lambda i: (layer, 0, 0)),
            pl.BlockSpec((1, D), const),
            hbm, hbm, hbm,
        ],
        out_specs=pl.BlockSpec((T, D), row),
        out_shape=jax.ShapeDtypeStruct((N, D), F32),
        scratch_shapes=[pltpu.VMEM(wo.shape[1:], BF16), pltpu.VMEM(wup.shape[1:], BF16),
                        pltpu.VMEM(wdn.shape[1:], BF16), pltpu.VMEM((2, MLP_STAGE_ROWS, D), F32),
                        pltpu.SemaphoreType.DMA((2,))],
        compiler_params=pltpu.CompilerParams(
            dimension_semantics=("arbitrary",), vmem_limit_bytes=VMEM_LIMIT),
        name="outproj_mlp",
    )(x2, yab, yc, g2, gf, wo, wup, wdn)


def kernel(x, norm1, w_in, pool_w, pool_scale, sg_norm, sg_w, sg_b, w_out, norm2, w_up, w_down, final_norm):
    B, S, D = x.shape
    depth = norm1.shape[0]
    n_grp, gw = pool_w.shape[1], pool_w.shape[2]
    pw, sw = pool_scale.shape[1], sg_norm.shape[1]
    sbw = (w_in.shape[2] - pw - 2 * sw) // 3
    assert S % ROW_TILE == 0 and S % ATT_ROWS == 0 and ROW_TILE % CHUNK == 0 and ROW_TILE % ATT_BLOCK == 0
    assert sg_w.shape[1] == SG_HEADS and sg_w.shape[2] == CHUNK and n_grp == len(POOL_WINDOWS)
    assert sbw % LANES == 0 and ATT_BLOCK == LANES == 2 * ATT_SUB == 2 * SB_HD and (ATT_ROWS // ATT_SUB) % max(2, ATT_SCREEN_GROUP) == 0

    poolw_all = jnp.einsum('lgcd,gh->lgchd', pool_w, jnp.eye(n_grp, dtype=pool_w.dtype)).reshape(depth, pw, pw).astype(BF16)
    sgb_all = jnp.repeat(jnp.swapaxes(sg_b, 1, 2), sw // SG_HEADS, axis=2)
    g1_all, g2_all = norm1[:, None, :], norm2[:, None, :]
    pscale_all, sgn_all = pool_scale[:, None, :], sg_norm[:, None, :]
    x2 = x.reshape(B * S, D)
    for l in range(depth):
        yab, q, kt, v = _inproj(x2, g1_all, w_in, poolw_all, pscale_all, sgn_all, sg_w, sgb_all,
                                layer=l, batch=B, seq=S)
        yc = _attention(q, kt, v, batch=B, seq=S)
        x2 = _mlp(x2, yab, yc, w_out, g2_all, w_up, w_down, final_norm[None],
                  layer=l, final_norm=(l == depth - 1))
    return x2.reshape(B, S, D)
```

```python
import functools

import jax
import jax.numpy as jnp
import numpy as np
from jax import lax
from jax.experimental import pallas as pl
from jax.experimental.pallas import tpu as pltpu

EPS = 1e-6
POOL_WINDOWS = (2, 4, 8, 16)
POOL_HALO = 16
CHUNK = 128
SG_HEADS = 4
SB_HD = 64
LANES = 128

ROW_TILE = 1024
MLP_ROW_GROUPS = 2
MLP_STAGE_ROWS = 512
INPROJ_STAGE_ROWS = 256
ATT_BLOCK = 128
ATT_SUB = 64
ATT_ROWS = 8192
ATT_SCREEN_GROUP = 16
ATT_SKEW = (3, 4)
STICK_CUTOFF = 30.0
MASK_LOGIT = -1e30
LOGIT_CAP = 80.0
LOG2E = 1.4426950408889634
VMEM_LIMIT = 56 * 1024 * 1024

F32 = jnp.float32
BF16 = jnp.bfloat16


def _rms(x, g):
    ms = jnp.mean(x * x, axis=-1, keepdims=True)
    return x * lax.rsqrt(ms + EPS) * g


def _gelu_tanh(x):
    c = np.float32(np.sqrt(2.0 / np.pi))
    return 0.5 * x * (1.0 + jnp.tanh(c * (x + 0.044715 * (x * x * x))))


def _inproj_kernel(x_ref, g1_ref, win_hbm, poolw_ref, pscale_ref, sgn_ref, sgw_ref, sgb_ref,
                   yab_ref, q_ref, kt_ref, v_ref, lvl_ref, win_ref, stage_ref, sem, *, layer, tiles_per_seq):
    T = x_ref.shape[0]
    pw = pscale_ref.shape[1]
    sw = sgn_ref.shape[1]
    i = pl.program_id(0)
    tile_in_seq = i % tiles_per_seq

    H = POOL_HALO

    @pl.when(i == 0)
    def _():
        lvl_ref[:, 0:H, :] = jnp.zeros((lvl_ref.shape[0], H, pw), F32)
        rows = stage_ref.shape[1]
        n_chunks = win_ref.shape[0] // rows
        copy = lambda n: pltpu.make_async_copy(win_hbm.at[layer, pl.ds(n * rows, rows), :],
                                               stage_ref.at[n % 2], sem.at[n % 2])
        copy(0).start()
        for n in range(n_chunks):
            if n + 1 < n_chunks:
                copy(n + 1).start()
            copy(n).wait()
            win_ref[n * rows:(n + 1) * rows, :] = stage_ref[n % 2].astype(BF16)

    @pl.when(tile_in_seq == 0)
    def _():
        lvl_ref[0, H:2 * H, :] = jnp.zeros((H, pw), F32)

    @pl.when(tile_in_seq != 0)
    def _():
        lvl_ref[0, H:2 * H, :] = lvl_ref[0, T + H:T + 2 * H, :]

    hs, bpres, a_s = [], [], []
    for r0 in range(0, T, T // 2):
        hg = _rms(x_ref[r0:r0 + T // 2, :], g1_ref[...]).astype(BF16)
        bpres.append(jnp.dot(hg, win_ref[:, pw:pw + 2 * sw], preferred_element_type=F32))
        a_s.append(jnp.dot(hg, win_ref[:, 0:pw], preferred_element_type=F32))
        hs.append(hg)
    h = jnp.concatenate(hs, axis=0)
    bpre = jnp.concatenate(bpres, axis=0)
    a = jnp.concatenate(a_s, axis=0)
    c2 = pw + 2 * sw
    sbw = q_ref.shape[1]
    half = sbw // 2

    def project(out_ref, col, out_col, scale=None):
        r = jnp.dot(h, win_ref[:, col:col + half], preferred_element_type=F32)
        out_ref[:, out_col:out_col + half] = (r if scale is None else r * scale).astype(BF16)

    r_i = lax.broadcasted_iota(jnp.int32, (CHUNK, CHUNK), 0)
    c_i = lax.broadcasted_iota(jnp.int32, (CHUNK, CHUNK), 1)
    tril = c_i <= r_i
    wm = jnp.concatenate([jnp.where(tril, sgw_ref[hh], 0.0).astype(BF16) for hh in range(SG_HEADS)], axis=1)
    head = lax.broadcasted_iota(jnp.int32, (1, sw), 1) // (sw // SG_HEADS)

    def gate_chunk(c):
        bz = _gelu_tanh(bpre[c * CHUNK:(c + 1) * CHUNK, :])
        vc = _rms(bz[:, sw:2 * sw], sgn_ref[...]).astype(BF16)
        vstack = jnp.concatenate([jnp.where(head == hh, vc, jnp.zeros_like(vc)) for hh in range(SG_HEADS)], axis=0)
        sv = jnp.dot(wm, vstack, preferred_element_type=F32) + sgb_ref[...]
        yab_ref[c * CHUNK:(c + 1) * CHUNK, pw:pw + sw] = (bz[:, 0:sw] * sv).astype(BF16)

    q_scale = float(SB_HD ** -0.5)
    pieces = [(q_ref, c2, 0, q_scale), (q_ref, c2 + half, half, q_scale),
              (v_ref, c2 + 2 * sbw, 0, None), (v_ref, c2 + 2 * sbw + half, half, None)]
    n_chunks = T // CHUNK
    per_piece = -(-n_chunks // len(pieces))
    for p, piece in enumerate(pieces):
        project(*piece)
        for c in range(p * per_piece, min((p + 1) * per_piece, n_chunks)):
            gate_chunk(c)

    lo, n = H, T + H
    lvl_ref[0, 2 * H:2 * H + T, :] = a
    for s, shift in enumerate((1, 2, 4, 8)):
        lvl_ref[s + 1, lo:lo + n, :] = (lvl_ref[s, lo:lo + n, :]
                                        + lvl_ref[s, lo - shift:lo - shift + n, :])
    lane = lax.broadcasted_iota(jnp.int32, (1, pw), 1)
    grp = lane // (pw // len(POOL_WINDOWS))
    win = jnp.where(grp == 0, 2, jnp.where(grp == 1, 4, jnp.where(grp == 2, 8, 16)))
    psum = jnp.where(grp == 0, lvl_ref[1, 2 * H:2 * H + T, :],
                     jnp.where(grp == 1, lvl_ref[2, 2 * H:2 * H + T, :],
                               jnp.where(grp == 2, lvl_ref[3, 2 * H:2 * H + T, :],
                                         lvl_ref[4, 2 * H:2 * H + T, :])))
    pos = tile_in_seq * T + lax.broadcasted_iota(jnp.int32, (T, 1), 0)
    cnt = jnp.minimum(pos + 1, win).astype(F32)
    d = psum / cnt - a
    ya = jnp.dot(d.astype(BF16), poolw_ref[...], preferred_element_type=F32) * pscale_ref[...]
    yab_ref[:, 0:pw] = ya.astype(BF16)

    kt = lax.dot_general(win_ref[:, c2 + sbw:c2 + 2 * sbw], h, (((0,), (1,)), ((), ())),
                         preferred_element_type=F32).astype(BF16)
    for hp in range(kt_ref.shape[0]):
        for c in range(kt_ref.shape[1]):
            kt_ref[hp, c] = kt[hp * LANES:(hp + 1) * LANES, c * ATT_BLOCK:(c + 1) * ATT_BLOCK]


def _inproj(x2, g1, win, poolw, pscale, sgn, sgw, sgb, *, layer, batch, seq):
    N, D = x2.shape
    T = ROW_TILE
    pw, sw = pscale.shape[2], sgn.shape[2]
    of_layer = lambda arr: pl.BlockSpec((None,) + arr.shape[1:], lambda i: (layer, 0, 0))
    sbw = (win.shape[2] - pw - 2 * sw) // 3
    tps = seq // T
    row = lambda i: (i, 0)
    return pl.pallas_call(
        functools.partial(_inproj_kernel, layer=layer, tiles_per_seq=tps),
        grid=(N // T,),
        in_specs=[
            pl.BlockSpec((T, D), row),
            of_layer(g1),
            pl.BlockSpec(memory_space=pl.ANY),
            of_layer(poolw),
            of_layer(pscale),
            of_layer(sgn),
            pl.BlockSpec((None,) + sgw.shape[1:], lambda i: (layer, 0, 0, 0)),
            of_layer(sgb),
        ],
        out_specs=[
            pl.BlockSpec((T, pw + sw), row),
            pl.BlockSpec((T, sbw), row),
            pl.BlockSpec((None, sbw // LANES, T // ATT_BLOCK, LANES, ATT_BLOCK),
                         lambda i: (i // tps, 0, i % tps, 0, 0)),
            pl.BlockSpec((T, sbw), row),
        ],
        out_shape=[
            jax.ShapeDtypeStruct((N, pw + sw), BF16),
            jax.ShapeDtypeStruct((N, sbw), BF16),
            jax.ShapeDtypeStruct((batch, sbw // LANES, seq // ATT_BLOCK, LANES, ATT_BLOCK), BF16),
            jax.ShapeDtypeStruct((N, sbw), BF16),
        ],
        scratch_shapes=[pltpu.VMEM((len(POOL_WINDOWS) + 1, T + 2 * POOL_HALO, pw), F32),
                        pltpu.VMEM(win.shape[1:], BF16), pltpu.VMEM((2, INPROJ_STAGE_ROWS, win.shape[2]), F32),
                        pltpu.SemaphoreType.DMA((2,))],
        compiler_params=pltpu.CompilerParams(
            dimension_semantics=("arbitrary",), vmem_limit_bytes=VMEM_LIMIT),
        name="inproj_mixers",
    )(x2, g1, win, poolw, pscale, sgn, sgw, sgb)


def _attn_kernel(q_ref, kt_ref, v_ref, tri_ref, o_ref, acc_ref, carry_ref, gmax_ref, cmax_ref):
    SB = ATT_SUB
    NS = o_ref.shape[0] // SB
    i = pl.program_id(2)
    kb0 = i * NS
    head0_lane = lax.broadcasted_iota(jnp.int32, (1, LANES), 1) < SB_HD
    row = lax.broadcasted_iota(jnp.int32, (SB, LANES), 0)
    col = lax.broadcasted_iota(jnp.int32, (SB, LANES), 1) & (SB - 1)
    cap_diag = jnp.where(col < row, LOGIT_CAP, MASK_LOGIT)
    tri = tri_ref[...]

    def key_rhs(ktile, rolled, half):
        h0, h1 = (ktile, rolled) if half == 0 else (rolled, ktile)
        zero = jnp.zeros((SB_HD, LANES), ktile.dtype)
        return jnp.concatenate([jnp.where(head0_lane, h0[:SB_HD], zero),
                                jnp.where(head0_lane, zero, h1[SB_HD:])], axis=0)

    def value_rhs(kb):
        vrows = v_ref[pl.ds(pl.multiple_of(kb * SB, SB), SB), :]
        zero = jnp.zeros_like(vrows)
        return jnp.concatenate([jnp.where(head0_lane, vrows, zero),
                                jnp.where(head0_lane, zero, vrows)], axis=0)

    def logits_stage(qrows, krhs, diagonal_rows):
        z = jnp.dot(qrows, krhs, preferred_element_type=F32)
        if diagonal_rows == z.shape[0]:
            z = jnp.minimum(z, cap_diag)
        elif diagonal_rows:
            z = jnp.concatenate([jnp.minimum(z[:diagonal_rows], cap_diag),
                                 jnp.minimum(z[diagonal_rows:], LOGIT_CAP)], axis=0)
        else:
            z = jnp.minimum(z, LOGIT_CAP)
        neg_log_1m = jnp.log(1.0 + jnp.exp2(z * LOG2E))
        hi = neg_log_1m.astype(BF16)
        lo = (neg_log_1m - hi.astype(F32)).astype(BF16)
        return z, jnp.concatenate([hi, lo], axis=1)

    def weights_stage(z, hilo):
        ext = jnp.dot(hilo, tri, preferred_element_type=F32)
        return jnp.exp(z + ext[:, :LANES]).astype(BF16), ext[:, LANES:]

    def output_stage(a, vrhs):
        return jnp.dot(a, vrhs, preferred_element_type=F32)

    def tile(qrows, krhs, vrhs):
        a, rs = weights_stage(*logits_stage(qrows, krhs, 0))
        return output_stage(a, vrhs), rs

    def q_rows(first, count):
        return q_ref[first * SB:(first + count) * SB, :]

    ktiles = {}
    for t in range(-1, NS // 2):
        kt_tile = kt_ref[jnp.maximum(kb0 // 2 + t, 0)]
        ktiles[t] = (kt_tile, pltpu.roll(kt_tile, SB, 1))

    def first_stage(m):
        krhs = key_rhs(*ktiles[m // 2], m % 2)
        if m == -1:
            return logits_stage(q_rows(0, 1), krhs, 0)
        if m == NS - 1:
            return logits_stage(q_rows(m, 1), krhs, SB)
        return logits_stage(q_rows(m, 2), krhs, SB)

    blocks = list(range(-1, NS))
    stage1, stage2, parts = {}, {}, []
    lag2, lag3 = ATT_SKEW[0], ATT_SKEW[0] + ATT_SKEW[1]
    for s in range(len(blocks) + lag3):
        if s < len(blocks):
            stage1[s] = first_stage(blocks[s])
        if 0 <= s - lag2 < len(blocks):
            stage2[s - lag2] = weights_stage(*stage1.pop(s - lag2))
        if 0 <= s - lag3 < len(blocks):
            a, rs = stage2.pop(s - lag3)
            vrhs = value_rhs(jnp.maximum(kb0 + blocks[s - lag3], 0))
            parts.append((output_stage(a, vrhs), rs))

    carry_max = None
    for r in range(NS):
        pv_p, rs_p = parts[r] if r == 0 else (parts[r][0][SB:], parts[r][1][SB:])
        pv_d, rs_d = parts[r + 1] if r == NS - 1 else (parts[r + 1][0][:SB], parts[r + 1][1][:SB])
        scale = jnp.exp(rs_d)
        if r == 0:
            scale = jnp.where(kb0 > 0, scale, 0.0)
        acc_ref[r] = pv_d + scale * pv_p
        carry_ref[r] = rs_d + rs_p
        carry_max = rs_d + rs_p if carry_max is None else jnp.maximum(carry_max, rs_d + rs_p)

    def sub_block(r, cmax0):
        qr = q_ref[pl.ds(pl.multiple_of(r * SB, SB), SB), :]

        def cond(st):
            j, cmax = st
            return jnp.logical_and(j >= 0, cmax > -STICK_CUTOFF)

        def body(st):
            j, _ = st
            kt_tile = kt_ref[lax.shift_right_logical(j, 1)]
            rolled = pltpu.roll(kt_tile, SB, 1)
            krhs = jnp.where((j & 1) == 0, key_rhs(kt_tile, rolled, 0), key_rhs(kt_tile, rolled, 1))
            pv, rs = tile(qr, krhs, value_rhs(j))
            carry = carry_ref[r]
            acc_ref[r] += jnp.exp(carry) * pv
            carry_ref[r] = carry + rs
            return j - 1, jnp.max(carry + rs)

        lax.while_loop(cond, body, (kb0 + r - 2, cmax0))

    G = ATT_SCREEN_GROUP
    n_groups = NS // G

    def group(g, _):
        @pl.when(gmax_ref[g] > -STICK_CUTOFF)
        def _():
            for k in range(G):
                cmax_ref[k] = jnp.max(carry_ref[g * G + k])

            def member(k, _):
                @pl.when(cmax_ref[k] > -STICK_CUTOFF)
                def _():
                    sub_block(g * G + k, cmax_ref[k])

                return 0

            lax.fori_loop(0, G, member, 0)

        return 0

    @pl.when(jnp.max(carry_max) > -STICK_CUTOFF)
    def _():
        for g in range(n_groups):
            gmax = carry_ref[g * G]
            for k in range(1, G):
                gmax = jnp.maximum(gmax, carry_ref[g * G + k])
            gmax_ref[g] = jnp.max(gmax)
        lax.fori_loop(0, n_groups, group, 0)

    for r in range(NS):
        o_ref[r * SB:(r + 1) * SB, :] = acc_ref[r].astype(o_ref.dtype)


def _stick_matrix():
    key = np.arange(LANES) % ATT_SUB
    head = np.arange(LANES) // ATT_SUB
    same = head[:, None] == head[None, :]
    suffix = np.logical_and(same, key[:, None] >= key[None, :])
    half = -np.concatenate([suffix, same], axis=1).astype(np.float32)
    return jnp.asarray(np.concatenate([half, half], axis=0), BF16)


def _attention(q, kt, v, *, batch, seq):
    N, W = q.shape
    R = ATT_ROWS
    q3, v3 = q.reshape(batch, seq, W), v.reshape(batch, seq, W)
    tri = _stick_matrix()
    resident = lambda b, hp, i: (b, 0, hp)
    out = pl.pallas_call(
        _attn_kernel,
        grid=(batch, W // LANES, seq // R),
        in_specs=[
            pl.BlockSpec((None, R, LANES), lambda b, hp, i: (b, i, hp)),
            pl.BlockSpec((None, None, seq // ATT_BLOCK, LANES, ATT_BLOCK), lambda b, hp, i: (b, hp, 0, 0, 0)),
            pl.BlockSpec((None, seq, LANES), resident),
            pl.BlockSpec(tri.shape, lambda b, hp, i: (0, 0)),
        ],
        out_specs=pl.BlockSpec((None, R, LANES), lambda b, hp, i: (b, i, hp)),
        out_shape=jax.ShapeDtypeStruct((batch, seq, W), BF16),
        scratch_shapes=[pltpu.VMEM((R // ATT_SUB, ATT_SUB, LANES), F32),
                        pltpu.VMEM((R // ATT_SUB, ATT_SUB, LANES), F32),
                        pltpu.SMEM((R // ATT_SUB // ATT_SCREEN_GROUP,), F32),
                        pltpu.SMEM((ATT_SCREEN_GROUP,), F32)],
        compiler_params=pltpu.CompilerParams(
            dimension_semantics=("arbitrary", "arbitrary", "arbitrary"),
            vmem_limit_bytes=VMEM_LIMIT),
        name="stickbreak_attn",
    )(q3, kt, v3, tri)
    return out.reshape(N, W)


def _mlp_kernel(x_ref, yab_ref, yc_ref, g2_ref, gf_ref, wo_hbm, wup_hbm, wdn_hbm, o_ref,
                wo_ref, wup_ref, wdn_ref, stage_ref, sem, *, layer, final_norm, ff_chunk):
    D = x_ref.shape[1]

    @pl.when(pl.program_id(0) == 0)
    def _():
        chunks = []
        for src, dst in ((wo_hbm, wo_ref), (wup_hbm, wup_ref), (wdn_hbm, wdn_ref)):
            for col in range(0, dst.shape[1], D):
                for r0 in range(0, dst.shape[0], MLP_STAGE_ROWS):
                    chunks.append((src.at[layer, pl.ds(r0, MLP_STAGE_ROWS), pl.ds(col, D)],
                                   dst.at[pl.ds(r0, MLP_STAGE_ROWS), pl.ds(col, D)]))
        copy = lambda n: pltpu.make_async_copy(chunks[n][0], stage_ref.at[n % 2], sem.at[n % 2])
        copy(0).start()
        for n in range(len(chunks)):
            if n + 1 < len(chunks):
                copy(n + 1).start()
            copy(n).wait()
            chunks[n][1][...] = stage_ref[n % 2].astype(BF16)

    nab = yab_ref.shape[1]
    T = x_ref.shape[0]
    rows = T // MLP_ROW_GROUPS
    x1s, hs = [], []
    for g in range(MLP_ROW_GROUPS):
        r = slice(g * rows, (g + 1) * rows)
        x1 = (x_ref[r, :]
              + jnp.dot(yab_ref[r, :], wo_ref[0:nab, :], preferred_element_type=F32)
              + jnp.dot(yc_ref[r, :], wo_ref[nab:, :], preferred_element_type=F32))
        x1s.append(x1)
        hs.append(_rms(x1, g2_ref[...]).astype(BF16))
    for g in range(MLP_ROW_GROUPS):
        acc = None
        for c in range(wup_ref.shape[1] // ff_chunk):
            up = jnp.dot(hs[g], wup_ref[:, c * ff_chunk:(c + 1) * ff_chunk], preferred_element_type=F32)
            act = jnp.square(jnp.maximum(up, 0.0)).astype(BF16)
            dn = jnp.dot(act, wdn_ref[c * ff_chunk:(c + 1) * ff_chunk, :], preferred_element_type=F32)
            acc = dn if acc is None else acc + dn
        acc = acc + x1s[g]
        if final_norm:
            acc = _rms(acc, gf_ref[...])
        o_ref[g * rows:(g + 1) * rows, :] = acc


def _mlp(x2, yab, yc, wo, g2, wup, wdn, gf, *, layer, final_norm):
    N, D = x2.shape
    T = ROW_TILE
    const = lambda i: (0, 0)
    row = lambda i: (i, 0)
    hbm = pl.BlockSpec(memory_space=pl.ANY)
    return pl.pallas_call(
        functools.partial(_mlp_kernel, layer=layer, final_norm=final_norm, ff_chunk=1024),
        grid=(N // T,),
        in_specs=[
            pl.BlockSpec((T, D), row),
            pl.BlockSpec((T, yab.shape[1]), row),
            pl.BlockSpec((T, yc.shape[1]), row),
            pl.BlockSpec((None, 1, D), lambda i: (layer, 0, 0)),
            pl.BlockSpec((1, D), const),
            hbm, hbm, hbm,
        ],
        out_specs=pl.BlockSpec((T, D), row),
        out_shape=jax.ShapeDtypeStruct((N, D), F32),
        scratch_shapes=[pltpu.VMEM(wo.shape[1:], BF16), pltpu.VMEM(wup.shape[1:], BF16),
                        pltpu.VMEM(wdn.shape[1:], BF16), pltpu.VMEM((2, MLP_STAGE_ROWS, D), F32),
                        pltpu.SemaphoreType.DMA((2,))],
        compiler_params=pltpu.CompilerParams(
            dimension_semantics=("arbitrary",), vmem_limit_bytes=VMEM_LIMIT),
        name="outproj_mlp",
    )(x2, yab, yc, g2, gf, wo, wup, wdn)


def kernel(x, norm1, w_in, pool_w, pool_scale, sg_norm, sg_w, sg_b, w_out, norm2, w_up, w_down, final_norm):
    B, S, D = x.shape
    depth = norm1.shape[0]
    n_grp = pool_w.shape[1]
    pw, sw = pool_scale.shape[1], sg_norm.shape[1]
    sbw = (w_in.shape[2] - pw - 2 * sw) // 3
    assert S % ROW_TILE == 0 and S % ATT_ROWS == 0 and ROW_TILE % CHUNK == 0 and ROW_TILE % ATT_BLOCK == 0
    assert sg_w.shape[1] == SG_HEADS and sg_w.shape[2] == CHUNK and n_grp == len(POOL_WINDOWS)
    assert sbw % LANES == 0 and ATT_BLOCK == LANES == 2 * ATT_SUB == 2 * SB_HD and (ATT_ROWS // ATT_SUB) % max(2, ATT_SCREEN_GROUP) == 0

    poolw_all = jnp.einsum('lgcd,gh->lgchd', pool_w, jnp.eye(n_grp, dtype=pool_w.dtype)).reshape(depth, pw, pw).astype(BF16)
    sgb_all = jnp.repeat(jnp.swapaxes(sg_b, 1, 2), sw // SG_HEADS, axis=2)
    g1_all, g2_all = norm1[:, None, :], norm2[:, None, :]
    pscale_all, sgn_all = pool_scale[:, None, :], sg_norm[:, None, :]
    x2 = x.reshape(B * S, D)
    for l in range(depth):
        yab, q, kt, v = _inproj(x2, g1_all, w_in, poolw_all, pscale_all, sgn_all, sg_w, sgb_all,
                                layer=l, batch=B, seq=S)
        yc = _attention(q, kt, v, batch=B, seq=S)
        x2 = _mlp(x2, yab, yc, w_out, g2_all, w_up, w_down, final_norm[None],
                  layer=l, final_norm=(l == depth - 1))
    return x2.reshape(B, S, D)
```

```python
import functools

import jax
import jax.numpy as jnp
import numpy as np
from jax import lax
from jax.experimental import pallas as pl
from jax.experimental.pallas import tpu as pltpu

EPS = 1e-6
POOL_WINDOWS = (2, 4, 8, 16)
POOL_HALO = 16
CHUNK = 128
SG_HEADS = 4
SB_HD = 64
LANES = 128

ROW_TILE = 1024
MLP_ROW_GROUPS = 4
MLP_STAGE_ROWS = 512
INPROJ_STAGE_ROWS = 256
ATT_BLOCK = 128
ATT_SUB = 64
ATT_ROWS = 8192
ATT_SCREEN_GROUP = 16
ATT_SKEW = (3, 4)
STICK_CUTOFF = 30.0
MASK_LOGIT = -1e30
LOGIT_CAP = 80.0
LOG2E = 1.4426950408889634
VMEM_LIMIT = 56 * 1024 * 1024

F32 = jnp.float32
BF16 = jnp.bfloat16


def _rms(x, g):
    ms = jnp.mean(x * x, axis=-1, keepdims=True)
    return x * lax.rsqrt(ms + EPS) * g


def _gelu_tanh(x):
    c = np.float32(np.sqrt(2.0 / np.pi))
    return 0.5 * x * (1.0 + jnp.tanh(c * (x + 0.044715 * (x * x * x))))


def _inproj_kernel(x_ref, g1_ref, win_hbm, poolw_ref, pscale_ref, sgn_ref, sgw_ref, sgb_ref,
                   yab_ref, q_ref, kt_ref, v_ref, lvl_ref, win_ref, stage_ref, sem, *, layer, tiles_per_seq):
    T = x_ref.shape[0]
    pw = pscale_ref.shape[1]
    sw = sgn_ref.shape[1]
    i = pl.program_id(0)
    tile_in_seq = i % tiles_per_seq

    H = POOL_HALO

    @pl.when(i == 0)
    def _():
        lvl_ref[:, 0:H, :] = jnp.zeros((lvl_ref.shape[0], H, pw), F32)
        rows = stage_ref.shape[1]
        n_chunks = win_ref.shape[0] // rows
        copy = lambda n: pltpu.make_async_copy(win_hbm.at[layer, pl.ds(n * rows, rows), :],
                                               stage_ref.at[n % 2], sem.at[n % 2])
        copy(0).start()
        for n in range(n_chunks):
            if n + 1 < n_chunks:
                copy(n + 1).start()
            copy(n).wait()
            win_ref[n * rows:(n + 1) * rows, :] = stage_ref[n % 2].astype(BF16)

    @pl.when(tile_in_seq == 0)
    def _():
        lvl_ref[0, H:2 * H, :] = jnp.zeros((H, pw), F32)

    @pl.when(tile_in_seq != 0)
    def _():
        lvl_ref[0, H:2 * H, :] = lvl_ref[0, T + H:T + 2 * H, :]

    hs, bpres, a_s = [], [], []
    for r0 in range(0, T, T // 2):
        hg = _rms(x_ref[r0:r0 + T // 2, :], g1_ref[...]).astype(BF16)
        bpres.append(jnp.dot(hg, win_ref[:, pw:pw + 2 * sw], preferred_element_type=F32))
        a_s.append(jnp.dot(hg, win_ref[:, 0:pw], preferred_element_type=F32))
        hs.append(hg)
    h = jnp.concatenate(hs, axis=0)
    bpre = jnp.concatenate(bpres, axis=0)
    a = jnp.concatenate(a_s, axis=0)
    c2 = pw + 2 * sw
    sbw = q_ref.shape[1]
    half = sbw // 2

    def project(out_ref, col, out_col, scale=None):
        r = jnp.dot(h, win_ref[:, col:col + half], preferred_element_type=F32)
        out_ref[:, out_col:out_col + half] = (r if scale is None else r * scale).astype(BF16)

    r_i = lax.broadcasted_iota(jnp.int32, (CHUNK, CHUNK), 0)
    c_i = lax.broadcasted_iota(jnp.int32, (CHUNK, CHUNK), 1)
    tril = c_i <= r_i
    wm = jnp.concatenate([jnp.where(tril, sgw_ref[hh], 0.0).astype(BF16) for hh in range(SG_HEADS)], axis=1)
    head = lax.broadcasted_iota(jnp.int32, (1, sw), 1) // (sw // SG_HEADS)

    def gate_chunk(c):
        bz = _gelu_tanh(bpre[c * CHUNK:(c + 1) * CHUNK, :])
        vc = _rms(bz[:, sw:2 * sw], sgn_ref[...]).astype(BF16)
        vstack = jnp.concatenate([jnp.where(head == hh, vc, jnp.zeros_like(vc)) for hh in range(SG_HEADS)], axis=0)
        sv = jnp.dot(wm, vstack, preferred_element_type=F32) + sgb_ref[...]
        yab_ref[c * CHUNK:(c + 1) * CHUNK, pw:pw + sw] = (bz[:, 0:sw] * sv).astype(BF16)

    q_scale = float(SB_HD ** -0.5)
    pieces = [(q_ref, c2, 0, q_scale), (q_ref, c2 + half, half, q_scale),
              (v_ref, c2 + 2 * sbw, 0, None), (v_ref, c2 + 2 * sbw + half, half, None)]
    n_chunks = T // CHUNK
    per_piece = -(-n_chunks // len(pieces))
    for p, piece in enumerate(pieces):
        project(*piece)
        for c in range(p * per_piece, min((p + 1) * per_piece, n_chunks)):
            gate_chunk(c)

    lo, n = H, T + H
    lvl_ref[0, 2 * H:2 * H + T, :] = a
    for s, shift in enumerate((1, 2, 4, 8)):
        lvl_ref[s + 1, lo:lo + n, :] = (lvl_ref[s, lo:lo + n, :]
                                        + lvl_ref[s, lo - shift:lo - shift + n, :])
    lane = lax.broadcasted_iota(jnp.int32, (1, pw), 1)
    grp = lane // (pw // len(POOL_WINDOWS))
    win = jnp.where(grp == 0, 2, jnp.where(grp == 1, 4, jnp.where(grp == 2, 8, 16)))
    psum = jnp.where(grp == 0, lvl_ref[1, 2 * H:2 * H + T, :],
                     jnp.where(grp == 1, lvl_ref[2, 2 * H:2 * H + T, :],
                               jnp.where(grp == 2, lvl_ref[3, 2 * H:2 * H + T, :],
                                         lvl_ref[4, 2 * H:2 * H + T, :])))
    pos = tile_in_seq * T + lax.broadcasted_iota(jnp.int32, (T, 1), 0)
    cnt = jnp.minimum(pos + 1, win).astype(F32)
    d = psum / cnt - a
    ya = jnp.dot(d.astype(BF16), poolw_ref[...], preferred_element_type=F32) * pscale_ref[...]
    yab_ref[:, 0:pw] = ya.astype(BF16)

    kt = lax.dot_general(win_ref[:, c2 + sbw:c2 + 2 * sbw], h, (((0,), (1,)), ((), ())),
                         preferred_element_type=F32).astype(BF16)
    for hp in range(kt_ref.shape[0]):
        for c in range(kt_ref.shape[1]):
            kt_ref[hp, c] = kt[hp * LANES:(hp + 1) * LANES, c * ATT_BLOCK:(c + 1) * ATT_BLOCK]


def _inproj(x2, g1, win, poolw, pscale, sgn, sgw, sgb, *, layer, batch, seq):
    N, D = x2.shape
    T = ROW_TILE
    pw, sw = pscale.shape[2], sgn.shape[2]
    of_layer = lambda arr: pl.BlockSpec((None,) + arr.shape[1:], lambda i: (layer, 0, 0))
    sbw = (win.shape[2] - pw - 2 * sw) // 3
    tps = seq // T
    row = lambda i: (i, 0)
    return pl.pallas_call(
        functools.partial(_inproj_kernel, layer=layer, tiles_per_seq=tps),
        grid=(N // T,),
        in_specs=[
            pl.BlockSpec((T, D), row),
            of_layer(g1),
            pl.BlockSpec(memory_space=pl.ANY),
            of_layer(poolw),
            of_layer(pscale),
            of_layer(sgn),
            pl.BlockSpec((None,) + sgw.shape[1:], lambda i: (layer, 0, 0, 0)),
            of_layer(sgb),
        ],
        out_specs=[
            pl.BlockSpec((T, pw + sw), row),
            pl.BlockSpec((T, sbw), row),
            pl.BlockSpec((None, sbw // LANES, T // ATT_BLOCK, LANES, ATT_BLOCK),
                         lambda i: (i // tps, 0, i % tps, 0, 0)),
            pl.BlockSpec((T, sbw), row),
        ],
        out_shape=[
            jax.ShapeDtypeStruct((N, pw + sw), BF16),
            jax.ShapeDtypeStruct((N, sbw), BF16),
            jax.ShapeDtypeStruct((batch, sbw // LANES, seq // ATT_BLOCK, LANES, ATT_BLOCK), BF16),
            jax.ShapeDtypeStruct((N, sbw), BF16),
        ],
        scratch_shapes=[pltpu.VMEM((len(POOL_WINDOWS) + 1, T + 2 * POOL_HALO, pw), F32),
                        pltpu.VMEM(win.shape[1:], BF16), pltpu.VMEM((2, INPROJ_STAGE_ROWS, win.shape[2]), F32),
                        pltpu.SemaphoreType.DMA((2,))],
        compiler_params=pltpu.CompilerParams(
            dimension_semantics=("arbitrary",), vmem_limit_bytes=VMEM_LIMIT),
        name="inproj_mixers",
    )(x2, g1, win, poolw, pscale, sgn, sgw, sgb)


def _attn_kernel(q_ref, kt_ref, v_ref, tri_ref, o_ref, acc_ref, carry_ref, gmax_ref, cmax_ref):
    SB = ATT_SUB
    NS = o_ref.shape[0] // SB
    i = pl.program_id(2)
    kb0 = i * NS
    head0_lane = lax.broadcasted_iota(jnp.int32, (1, LANES), 1) < SB_HD
    row = lax.broadcasted_iota(jnp.int32, (SB, LANES), 0)
    col = lax.broadcasted_iota(jnp.int32, (SB, LANES), 1) & (SB - 1)
    cap_diag = jnp.where(col < row, LOGIT_CAP, MASK_LOGIT)
    tri = tri_ref[...]

    def key_rhs(ktile, rolled, half):
        h0, h1 = (ktile, rolled) if half == 0 else (rolled, ktile)
        zero = jnp.zeros((SB_HD, LANES), ktile.dtype)
        return jnp.concatenate([jnp.where(head0_lane, h0[:SB_HD], zero),
                                jnp.where(head0_lane, zero, h1[SB_HD:])], axis=0)

    def value_rhs(kb):
        vrows = v_ref[pl.ds(pl.multiple_of(kb * SB, SB), SB), :]
        zero = jnp.zeros_like(vrows)
        return jnp.concatenate([jnp.where(head0_lane, vrows, zero),
                                jnp.where(head0_lane, zero, vrows)], axis=0)

    def logits_stage(qrows, krhs, diagonal_rows):
        z = jnp.dot(qrows, krhs, preferred_element_type=F32)
        if diagonal_rows == z.shape[0]:
            z = jnp.minimum(z, cap_diag)
        elif diagonal_rows:
            z = jnp.concatenate([jnp.minimum(z[:diagonal_rows], cap_diag),
                                 jnp.minimum(z[diagonal_rows:], LOGIT_CAP)], axis=0)
        else:
            z = jnp.minimum(z, LOGIT_CAP)
        neg_log_1m = jnp.log(1.0 + jnp.exp2(z * LOG2E))
        hi = neg_log_1m.astype(BF16)
        lo = (neg_log_1m - hi.astype(F32)).astype(BF16)
        return z, jnp.concatenate([hi, lo], axis=1)

    def weights_stage(z, hilo):
        ext = jnp.dot(hilo, tri, preferred_element_type=F32)
        return jnp.exp(z + ext[:, :LANES]).astype(BF16), ext[:, LANES:]

    def output_stage(a, vrhs):
        return jnp.dot(a, vrhs, preferred_element_type=F32)

    def tile(qrows, krhs, vrhs):
        a, rs = weights_stage(*logits_stage(qrows, krhs, 0))
        return output_stage(a, vrhs), rs

    def q_rows(first, count):
        return q_ref[first * SB:(first + count) * SB, :]

    ktiles = {}
    for t in range(-1, NS // 2):
        kt_tile = kt_ref[jnp.maximum(kb0 // 2 + t, 0)]
        ktiles[t] = (kt_tile, pltpu.roll(kt_tile, SB, 1))

    def first_stage(m):
        krhs = key_rhs(*ktiles[m // 2], m % 2)
        if m == -1:
            return logits_stage(q_rows(0, 1), krhs, 0)
        if m == NS - 1:
            return logits_stage(q_rows(m, 1), krhs, SB)
        return logits_stage(q_rows(m, 2), krhs, SB)

    blocks = list(range(-1, NS))
    stage1, stage2, parts = {}, {}, []
    lag2, lag3 = ATT_SKEW[0], ATT_SKEW[0] + ATT_SKEW[1]
    for s in range(len(blocks) + lag3):
        if s < len(blocks):
            stage1[s] = first_stage(blocks[s])
        if 0 <= s - lag2 < len(blocks):
            stage2[s - lag2] = weights_stage(*stage1.pop(s - lag2))
        if 0 <= s - lag3 < len(blocks):
            a, rs = stage2.pop(s - lag3)
            vrhs = value_rhs(jnp.maximum(kb0 + blocks[s - lag3], 0))
            parts.append((output_stage(a, vrhs), rs))

    carry_max = None
    for r in range(NS):
        pv_p, rs_p = parts[r] if r == 0 else (parts[r][0][SB:], parts[r][1][SB:])
        pv_d, rs_d = parts[r + 1] if r == NS - 1 else (parts[r + 1][0][:SB], parts[r + 1][1][:SB])
        scale = jnp.exp(rs_d)
        if r == 0:
            scale = jnp.where(kb0 > 0, scale, 0.0)
        acc_ref[r] = pv_d + scale * pv_p
        carry_ref[r] = rs_d + rs_p
        carry_max = rs_d + rs_p if carry_max is None else jnp.maximum(carry_max, rs_d + rs_p)

    def sub_block(r, cmax0):
        qr = q_ref[pl.ds(pl.multiple_of(r * SB, SB), SB), :]

        def cond(st):
            j, cmax = st
            return jnp.logical_and(j >= 0, cmax > -STICK_CUTOFF)

        def body(st):
            j, _ = st
            kt_tile = kt_ref[lax.shift_right_logical(j, 1)]
            rolled = pltpu.roll(kt_tile, SB, 1)
            krhs = jnp.where((j & 1) == 0, key_rhs(kt_tile, rolled, 0), key_rhs(kt_tile, rolled, 1))
            pv, rs = tile(qr, krhs, value_rhs(j))
            carry = carry_ref[r]
            acc_ref[r] += jnp.exp(carry) * pv
            carry_ref[r] = carry + rs
            return j - 1, jnp.max(carry + rs)

        lax.while_loop(cond, body, (kb0 + r - 2, cmax0))

    G = ATT_SCREEN_GROUP
    n_groups = NS // G

    def group(g, _):
        @pl.when(gmax_ref[g] > -STICK_CUTOFF)
        def _():
            for k in range(G):
                cmax_ref[k] = jnp.max(carry_ref[g * G + k])

            def member(k, _):
                @pl.when(cmax_ref[k] > -STICK_CUTOFF)
                def _():
                    sub_block(g * G + k, cmax_ref[k])

                return 0

            lax.fori_loop(0, G, member, 0)

        return 0

    @pl.when(jnp.max(carry_max) > -STICK_CUTOFF)
    def _():
        for g in range(n_groups):
            gmax = carry_ref[g * G]
            for k in range(1, G):
                gmax = jnp.maximum(gmax, carry_ref[g * G + k])
            gmax_ref[g] = jnp.max(gmax)
        lax.fori_loop(0, n_groups, group, 0)

    for r in range(NS):
        o_ref[r * SB:(r + 1) * SB, :] = acc_ref[r].astype(o_ref.dtype)


def _stick_matrix():
    key = np.arange(LANES) % ATT_SUB
    head = np.arange(LANES) // ATT_SUB
    same = head[:, None] == head[None, :]
    suffix = np.logical_and(same, key[:, None] >= key[None, :])
    half = -np.concatenate([suffix, same], axis=1).astype(np.float32)
    return jnp.asarray(np.concatenate([half, half], axis=0), BF16)


def _attention(q, kt, v, *, batch, seq):
    N, W = q.shape
    R = ATT_ROWS
    q3, v3 = q.reshape(batch, seq, W), v.reshape(batch, seq, W)
    tri = _stick_matrix()
    resident = lambda b, hp, i: (b, 0, hp)
    out = pl.pallas_call(
        _attn_kernel,
        grid=(batch, W // LANES, seq // R),
        in_specs=[
            pl.BlockSpec((None, R, LANES), lambda b, hp, i: (b, i, hp)),
            pl.BlockSpec((None, None, seq // ATT_BLOCK, LANES, ATT_BLOCK), lambda b, hp, i: (b, hp, 0, 0, 0)),
            pl.BlockSpec((None, seq, LANES), resident),
            pl.BlockSpec(tri.shape, lambda b, hp, i: (0, 0)),
        ],
        out_specs=pl.BlockSpec((None, R, LANES), lambda b, hp, i: (b, i, hp)),
        out_shape=jax.ShapeDtypeStruct((batch, seq, W), BF16),
        scratch_shapes=[pltpu.VMEM((R // ATT_SUB, ATT_SUB, LANES), F32),
                        pltpu.VMEM((R // ATT_SUB, ATT_SUB, LANES), F32),
                        pltpu.SMEM((R // ATT_SUB // ATT_SCREEN_GROUP,), F32),
                        pltpu.SMEM((ATT_SCREEN_GROUP,), F32)],
        compiler_params=pltpu.CompilerParams(
            dimension_semantics=("arbitrary", "arbitrary", "arbitrary"),
            vmem_limit_bytes=VMEM_LIMIT),
        name="stickbreak_attn",
    )(q3, kt, v3, tri)
    return out.reshape(N, W)


def _mlp_kernel(x_ref, yab_ref, yc_ref, g2_ref, gf_ref, wo_hbm, wup_hbm, wdn_hbm, o_ref,
                wo_ref, wup_ref, wdn_ref, stage_ref, sem, *, layer, final_norm, ff_chunk):
    D = x_ref.shape[1]

    @pl.when(pl.program_id(0) == 0)
    def _():
        chunks = []
        for src, dst in ((wo_hbm, wo_ref), (wup_hbm, wup_ref), (wdn_hbm, wdn_ref)):
            for col in range(0, dst.shape[1], D):
                for r0 in range(0, dst.shape[0], MLP_STAGE_ROWS):
                    chunks.append((src.at[layer, pl.ds(r0, MLP_STAGE_ROWS), pl.ds(col, D)],
                                   dst.at[pl.ds(r0, MLP_STAGE_ROWS), pl.ds(col, D)]))
        copy = lambda n: pltpu.make_async_copy(chunks[n][0], stage_ref.at[n % 2], sem.at[n % 2])
        copy(0).start()
        for n in range(len(chunks)):
            if n + 1 < len(chunks):
                copy(n + 1).start()
            copy(n).wait()
            chunks[n][1][...] = stage_ref[n % 2].astype(BF16)

    nab = yab_ref.shape[1]
    T = x_ref.shape[0]
    rows = T // MLP_ROW_GROUPS
    x1s, hs = [], []
    for g in range(MLP_ROW_GROUPS):
        r = slice(g * rows, (g + 1) * rows)
        x1 = (x_ref[r, :]
              + jnp.dot(yab_ref[r, :], wo_ref[0:nab, :], preferred_element_type=F32)
              + jnp.dot(yc_ref[r, :], wo_ref[nab:, :], preferred_element_type=F32))
        x1s.append(x1)
        hs.append(_rms(x1, g2_ref[...]).astype(BF16))
    for g in range(MLP_ROW_GROUPS):
        acc = None
        for c in range(wup_ref.shape[1] // ff_chunk):
            up = jnp.dot(hs[g], wup_ref[:, c * ff_chunk:(c + 1) * ff_chunk], preferred_element_type=F32)
            act = jnp.square(jnp.maximum(up, 0.0)).astype(BF16)
            dn = jnp.dot(act, wdn_ref[c * ff_chunk:(c + 1) * ff_chunk, :], preferred_element_type=F32)
            acc = dn if acc is None else acc + dn
        acc = acc + x1s[g]
        if final_norm:
            acc = _rms(acc, gf_ref[...])
        o_ref[g * rows:(g + 1) * rows, :] = acc


def _mlp(x2, yab, yc, wo, g2, wup, wdn, gf, *, layer, final_norm):
    N, D = x2.shape
    T = ROW_TILE
    const = lambda i: (0, 0)
    row = lambda i: (i, 0)
    hbm = pl.BlockSpec(memory_space=pl.ANY)
    return pl.pallas_call(
        functools.partial(_mlp_kernel, layer=layer, final_norm=final_norm, ff_chunk=1024),
        grid=(N // T,),
        in_specs=[
            pl.BlockSpec((T, D), row),
            pl.BlockSpec((T, yab.shape[1]), row),
            pl.BlockSpec((T, yc.shape[1]), row),
            pl.BlockSpec((None, 1, D), lambda i: (layer, 0, 0)),
            pl.BlockSpec((1, D), const),
            hbm, hbm, hbm,
        ],
        out_specs=pl.BlockSpec((T, D), row),
        out_shape=jax.ShapeDtypeStruct((N, D), F32),
        scratch_shapes=[pltpu.VMEM(wo.shape[1:], BF16), pltpu.VMEM(wup.shape[1:], BF16),
                        pltpu.VMEM(wdn.shape[1:], BF16), pltpu.VMEM((2, MLP_STAGE_ROWS, D), F32),
                        pltpu.SemaphoreType.DMA((2,))],
        compiler_params=pltpu.CompilerParams(
            dimension_semantics=("arbitrary",), vmem_limit_bytes=VMEM_LIMIT),
        name="outproj_mlp",
    )(x2, yab, yc, g2, gf, wo, wup, wdn)


def kernel(x, norm1, w_in, pool_w, pool_scale, sg_norm, sg_w, sg_b, w_out, norm2, w_up, w_down, final_norm):
    B, S, D = x.shape
    depth = norm1.shape[0]
    n_grp = pool_w.shape[1]
    pw, sw = pool_scale.shape[1], sg_norm.shape[1]
    sbw = (w_in.shape[2] - pw - 2 * sw) // 3
    assert S % ROW_TILE == 0 and S % ATT_ROWS == 0 and ROW_TILE % CHUNK == 0 and ROW_TILE % ATT_BLOCK == 0
    assert sg_w.shape[1] == SG_HEADS and sg_w.shape[2] == CHUNK and n_grp == len(POOL_WINDOWS)
    assert sbw % LANES == 0 and ATT_BLOCK == LANES == 2 * ATT_SUB == 2 * SB_HD and (ATT_ROWS // ATT_SUB) % max(2, ATT_SCREEN_GROUP) == 0

    poolw_all = jnp.einsum('lgcd,gh->lgchd', pool_w, jnp.eye(n_grp, dtype=pool_w.dtype)).reshape(depth, pw, pw).astype(BF16)
    sgb_all = jnp.repeat(jnp.swapaxes(sg_b, 1, 2), sw // SG_HEADS, axis=2)
    g1_all, g2_all = norm1[:, None, :], norm2[:, None, :]
    pscale_all, sgn_all = pool_scale[:, None, :], sg_norm[:, None, :]
    x2 = x.reshape(B * S, D)
    for l in range(depth):
        yab, q, kt, v = _inproj(x2, g1_all, w_in, poolw_all, pscale_all, sgn_all, sg_w, sgb_all,
                                layer=l, batch=B, seq=S)
        yc = _attention(q, kt, v, batch=B, seq=S)
        x2 = _mlp(x2, yab, yc, w_out, g2_all, w_up, w_down, final_norm[None],
                  layer=l, final_norm=(l == depth - 1))
    return x2.reshape(B, S, D)
```

```python
import functools

import jax
import jax.numpy as jnp
import numpy as np
from jax import lax
from jax.experimental import pallas as pl
from jax.experimental.pallas import tpu as pltpu

EPS = 1e-6
POOL_WINDOWS = (2, 4, 8, 16)
POOL_HALO = 16
CHUNK = 128
SG_HEADS = 4
SB_HD = 64
LANES = 128

ROW_TILE = 1024
MLP_ROW_GROUPS = 4
MLP_STAGE_ROWS = 512
INPROJ_STAGE_ROWS = 256
STAGE_BUFFERS = 3
ATT_BLOCK = 128
ATT_SUB = 64
ATT_ROWS = 8192
ATT_SCREEN_GROUP = 16
ATT_SKEW = (3, 4)
STICK_CUTOFF = 30.0
MASK_LOGIT = -1e30
LOGIT_CAP = 80.0
LOG2E = 1.4426950408889634
VMEM_LIMIT = 56 * 1024 * 1024

F32 = jnp.float32
BF16 = jnp.bfloat16


def _rms(x, g):
    ms = jnp.mean(x * x, axis=-1, keepdims=True)
    return x * lax.rsqrt(ms + EPS) * g


def _gelu_tanh(x):
    c = np.float32(np.sqrt(2.0 / np.pi))
    return 0.5 * x * (1.0 + jnp.tanh(c * (x + 0.044715 * (x * x * x))))


def _inproj_kernel(x_ref, g1_ref, win_hbm, poolw_ref, pscale_ref, sgn_ref, sgw_ref, sgb_ref,
                   yab_ref, q_ref, kt_ref, v_ref, lvl_ref, win_ref, stage_ref, sem, *, layer, tiles_per_seq):
    T = x_ref.shape[0]
    pw = pscale_ref.shape[1]
    sw = sgn_ref.shape[1]
    i = pl.program_id(0)
    tile_in_seq = i % tiles_per_seq

    H = POOL_HALO

    @pl.when(i == 0)
    def _():
        lvl_ref[:, 0:H, :] = jnp.zeros((lvl_ref.shape[0], H, pw), F32)
        rows = stage_ref.shape[1]
        n_chunks = win_ref.shape[0] // rows
        copy = lambda n: pltpu.make_async_copy(win_hbm.at[layer, pl.ds(n * rows, rows), :],
                                               stage_ref.at[n % STAGE_BUFFERS], sem.at[n % STAGE_BUFFERS])
        for n in range(min(STAGE_BUFFERS - 1, n_chunks)):
            copy(n).start()
        for n in range(n_chunks):
            if n + STAGE_BUFFERS - 1 < n_chunks:
                copy(n + STAGE_BUFFERS - 1).start()
            copy(n).wait()
            win_ref[n * rows:(n + 1) * rows, :] = stage_ref[n % STAGE_BUFFERS].astype(BF16)

    @pl.when(tile_in_seq == 0)
    def _():
        lvl_ref[0, H:2 * H, :] = jnp.zeros((H, pw), F32)

    @pl.when(tile_in_seq != 0)
    def _():
        lvl_ref[0, H:2 * H, :] = lvl_ref[0, T + H:T + 2 * H, :]

    hs, bpres, a_s = [], [], []
    for r0 in range(0, T, T // 2):
        hg = _rms(x_ref[r0:r0 + T // 2, :], g1_ref[...]).astype(BF16)
        bpres.append(jnp.dot(hg, win_ref[:, pw:pw + 2 * sw], preferred_element_type=F32))
        a_s.append(jnp.dot(hg, win_ref[:, 0:pw], preferred_element_type=F32))
        hs.append(hg)
    h = jnp.concatenate(hs, axis=0)
    bpre = jnp.concatenate(bpres, axis=0)
    a = jnp.concatenate(a_s, axis=0)
    c2 = pw + 2 * sw
    sbw = q_ref.shape[1]
    half = sbw // 2

    def project(out_ref, col, out_col, scale=None):
        r = jnp.dot(h, win_ref[:, col:col + half], preferred_element_type=F32)
        out_ref[:, out_col:out_col + half] = (r if scale is None else r * scale).astype(BF16)

    r_i = lax.broadcasted_iota(jnp.int32, (CHUNK, CHUNK), 0)
    c_i = lax.broadcasted_iota(jnp.int32, (CHUNK, CHUNK), 1)
    tril = c_i <= r_i
    wm = jnp.concatenate([jnp.where(tril, sgw_ref[hh], 0.0).astype(BF16) for hh in range(SG_HEADS)], axis=1)
    head = lax.broadcasted_iota(jnp.int32, (1, sw), 1) // (sw // SG_HEADS)

    def gate_chunk(c):
        bz = _gelu_tanh(bpre[c * CHUNK:(c + 1) * CHUNK, :])
        vc = _rms(bz[:, sw:2 * sw], sgn_ref[...]).astype(BF16)
        vstack = jnp.concatenate([jnp.where(head == hh, vc, jnp.zeros_like(vc)) for hh in range(SG_HEADS)], axis=0)
        sv = jnp.dot(wm, vstack, preferred_element_type=F32) + sgb_ref[...]
        yab_ref[c * CHUNK:(c + 1) * CHUNK, pw:pw + sw] = (bz[:, 0:sw] * sv).astype(BF16)

    q_scale = float(SB_HD ** -0.5)
    pieces = [(q_ref, c2, 0, q_scale), (q_ref, c2 + half, half, q_scale),
              (v_ref, c2 + 2 * sbw, 0, None), (v_ref, c2 + 2 * sbw + half, half, None)]
    n_chunks = T // CHUNK
    per_piece = -(-n_chunks // len(pieces))
    for p, piece in enumerate(pieces):
        project(*piece)
        for c in range(p * per_piece, min((p + 1) * per_piece, n_chunks)):
            gate_chunk(c)

    lo, n = H, T + H
    lvl_ref[0, 2 * H:2 * H + T, :] = a
    for s, shift in enumerate((1, 2, 4, 8)):
        lvl_ref[s + 1, lo:lo + n, :] = (lvl_ref[s, lo:lo + n, :]
                                        + lvl_ref[s, lo - shift:lo - shift + n, :])
    lane = lax.broadcasted_iota(jnp.int32, (1, pw), 1)
    grp = lane // (pw // len(POOL_WINDOWS))
    win = jnp.where(grp == 0, 2, jnp.where(grp == 1, 4, jnp.where(grp == 2, 8, 16)))
    psum = jnp.where(grp == 0, lvl_ref[1, 2 * H:2 * H + T, :],
                     jnp.where(grp == 1, lvl_ref[2, 2 * H:2 * H + T, :],
                               jnp.where(grp == 2, lvl_ref[3, 2 * H:2 * H + T, :],
                                         lvl_ref[4, 2 * H:2 * H + T, :])))
    pos = tile_in_seq * T + lax.broadcasted_iota(jnp.int32, (T, 1), 0)
    cnt = jnp.minimum(pos + 1, win).astype(F32)
    d = psum / cnt - a
    ya = jnp.dot(d.astype(BF16), poolw_ref[...], preferred_element_type=F32) * pscale_ref[...]
    yab_ref[:, 0:pw] = ya.astype(BF16)

    kt = lax.dot_general(win_ref[:, c2 + sbw:c2 + 2 * sbw], h, (((0,), (1,)), ((), ())),
                         preferred_element_type=F32).astype(BF16)
    for hp in range(kt_ref.shape[0]):
        for c in range(kt_ref.shape[1]):
            kt_ref[hp, c] = kt[hp * LANES:(hp + 1) * LANES, c * ATT_BLOCK:(c + 1) * ATT_BLOCK]


def _inproj(x2, g1, win, poolw, pscale, sgn, sgw, sgb, *, layer, batch, seq):
    N, D = x2.shape
    T = ROW_TILE
    pw, sw = pscale.shape[2], sgn.shape[2]
    of_layer = lambda arr: pl.BlockSpec((None,) + arr.shape[1:], lambda i: (layer, 0, 0))
    sbw = (win.shape[2] - pw - 2 * sw) // 3
    tps = seq // T
    row = lambda i: (i, 0)
    return pl.pallas_call(
        functools.partial(_inproj_kernel, layer=layer, tiles_per_seq=tps),
        grid=(N // T,),
        in_specs=[
            pl.BlockSpec((T, D), row),
            of_layer(g1),
            pl.BlockSpec(memory_space=pl.ANY),
            of_layer(poolw),
            of_layer(pscale),
            of_layer(sgn),
            pl.BlockSpec((None,) + sgw.shape[1:], lambda i: (layer, 0, 0, 0)),
            of_layer(sgb),
        ],
        out_specs=[
            pl.BlockSpec((T, pw + sw), row),
            pl.BlockSpec((T, sbw), row),
            pl.BlockSpec((None, sbw // LANES, T // ATT_BLOCK, LANES, ATT_BLOCK),
                         lambda i: (i // tps, 0, i % tps, 0, 0)),
            pl.BlockSpec((T, sbw), row),
        ],
        out_shape=[
            jax.ShapeDtypeStruct((N, pw + sw), BF16),
            jax.ShapeDtypeStruct((N, sbw), BF16),
            jax.ShapeDtypeStruct((batch, sbw // LANES, seq // ATT_BLOCK, LANES, ATT_BLOCK), BF16),
            jax.ShapeDtypeStruct((N, sbw), BF16),
        ],
        scratch_shapes=[pltpu.VMEM((len(POOL_WINDOWS) + 1, T + 2 * POOL_HALO, pw), F32),
                        pltpu.VMEM(win.shape[1:], BF16),
                        pltpu.VMEM((STAGE_BUFFERS, INPROJ_STAGE_ROWS, win.shape[2]), F32),
                        pltpu.SemaphoreType.DMA((STAGE_BUFFERS,))],
        compiler_params=pltpu.CompilerParams(
            dimension_semantics=("arbitrary",), vmem_limit_bytes=VMEM_LIMIT),
        name="inproj_mixers",
    )(x2, g1, win, poolw, pscale, sgn, sgw, sgb)


def _attn_kernel(q_ref, kt_ref, v_ref, tri_ref, o_ref, acc_ref, carry_ref, gmax_ref, cmax_ref):
    SB = ATT_SUB
    NS = o_ref.shape[0] // SB
    i = pl.program_id(2)
    kb0 = i * NS
    head0_lane = lax.broadcasted_iota(jnp.int32, (1, LANES), 1) < SB_HD
    row = lax.broadcasted_iota(jnp.int32, (SB, LANES), 0)
    col = lax.broadcasted_iota(jnp.int32, (SB, LANES), 1) & (SB - 1)
    cap_diag = jnp.where(col < row, LOGIT_CAP, MASK_LOGIT)
    tri = tri_ref[...]

    def key_rhs(ktile, rolled, half):
        h0, h1 = (ktile, rolled) if half == 0 else (rolled, ktile)
        zero = jnp.zeros((SB_HD, LANES), ktile.dtype)
        return jnp.concatenate([jnp.where(head0_lane, h0[:SB_HD], zero),
                                jnp.where(head0_lane, zero, h1[SB_HD:])], axis=0)

    def value_rhs(kb):
        vrows = v_ref[pl.ds(pl.multiple_of(kb * SB, SB), SB), :]
        zero = jnp.zeros_like(vrows)
        return jnp.concatenate([jnp.where(head0_lane, vrows, zero),
                                jnp.where(head0_lane, zero, vrows)], axis=0)

    def logits_stage(qrows, krhs, diagonal_rows):
        z = jnp.dot(qrows, krhs, preferred_element_type=F32)
        if diagonal_rows == z.shape[0]:
            z = jnp.minimum(z, cap_diag)
        elif diagonal_rows:
            z = jnp.concatenate([jnp.minimum(z[:diagonal_rows], cap_diag),
                                 jnp.minimum(z[diagonal_rows:], LOGIT_CAP)], axis=0)
        else:
            z = jnp.minimum(z, LOGIT_CAP)
        neg_log_1m = jnp.log(1.0 + jnp.exp2(z * LOG2E))
        hi = neg_log_1m.astype(BF16)
        lo = (neg_log_1m - hi.astype(F32)).astype(BF16)
        return z, jnp.concatenate([hi, lo], axis=1)

    def weights_stage(z, hilo):
        ext = jnp.dot(hilo, tri, preferred_element_type=F32)
        return jnp.exp(z + ext[:, :LANES]).astype(BF16), ext[:, LANES:]

    def output_stage(a, vrhs):
        return jnp.dot(a, vrhs, preferred_element_type=F32)

    def tile(qrows, krhs, vrhs):
        a, rs = weights_stage(*logits_stage(qrows, krhs, 0))
        return output_stage(a, vrhs), rs

    def q_rows(first, count):
        return q_ref[first * SB:(first + count) * SB, :]

    ktiles = {}
    for t in range(-1, NS // 2):
        kt_tile = kt_ref[jnp.maximum(kb0 // 2 + t, 0)]
        ktiles[t] = (kt_tile, pltpu.roll(kt_tile, SB, 1))

    def first_stage(m):
        krhs = key_rhs(*ktiles[m // 2], m % 2)
        if m == -1:
            return logits_stage(q_rows(0, 1), krhs, 0)
        if m == NS - 1:
            return logits_stage(q_rows(m, 1), krhs, SB)
        return logits_stage(q_rows(m, 2), krhs, SB)

    blocks = list(range(-1, NS))
    stage1, stage2, parts = {}, {}, []
    lag2, lag3 = ATT_SKEW[0], ATT_SKEW[0] + ATT_SKEW[1]
    for s in range(len(blocks) + lag3):
        if s < len(blocks):
            stage1[s] = first_stage(blocks[s])
        if 0 <= s - lag2 < len(blocks):
            stage2[s - lag2] = weights_stage(*stage1.pop(s - lag2))
        if 0 <= s - lag3 < len(blocks):
            a, rs = stage2.pop(s - lag3)
            vrhs = value_rhs(jnp.maximum(kb0 + blocks[s - lag3], 0))
            parts.append((output_stage(a, vrhs), rs))

    carry_max = None
    for r in range(NS):
        pv_p, rs_p = parts[r] if r == 0 else (parts[r][0][SB:], parts[r][1][SB:])
        pv_d, rs_d = parts[r + 1] if r == NS - 1 else (parts[r + 1][0][:SB], parts[r + 1][1][:SB])
        scale = jnp.exp(rs_d)
        if r == 0:
            scale = jnp.where(kb0 > 0, scale, 0.0)
        acc_ref[r] = pv_d + scale * pv_p
        carry_ref[r] = rs_d + rs_p
        carry_max = rs_d + rs_p if carry_max is None else jnp.maximum(carry_max, rs_d + rs_p)

    def sub_block(r, cmax0):
        qr = q_ref[pl.ds(pl.multiple_of(r * SB, SB), SB), :]

        def cond(st):
            j, cmax = st
            return jnp.logical_and(j >= 0, cmax > -STICK_CUTOFF)

        def body(st):
            j, _ = st
            kt_tile = kt_ref[lax.shift_right_logical(j, 1)]
            rolled = pltpu.roll(kt_tile, SB, 1)
            krhs = jnp.where((j & 1) == 0, key_rhs(kt_tile, rolled, 0), key_rhs(kt_tile, rolled, 1))
            pv, rs = tile(qr, krhs, value_rhs(j))
            carry = carry_ref[r]
            acc_ref[r] += jnp.exp(carry) * pv
            carry_ref[r] = carry + rs
            return j - 1, jnp.max(carry + rs)

        lax.while_loop(cond, body, (kb0 + r - 2, cmax0))

    G = ATT_SCREEN_GROUP
    n_groups = NS // G

    def group(g, _):
        @pl.when(gmax_ref[g] > -STICK_CUTOFF)
        def _():
            for k in range(G):
                cmax_ref[k] = jnp.max(carry_ref[g * G + k])

            def member(k, _):
                @pl.when(cmax_ref[k] > -STICK_CUTOFF)
                def _():
                    sub_block(g * G + k, cmax_ref[k])

                return 0

            lax.fori_loop(0, G, member, 0)

        return 0

    @pl.when(jnp.max(carry_max) > -STICK_CUTOFF)
    def _():
        for g in range(n_groups):
            gmax = carry_ref[g * G]
            for k in range(1, G):
                gmax = jnp.maximum(gmax, carry_ref[g * G + k])
            gmax_ref[g] = jnp.max(gmax)
        lax.fori_loop(0, n_groups, group, 0)

    for r in range(NS):
        o_ref[r * SB:(r + 1) * SB, :] = acc_ref[r].astype(o_ref.dtype)


def _stick_matrix():
    key = np.arange(LANES) % ATT_SUB
    head = np.arange(LANES) // ATT_SUB
    same = head[:, None] == head[None, :]
    suffix = np.logical_and(same, key[:, None] >= key[None, :])
    half = -np.concatenate([suffix, same], axis=1).astype(np.float32)
    return jnp.asarray(np.concatenate([half, half], axis=0), BF16)


def _attention(q, kt, v, *, batch, seq):
    N, W = q.shape
    R = ATT_ROWS
    q3, v3 = q.reshape(batch, seq, W), v.reshape(batch, seq, W)
    tri = _stick_matrix()
    resident = lambda b, hp, i: (b, 0, hp)
    out = pl.pallas_call(
        _attn_kernel,
        grid=(batch, W // LANES, seq // R),
        in_specs=[
            pl.BlockSpec((None, R, LANES), lambda b, hp, i: (b, i, hp)),
            pl.BlockSpec((None, None, seq // ATT_BLOCK, LANES, ATT_BLOCK), lambda b, hp, i: (b, hp, 0, 0, 0)),
            pl.BlockSpec((None, seq, LANES), resident),
            pl.BlockSpec(tri.shape, lambda b, hp, i: (0, 0)),
        ],
        out_specs=pl.BlockSpec((None, R, LANES), lambda b, hp, i: (b, i, hp)),
        out_shape=jax.ShapeDtypeStruct((batch, seq, W), BF16),
        scratch_shapes=[pltpu.VMEM((R // ATT_SUB, ATT_SUB, LANES), F32),
                        pltpu.VMEM((R // ATT_SUB, ATT_SUB, LANES), F32),
                        pltpu.SMEM((R // ATT_SUB // ATT_SCREEN_GROUP,), F32),
                        pltpu.SMEM((ATT_SCREEN_GROUP,), F32)],
        compiler_params=pltpu.CompilerParams(
            dimension_semantics=("arbitrary", "arbitrary", "arbitrary"),
            vmem_limit_bytes=VMEM_LIMIT),
        name="stickbreak_attn",
    )(q3, kt, v3, tri)
    return out.reshape(N, W)


def _mlp_kernel(x_ref, yab_ref, yc_ref, g2_ref, gf_ref, wo_hbm, wup_hbm, wdn_hbm, o_ref,
                wo_ref, wup_ref, wdn_ref, stage_ref, sem, *, layer, final_norm, ff_chunk):
    D = x_ref.shape[1]

    @pl.when(pl.program_id(0) == 0)
    def _():
        chunks = []
        for src, dst in ((wo_hbm, wo_ref), (wup_hbm, wup_ref), (wdn_hbm, wdn_ref)):
            for col in range(0, dst.shape[1], D):
                for r0 in range(0, dst.shape[0], MLP_STAGE_ROWS):
                    chunks.append((src.at[layer, pl.ds(r0, MLP_STAGE_ROWS), pl.ds(col, D)],
                                   dst.at[pl.ds(r0, MLP_STAGE_ROWS), pl.ds(col, D)]))
        copy = lambda n: pltpu.make_async_copy(chunks[n][0], stage_ref.at[n % STAGE_BUFFERS],
                                               sem.at[n % STAGE_BUFFERS])
        for n in range(min(STAGE_BUFFERS - 1, len(chunks))):
            copy(n).start()
        for n in range(len(chunks)):
            if n + STAGE_BUFFERS - 1 < len(chunks):
                copy(n + STAGE_BUFFERS - 1).start()
            copy(n).wait()
            chunks[n][1][...] = stage_ref[n % STAGE_BUFFERS].astype(BF16)

    nab = yab_ref.shape[1]
    T = x_ref.shape[0]
    rows = T // MLP_ROW_GROUPS
    x1s, hs = [], []
    for g in range(MLP_ROW_GROUPS):
        r = slice(g * rows, (g + 1) * rows)
        x1 = (x_ref[r, :]
              + jnp.dot(yab_ref[r, :], wo_ref[0:nab, :], preferred_element_type=F32)
              + jnp.dot(yc_ref[r, :], wo_ref[nab:, :], preferred_element_type=F32))
        x1s.append(x1)
        hs.append(_rms(x1, g2_ref[...]).astype(BF16))
    for g in range(MLP_ROW_GROUPS):
        acc = None
        for c in range(wup_ref.shape[1] // ff_chunk):
            up = jnp.dot(hs[g], wup_ref[:, c * ff_chunk:(c + 1) * ff_chunk], preferred_element_type=F32)
            act = jnp.square(jnp.maximum(up, 0.0)).astype(BF16)
            dn = jnp.dot(act, wdn_ref[c * ff_chunk:(c + 1) * ff_chunk, :], preferred_element_type=F32)
            acc = dn if acc is None else acc + dn
        acc = acc + x1s[g]
        if final_norm:
            acc = _rms(acc, gf_ref[...])
        o_ref[g * rows:(g + 1) * rows, :] = acc


def _mlp(x2, yab, yc, wo, g2, wup, wdn, gf, *, layer, final_norm):
    N, D = x2.shape
    T = ROW_TILE
    const = lambda i: (0, 0)
    row = lambda i: (i, 0)
    hbm = pl.BlockSpec(memory_space=pl.ANY)
    return pl.pallas_call(
        functools.partial(_mlp_kernel, layer=layer, final_norm=final_norm, ff_chunk=1024),
        grid=(N // T,),
        in_specs=[
            pl.BlockSpec((T, D), row),
            pl.BlockSpec((T, yab.shape[1]), row),
            pl.BlockSpec((T, yc.shape[1]), row),
            pl.BlockSpec((None, 1, D), lambda i: (layer, 0, 0)),
            pl.BlockSpec((1, D), const),
            hbm, hbm, hbm,
        ],
        out_specs=pl.BlockSpec((T, D), row),
        out_shape=jax.ShapeDtypeStruct((N, D), F32),
        scratch_shapes=[pltpu.VMEM(wo.shape[1:], BF16), pltpu.VMEM(wup.shape[1:], BF16),
                        pltpu.VMEM(wdn.shape[1:], BF16), pltpu.VMEM((STAGE_BUFFERS, MLP_STAGE_ROWS, D), F32),
                        pltpu.SemaphoreType.DMA((STAGE_BUFFERS,))],
        compiler_params=pltpu.CompilerParams(
            dimension_semantics=("arbitrary",), vmem_limit_bytes=VMEM_LIMIT),
        name="outproj_mlp",
    )(x2, yab, yc, g2, gf, wo, wup, wdn)


def kernel(x, norm1, w_in, pool_w, pool_scale, sg_norm, sg_w, sg_b, w_out, norm2, w_up, w_down, final_norm):
    B, S, D = x.shape
    depth = norm1.shape[0]
    n_grp = pool_w.shape[1]
    pw, sw = pool_scale.shape[1], sg_norm.shape[1]
    sbw = (w_in.shape[2] - pw - 2 * sw) // 3
    assert S % ROW_TILE == 0 and S % ATT_ROWS == 0 and ROW_TILE % CHUNK == 0 and ROW_TILE % ATT_BLOCK == 0
    assert sg_w.shape[1] == SG_HEADS and sg_w.shape[2] == CHUNK and n_grp == len(POOL_WINDOWS)
    assert sbw % LANES == 0 and ATT_BLOCK == LANES == 2 * ATT_SUB == 2 * SB_HD and (ATT_ROWS // ATT_SUB) % max(2, ATT_SCREEN_GROUP) == 0

    poolw_all = jnp.einsum('lgcd,gh->lgchd', pool_w, jnp.eye(n_grp, dtype=pool_w.dtype)).reshape(depth, pw, pw).astype(BF16)
    sgb_all = jnp.repeat(jnp.swapaxes(sg_b, 1, 2), sw // SG_HEADS, axis=2)
    g1_all, g2_all = norm1[:, None, :], norm2[:, None, :]
    pscale_all, sgn_all = pool_scale[:, None, :], sg_norm[:, None, :]
    x2 = x.reshape(B * S, D)
    for l in range(depth):
        yab, q, kt, v = _inproj(x2, g1_all, w_in, poolw_all, pscale_all, sgn_all, sg_w, sgb_all,
                                layer=l, batch=B, seq=S)
        yc = _attention(q, kt, v, batch=B, seq=S)
        x2 = _mlp(x2, yab, yc, w_out, g2_all, w_up, w_down, final_norm[None],
                  layer=l, final_norm=(l == depth - 1))
    return x2.reshape(B, S, D)
```
